```python
import jax, jax.numpy as jnp
from jax import lax
import numpy as np

D_MODEL = 1024
BATCH = 8
SEQ = 4096
DEPTH = 2
DEC_BATCH = 128
DEC_SEQ = 4
PAST_LEN = 16384
PAGE_SIZE = 128

WINDOW = 128
HEAD_DIM = 64
N_Q_HEADS = D_MODEL // HEAD_DIM
N_KV_HEADS = N_Q_HEADS // 4
GQA_GROUP = N_Q_HEADS // N_KV_HEADS
ROT_DIM = HEAD_DIM // 4
ROPE_THETA = 500000.0
R_HEAD_DIM = 64
R_HEADS = D_MODEL // R_HEAD_DIM
R_WIDTH = R_HEADS * R_HEAD_DIM
DECAY_LORA = 64
ICLR_LORA = 64
GATE_LORA = 160
LNX_EPS = 64e-5
Q_COLS = N_Q_HEADS * HEAD_DIM
KV_COLS = N_KV_HEADS * HEAD_DIM
ATTN_COLS = Q_COLS + 2 * KV_COLS
SHIFT_COLS = 3 * R_WIDTH + DECAY_LORA + ICLR_LORA + GATE_LORA
GATE_COLS = 2 * D_MODEL
N_IN = ATTN_COLS + SHIFT_COLS + GATE_COLS
FFN_DIM = 2816
N_EXPERTS = 8
TOP_K = 2
EXPERT_DIM = 3584
RMS_EPS = 1e-5

kernel_name = "hybrid_swa_sink_rwkv7_moe_step"

F32 = jnp.float32


def _rmsnorm(x, g):
    x32 = x.astype(F32)
    y = x32 * lax.rsqrt(jnp.mean(x32 * x32, -1, keepdims=True) + RMS_EPS) * g.astype(F32)
    return y.astype(x.dtype)


def _rope_tables(start, T):
    pos = (start + jnp.arange(T)).astype(F32)
    inv = ROPE_THETA ** (-jnp.arange(0, ROT_DIM, 2, dtype=F32) / ROT_DIM)
    ang = pos[:, None] * inv[None, :]
    return jnp.cos(ang), jnp.sin(ang)


def _rope(x, cos, sin):
    half = ROT_DIM // 2
    x1 = x[..., :half].astype(F32)
    x2 = x[..., half:ROT_DIM].astype(F32)
    c, s = cos[:, None, :], sin[:, None, :]
    rot = jnp.concatenate([x1 * c - x2 * s, x2 * c + x1 * s], -1).astype(x.dtype)
    return jnp.concatenate([rot, x[..., ROT_DIM:]], -1)


def _sink_attend(q, k, v, mask, sinks):
    s = jnp.einsum('...qkgd,...skd->...kgqs', q, k).astype(F32) * (HEAD_DIM ** -0.5)
    s = jnp.where(mask, s, -jnp.inf)
    sink = sinks.astype(F32).reshape(N_KV_HEADS, GQA_GROUP)[:, :, None, None]
    m = jnp.maximum(jnp.max(s, -1, keepdims=True), sink)
    p = jnp.exp(s - m)
    probs = p / (jnp.sum(p, -1, keepdims=True) + jnp.exp(sink - m))
    return jnp.einsum('...kgqs,...skd->...qkgd', probs.astype(v.dtype), v)


def _attn_prompt(q, k, v, sinks):
    B, T = q.shape[:2]
    nb = T // WINDOW
    qb = q.reshape(B, nb, WINDOW, N_KV_HEADS, GQA_GROUP, HEAD_DIM)
    pad = jnp.zeros((B, WINDOW, N_KV_HEADS, HEAD_DIM), k.dtype)
    kp = jnp.concatenate([pad, k], 1).reshape(B, nb + 1, WINDOW, N_KV_HEADS, HEAD_DIM)
    vp = jnp.concatenate([pad.astype(v.dtype), v], 1).reshape(B, nb + 1, WINDOW, N_KV_HEADS, HEAD_DIM)
    kb = jnp.concatenate([kp[:, :-1], kp[:, 1:]], 2)
    vb = jnp.concatenate([vp[:, :-1], vp[:, 1:]], 2)
    i = jnp.arange(WINDOW)[:, None]
    j = jnp.arange(2 * WINDOW)[None, :]
    blk = jnp.arange(nb)[:, None, None]
    diff = i - j + WINDOW
    mask = (diff >= 0) & (diff < WINDOW) & (blk * WINDOW + j - WINDOW >= 0)
    o = _sink_attend(qb, kb, vb, mask[None, :, None, None], sinks)
    return o.reshape(B, T, Q_COLS)


def _attn_sample(q, k, v, k_buf, v_buf, sinks):
    B, Tn = q.shape[:2]
    n_buf = k_buf.shape[1]
    kc = jnp.concatenate([k_buf.astype(k.dtype), k], 1)
    vc = jnp.concatenate([v_buf.astype(v.dtype), v], 1)
    qpos = PAST_LEN + jnp.arange(Tn)
    kpos = PAST_LEN - n_buf + jnp.arange(n_buf + Tn)
    diff = qpos[:, None] - kpos[None, :]
    mask = (diff >= 0) & (diff < WINDOW) & (kpos[None, :] >= 0)
    qh = q.reshape(B, Tn, N_KV_HEADS, GQA_GROUP, HEAD_DIM)
    o = _sink_attend(qh, kc, vc, mask[None, None, None], sinks)
    return o.reshape(B, Tn, Q_COLS), kc[:, -n_buf:], vc[:, -n_buf:]


def _wkv_scan(S0, r, w, k, v, a_, b_):
    def step(S, inp):
        r_t, w_t, k_t, v_t, a_t, b_t = inp
        sa = jnp.einsum('bhij,bhj->bhi', S, a_t)
        S = S * w_t[:, :, None, :] + sa[..., None] * b_t[:, :, None, :] + v_t[..., None] * k_t[:, :, None, :]
        return S, jnp.einsum('bhij,bhj->bhi', S, r_t)
    xs = tuple(jnp.moveaxis(t, 1, 0) for t in (r, w, k, v, a_, b_))
    S, y = lax.scan(step, S0, xs)
    return S, jnp.moveaxis(y, 0, 1)


def _rwkv7(p, shift_prev, S0, mu, w0, w_up, a0, a_up, g_up, k_k, k_a, r_k, lnx_g, lnx_b):
    B, T = p.shape[:2]
    prev = jnp.concatenate([shift_prev[:, None].astype(p.dtype), p[:, :-1]], 1)
    xs = p + (prev - p) * mu
    r, k, v, wd, ad, gd = jnp.split(xs, [R_WIDTH, 2 * R_WIDTH, 3 * R_WIDTH,
                                        3 * R_WIDTH + DECAY_LORA,
                                        3 * R_WIDTH + DECAY_LORA + ICLR_LORA], axis=-1)
    w_log = -jax.nn.softplus(-(w0 + jnp.tanh(wd) @ w_up).astype(F32)) - 0.5
    decay = jnp.exp(-jnp.exp(w_log))
    a = jax.nn.sigmoid((a0 + ad @ a_up).astype(F32))
    g = jax.nn.sigmoid(gd) @ g_up
    heads = lambda t: t.astype(F32).reshape(B, T, R_HEADS, R_HEAD_DIM)
    kk = heads(k * k_k)
    kk = kk / jnp.maximum(jnp.sqrt(jnp.sum(kk * kk, -1, keepdims=True)), 1e-12)
    a_h = heads(a)
    k_h = heads(k.astype(F32) * (1.0 + (a - 1.0) * k_a.astype(F32)))
    r_h, v_h = heads(r), heads(v)
    S, y = _wkv_scan(S0.astype(F32), r_h, heads(decay), k_h, v_h, -kk, kk * a_h)
    mean = jnp.mean(y, -1, keepdims=True)
    var = jnp.mean(jnp.square(y - mean), -1, keepdims=True)
    yn = ((y - mean) * lax.rsqrt(var + LNX_EPS)).reshape(B, T, R_WIDTH)
    yn = yn * lnx_g.astype(F32) + lnx_b.astype(F32)
    bonus = (jnp.sum(r_h * k_h * r_k.astype(F32), -1, keepdims=True) * v_h).reshape(B, T, R_WIDTH)
    out = ((yn + bonus) * g.astype(F32)).astype(p.dtype)
    return out, S, p[:, -1]


def _swiglu(x, wg, wu, wd):
    return (jax.nn.silu(x @ wg) * (x @ wu)) @ wd


def _moe(h, router, wg, wu, wd):
    B, T, D = h.shape
    t = h.reshape(B * T, D)
    logits = (t @ router).astype(F32)
    top_v, top_i = lax.top_k(logits, TOP_K)
    wts = jax.nn.softmax(top_v, -1)
    gate = jnp.sum(jax.nn.one_hot(top_i, N_EXPERTS, dtype=F32) * wts[..., None], 1)
    out = jnp.zeros((B * T, D), F32)
    for e in range(N_EXPERTS):
        out = out + gate[:, e:e + 1] * _swiglu(t, wg[e], wu[e], wd[e]).astype(F32)
    return out.astype(h.dtype).reshape(B, T, D)


def setup_inputs(seed: int = 0) -> dict:
    key = jax.random.key(seed)
    ks = iter(jax.random.split(key, 40))
    nrm = lambda shape, scale: jax.random.normal(next(ks), shape, F32) * scale
    uni = lambda shape, lo, hi: jax.random.uniform(next(ks), shape, F32, lo, hi)
    n_buf = min(WINDOW, PAST_LEN)
    n_dense = (DEPTH + 1) // 2
    n_moe = DEPTH // 2
    D = D_MODEL
    return {
        "x_prompt": nrm((BATCH, SEQ, D), 1.0),
        "x_sample": nrm((DEC_BATCH, DEC_SEQ, D), 1.0),
        "cache_k_win": nrm((DEPTH, DEC_BATCH, n_buf, N_KV_HEADS, HEAD_DIM), 1.0),
        "cache_v_win": nrm((DEPTH, DEC_BATCH, n_buf, N_KV_HEADS, HEAD_DIM), 1.0),
        "state_wkv": nrm((DEPTH, DEC_BATCH, R_HEADS, R_HEAD_DIM, R_HEAD_DIM), 0.3),
        "state_shift": nrm((DEPTH, DEC_BATCH, SHIFT_COLS), 1.0),
        "norm_mix_g": 1.0 + nrm((DEPTH, D), 0.02),
        "w_in": nrm((DEPTH, D, N_IN), D ** -0.5),
        "w_out": nrm((DEPTH, D, D), D ** -0.5),
        "attn_sinks": nrm((DEPTH, N_Q_HEADS), 1.0),
        "shift_mu": uni((DEPTH, SHIFT_COLS), 0.0, 1.0),
        "decay_w0": uni((DEPTH, R_WIDTH), -5.0, 0.0),
        "decay_up": nrm((DEPTH, DECAY_LORA, R_WIDTH), DECAY_LORA ** -0.5),
        "iclr_a0": nrm((DEPTH, R_WIDTH), 0.1),
        "iclr_up": nrm((DEPTH, ICLR_LORA, R_WIDTH), ICLR_LORA ** -0.5),
        "gate_up": nrm((DEPTH, GATE_LORA, R_WIDTH), GATE_LORA ** -0.5),
        "key_kk": 0.85 + nrm((DEPTH, R_WIDTH), 0.02),
        "key_ka": 1.0 + nrm((DEPTH, R_WIDTH), 0.02),
        "bonus_rk": nrm((DEPTH, R_HEADS, R_HEAD_DIM), 0.1),
        "lnx_g": 1.0 + nrm((DEPTH, R_WIDTH), 0.02),
        "lnx_b": nrm((DEPTH, R_WIDTH), 0.01),
        "norm_ffn_g": 1.0 + nrm((DEPTH, D), 0.02),
        "ffn_w_gate": nrm((n_dense, D, FFN_DIM), D ** -0.5),
        "ffn_w_up": nrm((n_dense, D, FFN_DIM), D ** -0.5),
        "ffn_w_down": nrm((n_dense, FFN_DIM, D), FFN_DIM ** -0.5),
        "moe_router": nrm((n_moe, D, N_EXPERTS), D ** -0.5),
        "moe_w_gate": nrm((n_moe, N_EXPERTS, D, EXPERT_DIM), D ** -0.5),
        "moe_w_up": nrm((n_moe, N_EXPERTS, D, EXPERT_DIM), D ** -0.5),
        "moe_w_down": nrm((n_moe, N_EXPERTS, EXPERT_DIM, D), EXPERT_DIM ** -0.5),
        "norm_final_g": 1.0 + nrm((D,), 0.02),
    }


def reference(x_prompt, x_sample, cache_k_win, cache_v_win, state_wkv, state_shift,
              norm_mix_g, w_in, w_out, attn_sinks, shift_mu, decay_w0, decay_up, iclr_a0,
              iclr_up, gate_up, key_kk, key_ka, bonus_rk, lnx_g, lnx_b, norm_ffn_g,
              ffn_w_gate, ffn_w_up, ffn_w_down, moe_router, moe_w_gate, moe_w_up, moe_w_down,
              norm_final_g):
    def layer(x, l, past):
        B, T = x.shape[:2]
        h = _rmsnorm(x, norm_mix_g[l])
        p = h @ w_in[l]
        q, k, v, pr, gates = jnp.split(p, [Q_COLS, Q_COLS + KV_COLS, ATTN_COLS,
                                           ATTN_COLS + SHIFT_COLS], axis=-1)
        start = 0 if past is None else PAST_LEN
        cos, sin = _rope_tables(start, T)
        q = _rope(q.reshape(B, T, N_Q_HEADS, HEAD_DIM), cos, sin)
        k = _rope(k.reshape(B, T, N_KV_HEADS, HEAD_DIM), cos, sin)
        v = v.reshape(B, T, N_KV_HEADS, HEAD_DIM)
        if past is None:
            att = _attn_prompt(q, k, v, attn_sinks[l])
            n_buf = min(WINDOW, T)
            kb, vb = k[:, -n_buf:], v[:, -n_buf:]
            S0 = jnp.zeros((B, R_HEADS, R_HEAD_DIM, R_HEAD_DIM), F32)
            sh0 = jnp.zeros((B, SHIFT_COLS), p.dtype)
        else:
            k_buf, v_buf, S0, sh0 = past
            att, kb, vb = _attn_sample(q, k, v, k_buf, v_buf, attn_sinks[l])
        rw, S, sh = _rwkv7(pr, sh0, S0, shift_mu[l], decay_w0[l], decay_up[l], iclr_a0[l],
                           iclr_up[l], gate_up[l], key_kk[l], key_ka[l], bonus_rk[l],
                           lnx_g[l], lnx_b[l])
        g_att, g_rw = jnp.split(jax.nn.sigmoid(gates), 2, axis=-1)
        x = x + (g_att * att + g_rw * rw) @ w_out[l]
        h2 = _rmsnorm(x, norm_ffn_g[l])
        if l % 2 == 0:
            f = _swiglu(h2, ffn_w_gate[l // 2], ffn_w_up[l // 2], ffn_w_down[l // 2])
        else:
            f = _moe(h2, moe_router[l // 2], moe_w_gate[l // 2], moe_w_up[l // 2], moe_w_down[l // 2])
        return x + f, (kb, vb, S, sh)

    yp, ys = x_prompt, x_sample
    new_p, new_s = [], []
    for l in range(DEPTH):
        yp, st_p = layer(yp, l, None)
        new_p.append(st_p)
        ys, st_s = layer(ys, l, (cache_k_win[l], cache_v_win[l], state_wkv[l], state_shift[l]))
        new_s.append(st_s)
    yp = _rmsnorm(yp, norm_final_g)
    ys = _rmsnorm(ys, norm_final_g)
    stk = lambda sts, i: jnp.stack([s[i] for s in sts])
    return (yp, ys,
            stk(new_p, 0), stk(new_p, 1), stk(new_p, 2), stk(new_p, 3),
            stk(new_s, 0), stk(new_s, 1), stk(new_s, 2), stk(new_s, 3))
```

```python
import functools

import jax
import jax.numpy as jnp
from jax import lax
from jax.experimental import pallas as pl
from jax.experimental.pallas import tpu as pltpu

F32 = jnp.float32
BF16 = jnp.bfloat16
HIGHEST = lax.Precision.HIGHEST

LANES = 128
VMEM_LIMIT = 56 * 1024 * 1024

D = 1024
HD = 64
N_Q = 16
N_KV = 4
ROT = 16
ROPE_THETA = 500000.0
WINDOW = 128
RMS_EPS = 1e-5
LNX_EPS = 64e-5
N_PAIR = D // LANES
LORA_W, LORA_A, LORA_G = 64, 64, 160
LORA_PAD = 512
G_PAD = 256

C_RKV = 0
C_Q = 3072
C_GATE = 4096
C_K = 6144
C_V = 6400
C_LORA = 6656
N_COLS = 7168
TN_IN = 1024
Q_TILE = C_Q // TN_IN
KV_TILE = C_K // TN_IN

RWKV_CHUNK = 64


PAST_LEN = 16384


def _pick(n, cands):
    return next(c for c in cands if n % c == 0)


def _cparams(sem, **kw):
    return pltpu.CompilerParams(dimension_semantics=sem, vmem_limit_bytes=VMEM_LIMIT, **kw)


def _rms(x, g):
    return x * lax.rsqrt(jnp.mean(x * x, -1, keepdims=True) + RMS_EPS) * g


def _sigmoid(x):
    return 1.0 / (1.0 + jnp.exp(-x))


def _rope_chunk(a, c, sa, sb):
    return a * c + pltpu.roll(a, LANES - ROT // 2, 1) * sa + pltpu.roll(a, ROT // 2, 1) * sb


def _inproj_kernel(x_ref, g_ref, w_ref, c_ref, sa_ref, sb_ref, o_ref, h_ref):
    j = pl.program_id(1)

    @pl.when(j == 0)
    def _():
        h_ref[...] = _rms(x_ref[...], g_ref[...]).astype(BF16)

    acc = jnp.dot(h_ref[...], w_ref[...], preferred_element_type=F32)

    def roped(n_chunks):
        c, sa, sb = c_ref[...], sa_ref[...], sb_ref[...]
        parts = [_rope_chunk(acc[:, k * LANES:(k + 1) * LANES], c, sa, sb) for k in range(n_chunks)]
        if n_chunks * LANES < TN_IN:
            parts.append(acc[:, n_chunks * LANES:])
        return jnp.concatenate(parts, axis=1)

    @pl.when(j == Q_TILE)
    def _():
        o_ref[...] = roped(TN_IN // LANES)

    @pl.when(j == KV_TILE)
    def _():
        o_ref[...] = roped(N_KV * HD // LANES)

    @pl.when((j != Q_TILE) & (j != KV_TILE))
    def _():
        o_ref[...] = acc


def _inproj(x, g, w, rope_c, rope_sa, rope_sb, tm):
    n = x.shape[0]
    return pl.pallas_call(
        _inproj_kernel,
        out_shape=jax.ShapeDtypeStruct((n, N_COLS), F32),
        grid=(n // tm, N_COLS // TN_IN),
        in_specs=[
            pl.BlockSpec((tm, D), lambda i, j: (i, 0)),
            pl.BlockSpec((1, D), lambda i, j: (0, 0)),
            pl.BlockSpec((D, TN_IN), lambda i, j: (0, j)),
            pl.BlockSpec((tm, LANES), lambda i, j: (i, 0)),
            pl.BlockSpec((tm, LANES), lambda i, j: (i, 0)),
            pl.BlockSpec((tm, LANES), lambda i, j: (i, 0)),
        ],
        out_specs=pl.BlockSpec((tm, TN_IN), lambda i, j: (i, j)),
        scratch_shapes=[pltpu.VMEM((tm, D), BF16)],
        compiler_params=_cparams(("parallel", "arbitrary")),
        name="inproj",
    )(x, g, w, rope_c, rope_sa, rope_sb)


def _sink_softmax(s, mask, sink):
    s = jnp.where(mask, s * (HD ** -0.5), -jnp.inf)
    m = jnp.maximum(jnp.max(s, -1, keepdims=True), sink)
    p = jnp.exp(s - m)
    return p / (jnp.sum(p, -1, keepdims=True) + jnp.exp(sink - m))


def _dot_nt(a, b, **kw):
    return lax.dot_general(a, b, (((1,), (1,)), ((), ())), preferred_element_type=F32, **kw)


def _dot_tn(a, b, **kw):
    return lax.dot_general(a, b, (((0,), (0,)), ((), ())), preferred_element_type=F32, **kw)


def _attend_group(q2, kc, vc, half, mask, sink_a, sink_b):
    lo = lax.broadcasted_iota(jnp.int32, kc.shape, 1) < HD
    k_sw = pltpu.roll(kc, HD, 1)
    v_sw = pltpu.roll(vc, HD, 1)
    k_lo, k_hi = (kc, k_sw) if half == 0 else (k_sw, kc)
    v_lo, v_hi = (vc, v_sw) if half == 0 else (v_sw, vc)
    ka = jnp.where(lo, k_lo, 0.0).astype(BF16)
    kb = jnp.where(lo, 0.0, k_hi).astype(BF16)
    va = jnp.where(lo, v_lo, 0.0).astype(BF16)
    vb = jnp.where(lo, 0.0, v_hi).astype(BF16)
    pa = _sink_softmax(_dot_nt(q2, ka), mask, sink_a).astype(BF16)
    pb = _sink_softmax(_dot_nt(q2, kb), mask, sink_b).astype(BF16)
    return (jnp.dot(pa, va, preferred_element_type=F32)
            + jnp.dot(pb, vb, preferred_element_type=F32))


def _attn_prompt_kernel(sink_ref, q_ref, kp_ref, kc_ref, vp_ref, vc_ref, o_ref):
    blk = pl.program_id(1)
    w = WINDOW
    q = q_ref[...].astype(BF16)
    k = jnp.concatenate([kp_ref[...], kc_ref[...]], axis=0)
    v = jnp.concatenate([vp_ref[...], vc_ref[...]], axis=0)
    qi = lax.broadcasted_iota(jnp.int32, (2 * w, 2 * w), 0) & (w - 1)
    kj = lax.broadcasted_iota(jnp.int32, (2 * w, 2 * w), 1)
    top = lax.broadcasted_iota(jnp.int32, (2 * w, 1), 0) < w
    mask = (kj > qi) & (kj <= qi + w) & ((kj >= w) | (blk > 0))
    outs = []
    for g in range(N_KV):
        ch, half = divmod(g, 2)
        q2 = jnp.concatenate([q[:, (2 * g) * LANES:(2 * g + 1) * LANES],
                              q[:, (2 * g + 1) * LANES:(2 * g + 2) * LANES]], axis=0)
        sink_a = jnp.where(top, sink_ref[4 * g], sink_ref[4 * g + 2])
        sink_b = jnp.where(top, sink_ref[4 * g + 1], sink_ref[4 * g + 3])
        o = _attend_group(q2, k[:, ch * LANES:(ch + 1) * LANES], v[:, ch * LANES:(ch + 1) * LANES],
                          half, mask, sink_a, sink_b)
        outs += [o[:w], o[w:]]
    o_ref[...] = jnp.concatenate(outs, axis=1).astype(o_ref.dtype)


def _attn_prompt(p_all, sinks, batch, seq):
    nb = seq // WINDOW
    kcol, vcol = C_K // (N_KV * HD), C_V // (N_KV * HD)
    cur = lambda b, i: b * nb + i
    prev = lambda b, i: b * nb + jnp.maximum(i - 1, 0)
    return pl.pallas_call(
        _attn_prompt_kernel,
        out_shape=jax.ShapeDtypeStruct((batch * seq, D), BF16),
        grid=(batch, nb),
        in_specs=[
            pl.BlockSpec(memory_space=pltpu.SMEM),
            pl.BlockSpec((WINDOW, D), lambda b, i: (cur(b, i), C_Q // D)),
            pl.BlockSpec((WINDOW, N_KV * HD), lambda b, i: (prev(b, i), kcol)),
            pl.BlockSpec((WINDOW, N_KV * HD), lambda b, i: (cur(b, i), kcol)),
            pl.BlockSpec((WINDOW, N_KV * HD), lambda b, i: (prev(b, i), vcol)),
            pl.BlockSpec((WINDOW, N_KV * HD), lambda b, i: (cur(b, i), vcol)),
        ],
        out_specs=pl.BlockSpec((WINDOW, D), lambda b, i: (cur(b, i), 0)),
        compiler_params=_cparams(("parallel", "arbitrary")),
        name="attn_prompt",
    )(sinks, p_all, p_all, p_all, p_all, p_all)


ATT_S_BT = 8
T_PAD = 8


def _attn_sample_kernel(sink_ref, q_ref, kn_ref, vn_ref, kc_ref, vc_ref, o_ref):
    tn = q_ref.shape[1]
    nbuf = kc_ref.shape[1]
    rows = 2 * tn
    keys = nbuf + T_PAD
    r = lax.broadcasted_iota(jnp.int32, (rows, keys), 0)
    t = jnp.where(r >= tn, r - tn, r)
    kj = lax.broadcasted_iota(jnp.int32, (rows, keys), 1)
    mask = (kj > t + (nbuf - WINDOW)) & (kj <= t + nbuf)
    top = lax.broadcasted_iota(jnp.int32, (rows, 1), 0) < tn
    zpad = jnp.zeros((T_PAD - tn, N_KV * HD), F32)
    for b in range(ATT_S_BT):
        q = q_ref[b].astype(BF16)
        k = jnp.concatenate([kc_ref[b], kn_ref[b], zpad], axis=0)
        v = jnp.concatenate([vc_ref[b], vn_ref[b], zpad], axis=0)
        outs = []
        for g in range(N_KV):
            ch, half = divmod(g, 2)
            q2 = jnp.concatenate([q[:, (2 * g) * LANES:(2 * g + 1) * LANES],
                                  q[:, (2 * g + 1) * LANES:(2 * g + 2) * LANES]], axis=0)
            sink_a = jnp.where(top, sink_ref[4 * g], sink_ref[4 * g + 2])
            sink_b = jnp.where(top, sink_ref[4 * g + 1], sink_ref[4 * g + 3])
            o = _attend_group(q2, k[:, ch * LANES:(ch + 1) * LANES], v[:, ch * LANES:(ch + 1) * LANES],
                              half, mask, sink_a, sink_b)
            outs += [o[:tn], o[tn:]]
        o_ref[b] = jnp.concatenate(outs, axis=1).astype(o_ref.dtype)


def _attn_sample(q, k_new, v_new, k_cache, v_cache, sinks):
    bsz, tn, _ = q.shape
    nbuf = k_cache.shape[1]
    kvw = N_KV * HD
    blk = lambda w_, r_: pl.BlockSpec((ATT_S_BT, r_, w_), lambda i: (i, 0, 0))
    return pl.pallas_call(
        _attn_sample_kernel,
        out_shape=jax.ShapeDtypeStruct((bsz, tn, D), BF16),
        grid=(bsz // ATT_S_BT,),
        in_specs=[pl.BlockSpec(memory_space=pltpu.SMEM),
                  blk(D, tn), blk(kvw, tn), blk(kvw, tn), blk(kvw, nbuf), blk(kvw, nbuf)],
        out_specs=blk(D, tn),
        compiler_params=_cparams(("parallel",)),
        name="attn_sample",
    )(sinks, q, k_new, v_new, k_cache, v_cache)


def _to_pm(x):
    return jnp.concatenate([x[:, p * LANES:(p + 1) * LANES] for p in range(N_PAIR)], axis=0)


def _from_pm(x):
    r = x.shape[0] // N_PAIR
    return jnp.concatenate([x[p * r:(p + 1) * r] for p in range(N_PAIR)], axis=1)


def _param_pm(v, r):
    return jnp.concatenate(
        [jnp.broadcast_to(v[:, p * LANES:(p + 1) * LANES], (r, LANES)) for p in range(N_PAIR)], axis=0)


def _head_sum(x, ones_bd):
    hi = x.astype(BF16)
    lo = (x - hi.astype(F32)).astype(BF16)
    return (jnp.dot(hi, ones_bd, preferred_element_type=F32)
            + jnp.dot(lo, ones_bd, preferred_element_type=F32))


def _ones_bd():
    r = lax.broadcasted_iota(jnp.int32, (LANES, LANES), 0) // HD
    c = lax.broadcasted_iota(jnp.int32, (LANES, LANES), 1) // HD
    return jnp.where(r == c, 1.0, 0.0).astype(BF16)


def _softplus(z):
    return jnp.maximum(z, 0.0) + jnp.log(1.0 + jnp.exp(-jnp.abs(z)))


def _rwkv_pre(p_rkv, p_lora, prev_rkv, prev_lora, prm):
    rows = p_rkv.shape[0]
    xs = p_rkv + (prev_rkv - p_rkv) * prm["mu_rkv"]
    xl = p_lora + (prev_lora - p_lora) * prm["mu_lora"]
    wd = xl[:, 0:LORA_W]
    ad = xl[:, LORA_W:LORA_W + LORA_A]
    gd = xl[:, LANES:LANES + G_PAD]
    w_pre = prm["w0"] + jnp.dot(jnp.tanh(wd), prm["w_up"], preferred_element_type=F32, precision=HIGHEST)
    a_pre = prm["a0"] + jnp.dot(ad, prm["a_up"], preferred_element_type=F32, precision=HIGHEST)
    g = jnp.dot(_sigmoid(gd), prm["g_up"], preferred_element_type=F32, precision=HIGHEST)
    logw = -jnp.exp(-_softplus(-w_pre) - 0.5)
    a = _to_pm(_sigmoid(a_pre))
    r = _to_pm(xs[:, 0:D])
    k = _to_pm(xs[:, D:2 * D])
    v = _to_pm(xs[:, 2 * D:3 * D])
    kk = k * _param_pm(prm["k_k"], rows)
    nrm = jnp.sqrt(_head_sum(kk * kk, _ones_bd()))
    kk = kk / jnp.maximum(nrm, 1e-12)
    k = k * (1.0 + (a - 1.0) * _param_pm(prm["k_a"], rows))
    return r, k, v, -kk, kk * a, _to_pm(logw), _to_pm(g)


def _rwkv_post(y, r, k, v, g, prm):
    rows = y.shape[0] // N_PAIR
    ones_bd = _ones_bd()
    mean = _head_sum(y, ones_bd) * (1.0 / HD)
    yc = y - mean
    var = _head_sum(yc * yc, ones_bd) * (1.0 / HD)
    yn = yc * lax.rsqrt(var + LNX_EPS) * _param_pm(prm["lnx_g"], rows) + _param_pm(prm["lnx_b"], rows)
    bonus = _head_sum(r * k * _param_pm(prm["r_k"], rows), ones_bd) * v
    return (yn + bonus) * g


_PRM_NAMES = ("mu_rkv", "mu_lora", "w0", "w_up", "a0", "a_up", "g_up", "k_k", "k_a", "r_k", "lnx_g", "lnx_b")


def _prm_specs(prm, n_grid):
    zero = lambda *_: (0, 0)
    return [pl.BlockSpec(prm[n].shape, zero) for n in _PRM_NAMES]


def _stack2(x, lo):
    return jnp.concatenate([jnp.where(lo, x, 0.0), jnp.where(lo, 0.0, x)], axis=1)


def _bmm(a, b):
    return lax.dot_general(a, b, (((2,), (1,)), ((0,), (0,))), preferred_element_type=F32, precision=HIGHEST)


def _bmm_nt(a, b):
    return lax.dot_general(a, b, (((2,), (2,)), ((0,), (0,))), preferred_element_type=F32, precision=HIGHEST)


def _bmm_tn(a, b):
    return lax.dot_general(a, b, (((1,), (1,)), ((0,), (0,))), preferred_element_type=F32, precision=HIGHEST)


def _rwkv_chunk(s, r, k, v, a_, b_, logw):
    c = r.shape[1]
    t_idx = lax.broadcasted_iota(jnp.int32, logw.shape, 1)
    cum = logw
    d = 1
    while d < c:
        cum = cum + jnp.where(t_idx >= d, pltpu.roll(cum, d, 1), 0.0)
        d *= 2
    e_neg = jnp.exp(-cum)
    l_end = cum[:, c - 1:c, :]
    e_end = jnp.exp(l_end)
    lo = lax.broadcasted_iota(jnp.int32, (N_PAIR, c, LANES), 2) < HD
    xa = _stack2(a_ * jnp.exp(cum - logw), lo)
    xr = _stack2(r * jnp.exp(cum), lo)
    bt, kt = b_ * e_neg, k * e_neg
    yb, yk = _stack2(bt, lo), _stack2(kt, lo)
    vs = _stack2(v, lo)
    gmat = _bmm_nt(jnp.concatenate([xa, xr], axis=1), jnp.concatenate([yb, yk], axis=1))
    n2 = 2 * c
    ri = lax.broadcasted_iota(jnp.int32, (N_PAIR, n2, n2), 1)
    ci = lax.broadcasted_iota(jnp.int32, (N_PAIR, n2, n2), 2)
    same = (ri >= c) == (ci >= c)
    tr, tc = ri & (c - 1), ci & (c - 1)
    strict = same & (tr > tc)
    incl = same & (tr >= tc)
    a_ab = jnp.where(strict, gmat[:, :n2, :n2], 0.0)
    a_ak = jnp.where(strict, gmat[:, :n2, n2:], 0.0)
    a_rb = jnp.where(incl, gmat[:, n2:, :n2], 0.0)
    a_rk = jnp.where(incl, gmat[:, n2:, n2:], 0.0)
    tinv = jnp.where(ri == ci, 1.0, 0.0) + a_ab
    pw = a_ab
    d = 2
    while d < c:
        pw = _bmm(pw, pw)
        tinv = tinv + _bmm(tinv, pw)
        d *= 2
    us = _bmm(tinv, _bmm_nt(xa, s) + _bmm(a_ak, vs))
    ys = _bmm_nt(xr, s) + _bmm(a_rb, us) + _bmm(a_rk, vs)
    y = ys[:, :c] + ys[:, c:]
    s_new = s * e_end + _bmm_tn(jnp.concatenate([us, vs], axis=1),
                                 jnp.concatenate([yb * e_end, yk * e_end], axis=1))
    return s_new, y


def _rwkv_prompt_kernel(prkv_ref, plora_ref, *rest):
    prm_refs = rest[:len(_PRM_NAMES)]
    o_ref, s_out_ref, s_ref, carry_rkv, carry_lora = rest[len(_PRM_NAMES):]
    ci = pl.program_id(1)
    c = prkv_ref.shape[0]

    @pl.when(ci == 0)
    def _():
        s_ref[...] = jnp.zeros_like(s_ref)
        carry_rkv[...] = jnp.zeros_like(carry_rkv)
        carry_lora[...] = jnp.zeros_like(carry_lora)

    prm = {n: ref[...] for n, ref in zip(_PRM_NAMES, prm_refs)}
    p_rkv, p_lora = prkv_ref[...], plora_ref[...]

    def shifted(x, carry_ref):
        first = lax.broadcasted_iota(jnp.int32, x.shape, 0) == 0
        return jnp.where(first, carry_ref[0:1, :], pltpu.roll(x, 1, 0))

    prev_rkv = shifted(p_rkv, carry_rkv)
    prev_lora = shifted(p_lora, carry_lora)
    carry_rkv[0:1, :] = p_rkv[c - 1:c, :]
    carry_lora[0:1, :] = p_lora[c - 1:c, :]

    r, k, v, a_, b_, logw, g = _rwkv_pre(p_rkv, p_lora, prev_rkv, prev_lora, prm)
    sh = (N_PAIR, c, LANES)
    s_new, y = _rwkv_chunk(s_ref[...], r.reshape(sh), k.reshape(sh), v.reshape(sh),
                           a_.reshape(sh), b_.reshape(sh), logw.reshape(sh))
    s_ref[...] = s_new
    out = _rwkv_post(y.reshape(N_PAIR * c, LANES), r, k, v, g, prm)
    o_ref[...] = _from_pm(out).astype(o_ref.dtype)

    @pl.when(ci == pl.num_programs(1) - 1)
    def _():
        s_out_ref[0] = s_new


def _rwkv_prompt(p_all, prm, batch, seq):
    c = RWKV_CHUNK
    nc = seq // c
    row = lambda b, i: b * nc + i
    return pl.pallas_call(
        _rwkv_prompt_kernel,
        out_shape=(jax.ShapeDtypeStruct((batch * seq, D), BF16),
                   jax.ShapeDtypeStruct((batch, N_PAIR, LANES, LANES), F32)),
        grid=(batch, nc),
        in_specs=[pl.BlockSpec((c, 3 * D), lambda b, i: (row(b, i), C_RKV // (3 * D))),
                  pl.BlockSpec((c, LORA_PAD), lambda b, i: (row(b, i), C_LORA // LORA_PAD))]
                 + _prm_specs(prm, 2),
        out_specs=(pl.BlockSpec((c, D), lambda b, i: (row(b, i), 0)),
                   pl.BlockSpec((1, N_PAIR, LANES, LANES), lambda b, i: (b, 0, 0, 0))),
        scratch_shapes=[pltpu.VMEM((N_PAIR, LANES, LANES), F32),
                        pltpu.VMEM((8, 3 * D), F32),
                        pltpu.VMEM((8, LORA_PAD), F32)],
        compiler_params=_cparams(("parallel", "arbitrary")),
        name="rwkv_prompt",
    )(p_all, p_all, *[prm[n] for n in _PRM_NAMES])


def _rwkv_sample_pre_kernel(prkv_ref, plora_ref, qrkv_ref, qlora_ref, srkv_ref, slora_ref, *rest):
    prm_refs = rest[:len(_PRM_NAMES)]
    outs = rest[len(_PRM_NAMES):]
    t = pl.program_id(0)
    prm = {n: ref[...] for n, ref in zip(_PRM_NAMES, prm_refs)}
    first = t == 0
    prev_rkv = jnp.where(first, srkv_ref[...], qrkv_ref[...])
    prev_lora = jnp.where(first, slora_ref[...], qlora_ref[...])
    r, k, v, a_, b_, logw, g = _rwkv_pre(prkv_ref[...], plora_ref[...], prev_rkv, prev_lora, prm)
    for ref, val in zip(outs, (r, jnp.exp(logw), k, v, a_, b_, g)):
        ref[...] = _from_pm(val)


def _rwkv_sample_pre(p_all, row0, shift_rkv, shift_lora, prm, bsz, tn):
    base = row0 // bsz
    cur = lambda t: base + t
    prv = lambda t: base + jnp.maximum(t - 1, 0)
    out = jax.ShapeDtypeStruct((tn * bsz, D), F32)
    return pl.pallas_call(
        _rwkv_sample_pre_kernel,
        out_shape=(out,) * 7,
        grid=(tn,),
        in_specs=[pl.BlockSpec((bsz, 3 * D), lambda t: (cur(t), C_RKV // (3 * D))),
                  pl.BlockSpec((bsz, LORA_PAD), lambda t: (cur(t), C_LORA // LORA_PAD)),
                  pl.BlockSpec((bsz, 3 * D), lambda t: (prv(t), C_RKV // (3 * D))),
                  pl.BlockSpec((bsz, LORA_PAD), lambda t: (prv(t), C_LORA // LORA_PAD)),
                  pl.BlockSpec((bsz, 3 * D), lambda t: (0, 0)),
                  pl.BlockSpec((bsz, LORA_PAD), lambda t: (0, 0))]
                 + _prm_specs(prm, 1),
        out_specs=tuple(pl.BlockSpec((bsz, D), lambda t: (t, 0)) for _ in range(7)),
        compiler_params=_cparams(("arbitrary",)),
        name="rwkv_sample_pre",
    )(p_all, p_all, p_all, p_all, shift_rkv, shift_lora, *[prm[n] for n in _PRM_NAMES])


def _rwkv_sample_scan_kernel(s0_ref, r_ref, w_ref, k_ref, v_ref, a_ref, b_ref, y_ref, so_ref, s_ref):
    bsz = s0_ref.shape[-1]
    tn = r_ref.shape[1] // bsz
    s_ref[...] = s0_ref[0]
    for t in range(tn):
        cols = slice(t * bsz, (t + 1) * bsz)
        r_t, w_t, k_t = r_ref[:, cols], w_ref[:, cols], k_ref[:, cols]
        a_t, b_t = a_ref[:, cols], b_ref[:, cols]

        def body(i, _):
            s_i = s_ref[i]
            sa = jnp.sum(s_i * a_t, axis=0, keepdims=True)
            v_i = v_ref[i, :, cols]
            s_i = s_i * w_t + sa * b_t + v_i * k_t
            s_ref[i] = s_i
            y_ref[i, :, cols] = jnp.sum(s_i * r_t, axis=0, keepdims=True)
            return 0

        lax.fori_loop(0, HD, body, 0)
    so_ref[0] = s_ref[...]


def _rwkv_sample_scan(s0, r, w, k, v, a_, b_):
    nh, _, _, bsz = s0.shape
    tb = r.shape[1]
    vec = pl.BlockSpec((HD, tb), lambda h: (h, 0))
    vec3 = pl.BlockSpec((HD, 1, tb), lambda h: (h, 0, 0))
    st = pl.BlockSpec((1, HD, HD, bsz), lambda h: (h, 0, 0, 0))
    return pl.pallas_call(
        _rwkv_sample_scan_kernel,
        out_shape=(jax.ShapeDtypeStruct((nh * HD, 1, tb), F32), jax.ShapeDtypeStruct(s0.shape, F32)),
        grid=(nh,),
        in_specs=[st, vec, vec, vec, vec3, vec, vec],
        out_specs=(vec3, st),
        scratch_shapes=[pltpu.VMEM((HD, HD, bsz), F32)],
        compiler_params=_cparams(("parallel",)),
        name="rwkv_sample_scan",
    )(s0, r, w, k, v, a_, b_)


def _rwkv_sample_post_kernel(y_ref, r_ref, k_ref, v_ref, g_ref, *rest):
    prm_refs = rest[:len(_PRM_NAMES)]
    o_ref = rest[len(_PRM_NAMES)]
    prm = {n: ref[...] for n, ref in zip(_PRM_NAMES, prm_refs)}
    out = _rwkv_post(_to_pm(y_ref[...]), _to_pm(r_ref[...]), _to_pm(k_ref[...]),
                     _to_pm(v_ref[...]), _to_pm(g_ref[...]), prm)
    o_ref[...] = _from_pm(out).astype(o_ref.dtype)


def _rwkv_sample_post(y, r, k, v, g, prm, bsz):
    n = y.shape[0]
    blk = pl.BlockSpec((bsz, D), lambda t: (t, 0))
    return pl.pallas_call(
        _rwkv_sample_post_kernel,
        out_shape=jax.ShapeDtypeStruct((n, D), BF16),
        grid=(n // bsz,),
        in_specs=[blk] * 5 + _prm_specs(prm, 1),
        out_specs=blk,
        compiler_params=_cparams(("parallel",)),
        name="rwkv_sample_post",
    )(y, r, k, v, g, *[prm[n] for n in _PRM_NAMES])


def _outproj_kernel(att_ref, rw_ref, ga_ref, gr_ref, x_ref, w_ref, g2_ref, *rest, n_experts):
    m = _sigmoid(ga_ref[...]) * att_ref[...].astype(F32) + _sigmoid(gr_ref[...]) * rw_ref[...].astype(F32)
    xn = x_ref[...] + jnp.dot(m.astype(BF16), w_ref[...], preferred_element_type=F32)
    h2 = _rms(xn, g2_ref[...])
    if n_experts:
        router_ref, xo_ref, h2_ref, gate_ref = rest
        logits = jnp.dot(h2, router_ref[...], preferred_element_type=F32, precision=HIGHEST)
        lane = lax.broadcasted_iota(jnp.int32, logits.shape, 1).astype(F32)
        lg = jnp.where(lane < n_experts, logits, -jnp.inf)
        v1 = jnp.max(lg, -1, keepdims=True)
        i1 = jnp.min(jnp.where(lg == v1, lane, float(LANES)), -1, keepdims=True)
        lg2 = jnp.where(lane == i1, -jnp.inf, lg)
        v2 = jnp.max(lg2, -1, keepdims=True)
        i2 = jnp.min(jnp.where(lg2 == v2, lane, float(LANES)), -1, keepdims=True)
        e2 = jnp.exp(v2 - v1)
        den = 1.0 + e2
        gate_ref[...] = jnp.where(lane == i1, 1.0 / den, 0.0) + jnp.where(lane == i2, e2 / den, 0.0)
    else:
        xo_ref, h2_ref = rest
    xo_ref[...] = xn
    h2_ref[...] = h2.astype(BF16)


def _outproj(att, rw, p_all, x, w_out, g2, router, n_experts, tm):
    n = x.shape[0]
    row = lambda w_: pl.BlockSpec((tm, w_), lambda i: (i, 0))
    in_specs = [row(D), row(D),
                pl.BlockSpec((tm, D), lambda i: (i, C_GATE // D)),
                pl.BlockSpec((tm, D), lambda i: (i, C_GATE // D + 1)),
                row(D),
                pl.BlockSpec((D, D), lambda i: (0, 0)),
                pl.BlockSpec((1, D), lambda i: (0, 0))]
    args = [att, rw, p_all, p_all, x, w_out, g2]
    out_shape = [jax.ShapeDtypeStruct((n, D), F32), jax.ShapeDtypeStruct((n, D), BF16)]
    out_specs = [row(D), row(D)]
    if n_experts:
        in_specs.append(pl.BlockSpec((D, LANES), lambda i: (0, 0)))
        args.append(router)
        out_shape.append(jax.ShapeDtypeStruct((n, LANES), F32))
        out_specs.append(row(LANES))
    return pl.pallas_call(
        functools.partial(_outproj_kernel, n_experts=n_experts),
        out_shape=tuple(out_shape),
        grid=(n // tm,),
        in_specs=in_specs,
        out_specs=tuple(out_specs),
        compiler_params=_cparams(("parallel",)),
        name="outproj",
    )(*args)


def _ffn_kernel(h_ref, wg_ref, wu_ref, wd_ref, x_ref, *rest, gated, final_norm):
    rest = list(rest)
    gate_ref = rest.pop(0) if gated else None
    gf_ref = rest.pop(0) if final_norm else None
    o_ref, acc_ref = rest
    e, f = pl.program_id(1), pl.program_id(2)

    @pl.when((e == 0) & (f == 0))
    def _():
        acc_ref[...] = jnp.zeros_like(acc_ref)

    h = h_ref[...]
    a = jnp.dot(h, wg_ref[0], preferred_element_type=F32)
    b = jnp.dot(h, wu_ref[0], preferred_element_type=F32)
    t = (a * _sigmoid(a) * b).astype(BF16)
    part = jnp.dot(t, wd_ref[0], preferred_element_type=F32)
    if gated:
        gate = gate_ref[...]
        lane = lax.broadcasted_iota(jnp.int32, gate.shape, 1)
        part = part * jnp.sum(jnp.where(lane == e, gate, 0.0), -1, keepdims=True)
    acc_ref[...] += part

    @pl.when((e == pl.num_programs(1) - 1) & (f == pl.num_programs(2) - 1))
    def _():
        y = x_ref[...] + acc_ref[...]
        o_ref[...] = _rms(y, gf_ref[...]) if final_norm else y


def _ffn(h, wg, wu, wd, x, gate, gf, tm, tf):
    n = x.shape[0]
    ne, _, fdim = wg.shape
    gated, final_norm = gate is not None, gf is not None
    in_specs = [pl.BlockSpec((tm, D), lambda i, e, f: (i, 0)),
                pl.BlockSpec((1, D, tf), lambda i, e, f: (e, 0, f)),
                pl.BlockSpec((1, D, tf), lambda i, e, f: (e, 0, f)),
                pl.BlockSpec((1, tf, D), lambda i, e, f: (e, f, 0)),
                pl.BlockSpec((tm, D), lambda i, e, f: (i, 0))]
    args = [h, wg, wu, wd, x]
    if gated:
        in_specs.append(pl.BlockSpec((tm, LANES), lambda i, e, f: (i, 0)))
        args.append(gate)
    if final_norm:
        in_specs.append(pl.BlockSpec((1, D), lambda i, e, f: (0, 0)))
        args.append(gf)
    return pl.pallas_call(
        functools.partial(_ffn_kernel, gated=gated, final_norm=final_norm),
        out_shape=jax.ShapeDtypeStruct((n, D), F32),
        grid=(n // tm, ne, fdim // tf),
        in_specs=in_specs,
        out_specs=pl.BlockSpec((tm, D), lambda i, e, f: (i, 0)),
        scratch_shapes=[pltpu.VMEM((tm, D), F32)],
        compiler_params=_cparams(("parallel", "arbitrary", "arbitrary")),
        name="ffn",
    )(*args)


def _split_shift_cols(a):
    pad = jnp.zeros(a.shape[:-1] + (LORA_PAD - (LORA_W + LORA_A + LORA_G),), a.dtype)
    lead = a[..., 3 * D:3 * D + LORA_W + LORA_A]
    gd = a[..., 3 * D + LORA_W + LORA_A:]
    return a[..., :3 * D], jnp.concatenate([lead, gd, pad], -1)


def _relayout_w_in(w):
    q, k, v = w[:, 0:D], w[:, D:D + 256], w[:, D + 256:D + 512]
    pr = w[:, D + 512:D + 512 + 3360]
    gates = w[:, D + 512 + 3360:]
    rkv, lora = _split_shift_cols(pr)
    return jnp.concatenate([rkv, q, gates, k, v, lora], axis=1).astype(BF16)


def _rope_tables(pos):
    inv = ROPE_THETA ** (-jnp.arange(0, ROT, 2, dtype=F32) / ROT)
    ang = pos.astype(F32)[:, None] * inv[None, :]
    cos, sin = jnp.cos(ang), jnp.sin(ang)
    n = pos.shape[0]
    half = ROT // 2
    one = jnp.ones((n, HD - ROT), F32)
    zero = jnp.zeros((n, HD - half), F32)
    c = jnp.concatenate([cos, cos, one], 1)
    sa = jnp.concatenate([-sin, zero], 1)
    sb = jnp.concatenate([jnp.zeros((n, half), F32), sin, jnp.zeros((n, HD - ROT), F32)], 1)
    tile = lambda a: jnp.concatenate([a, a], 1)
    return tile(c), tile(sa), tile(sb)


def _pair_state_to_heads(s):
    even = s[:, :, :HD, :HD]
    odd = s[:, :, HD:, HD:]
    b = s.shape[0]
    return jnp.stack([even, odd], axis=2).reshape(b, 2 * N_PAIR, HD, HD)


def kernel(x_prompt, x_sample, cache_k_win, cache_v_win, state_wkv, state_shift, norm_mix_g, w_in, w_out, attn_sinks, shift_mu, decay_w0, decay_up, iclr_a0, iclr_up, gate_up, key_kk, key_ka, bonus_rk, lnx_g, lnx_b, norm_ffn_g, ffn_w_gate, ffn_w_up, ffn_w_down, moe_router, moe_w_gate, moe_w_up, moe_w_down, norm_final_g):
    batch, seq, _ = x_prompt.shape
    sb, st, _ = x_sample.shape
    depth = w_in.shape[0]
    n_p, n_s = batch * seq, sb * st
    n = n_p + n_s
    n_buf = cache_k_win.shape[2]
    tm = _pick(n, (512, 256, 128))
    tm_moe = _pick(n, (640, 512, 256, 128))

    x = jnp.concatenate([x_prompt.reshape(n_p, D), x_sample.transpose(1, 0, 2).reshape(n_s, D)], 0)
    pos = jnp.concatenate([jnp.tile(jnp.arange(seq), batch),
                           jnp.repeat(PAST_LEN + jnp.arange(st), sb)])
    rope_c, rope_sa, rope_sb = _rope_tables(pos)
    row = lambda a: a.reshape(1, -1)

    new_p, new_s = [], []
    for l in range(depth):
        mu_rkv, mu_lora = _split_shift_cols(row(shift_mu[l]))
        g_up = jnp.concatenate([gate_up[l], jnp.zeros((G_PAD - LORA_G, D), F32)], 0)
        prm = dict(mu_rkv=mu_rkv, mu_lora=mu_lora, w0=row(decay_w0[l]), w_up=decay_up[l],
                   a0=row(iclr_a0[l]), a_up=iclr_up[l], g_up=g_up, k_k=row(key_kk[l]),
                   k_a=row(key_ka[l]), r_k=row(bonus_rk[l]), lnx_g=row(lnx_g[l]), lnx_b=row(lnx_b[l]))
        p_all = _inproj(x, row(norm_mix_g[l]), _relayout_w_in(w_in[l]), rope_c, rope_sa, rope_sb, tm)

        att_p = _attn_prompt(p_all, attn_sinks[l], batch, seq)
        ps = p_all[n_p:].reshape(st, sb, N_COLS).transpose(1, 0, 2)
        k_new, v_new = ps[..., C_K:C_K + 256], ps[..., C_V:C_V + 256]
        k_cache = cache_k_win[l].reshape(sb, n_buf, 256)
        v_cache = cache_v_win[l].reshape(sb, n_buf, 256)
        att_s = _attn_sample(ps[..., C_Q:C_Q + D], k_new, v_new, k_cache, v_cache, attn_sinks[l])
        att = jnp.concatenate([att_p, att_s.transpose(1, 0, 2).reshape(n_s, D)], 0)

        rw_p, s_pairs = _rwkv_prompt(p_all, prm, batch, seq)
        sh_rkv, sh_lora = _split_shift_cols(state_shift[l])
        r_s, w_s, k_s, v_s, a_s, b_s, g_s = _rwkv_sample_pre(p_all, n_p, sh_rkv, sh_lora, prm, sb, st)
        s0 = state_wkv[l].transpose(1, 2, 3, 0)
        y_t, s_fin = _rwkv_sample_scan(s0, r_s.T, w_s.T, k_s.T, v_s.T[:, None, :], a_s.T, b_s.T)
        rw_s = _rwkv_sample_post(y_t[:, 0, :].T, r_s, k_s, v_s, g_s, prm, sb)
        rw = jnp.concatenate([rw_p, rw_s], 0)

        is_moe = l % 2 == 1
        last = l == depth - 1
        gf = row(norm_final_g) if last else None
        if is_moe:
            ne = moe_router.shape[-1]
            router = jnp.concatenate([moe_router[l // 2], jnp.zeros((D, LANES - ne), F32)], 1)
            x_mid, h2, gate = _outproj(att, rw, p_all, x, w_out[l].astype(BF16), row(norm_ffn_g[l]), router, ne, tm)
            x = _ffn(h2, moe_w_gate[l // 2].astype(BF16), moe_w_up[l // 2].astype(BF16),
                     moe_w_down[l // 2].astype(BF16), x_mid, gate, gf, tm_moe,
                     _pick(moe_w_gate.shape[-1], (896, 512, 256, 128)))
        else:
            x_mid, h2 = _outproj(att, rw, p_all, x, w_out[l].astype(BF16), row(norm_ffn_g[l]), None, 0, tm)
            x = _ffn(h2, ffn_w_gate[l // 2][None].astype(BF16), ffn_w_up[l // 2][None].astype(BF16),
                     ffn_w_down[l // 2][None].astype(BF16), x_mid, None, gf, tm,
                     _pick(ffn_w_gate.shape[-1], (1408, 512, 256, 128)))

        pp = p_all[:n_p].reshape(batch, seq, N_COLS)
        k_p = pp[:, seq - WINDOW:, C_K:C_K + 256].reshape(batch, WINDOW, N_KV, HD)
        v_p = pp[:, seq - WINDOW:, C_V:C_V + 256].reshape(batch, WINDOW, N_KV, HD)
        unsplit = lambda a: jnp.concatenate(
            [a[..., C_RKV:C_RKV + 3 * D], a[..., C_LORA:C_LORA + LORA_W + LORA_A + LORA_G]], -1)
        new_p.append((k_p, v_p, _pair_state_to_heads(s_pairs), unsplit(pp[:, -1])))
        k_s_win = jnp.concatenate([k_cache, k_new], 1)[:, -n_buf:].reshape(sb, n_buf, N_KV, HD)
        v_s_win = jnp.concatenate([v_cache, v_new], 1)[:, -n_buf:].reshape(sb, n_buf, N_KV, HD)
        new_s.append((k_s_win, v_s_win, s_fin.transpose(3, 0, 1, 2), unsplit(ps[:, -1])))

    if depth == 0:
        raise ValueError("depth must be positive")
    y_p = x[:n_p].reshape(batch, seq, D)
    y_s = x[n_p:].reshape(st, sb, D).transpose(1, 0, 2)
    stk = lambda sts, i: jnp.stack([s[i] for s in sts])
    return (y_p, y_s,
            stk(new_p, 0), stk(new_p, 1), stk(new_p, 2), stk(new_p, 3),
            stk(new_s, 0), stk(new_s, 1), stk(new_s, 2), stk(new_s, 3))
```

```python
import functools

import jax
import jax.numpy as jnp
from jax import lax
from jax.experimental import pallas as pl
from jax.experimental.pallas import tpu as pltpu

F32 = jnp.float32
BF16 = jnp.bfloat16
HIGHEST = lax.Precision.HIGHEST

LANES = 128
VMEM_LIMIT = 56 * 1024 * 1024

D = 1024
HD = 64
N_Q = 16
N_KV = 4
ROT = 16
ROPE_THETA = 500000.0
WINDOW = 128
RMS_EPS = 1e-5
LNX_EPS = 64e-5
N_PAIR = D // LANES
LORA_W, LORA_A, LORA_G = 64, 64, 160
LORA_PAD = 512
G_PAD = 256

C_RKV = 0
C_Q = 3072
C_GATE = 4096
C_K = 6144
C_V = 6400
C_LORA = 6656
N_COLS = 7168
TN_IN = 1024
Q_TILE = C_Q // TN_IN
KV_TILE = C_K // TN_IN

RWKV_CHUNK = 64


PAST_LEN = 16384


def _pick(n, cands):
    return next(c for c in cands if n % c == 0)


def _cparams(sem, **kw):
    return pltpu.CompilerParams(dimension_semantics=sem, vmem_limit_bytes=VMEM_LIMIT, **kw)


def _rms(x, g):
    return x * lax.rsqrt(jnp.mean(x * x, -1, keepdims=True) + RMS_EPS) * g


def _sigmoid(x):
    return 1.0 / (1.0 + jnp.exp(-x))


def _rope_chunk(a, c, sa, sb):
    return a * c + pltpu.roll(a, LANES - ROT // 2, 1) * sa + pltpu.roll(a, ROT // 2, 1) * sb


def _inproj_kernel(x_ref, g_ref, w_ref, c_ref, sa_ref, sb_ref, o_ref, h_ref):
    j = pl.program_id(1)

    @pl.when(j == 0)
    def _():
        h_ref[...] = _rms(x_ref[...], g_ref[...]).astype(BF16)

    acc = jnp.dot(h_ref[...], w_ref[...], preferred_element_type=F32)

    def roped(n_chunks):
        c, sa, sb = c_ref[...], sa_ref[...], sb_ref[...]
        parts = [_rope_chunk(acc[:, k * LANES:(k + 1) * LANES], c, sa, sb) for k in range(n_chunks)]
        if n_chunks * LANES < TN_IN:
            parts.append(acc[:, n_chunks * LANES:])
        return jnp.concatenate(parts, axis=1)

    @pl.when(j == Q_TILE)
    def _():
        o_ref[...] = roped(TN_IN // LANES)

    @pl.when(j == KV_TILE)
    def _():
        o_ref[...] = roped(N_KV * HD // LANES)

    @pl.when((j != Q_TILE) & (j != KV_TILE))
    def _():
        o_ref[...] = acc


def _inproj(x, g, w, rope_c, rope_sa, rope_sb, tm):
    n = x.shape[0]
    return pl.pallas_call(
        _inproj_kernel,
        out_shape=jax.ShapeDtypeStruct((n, N_COLS), F32),
        grid=(n // tm, N_COLS // TN_IN),
        in_specs=[
            pl.BlockSpec((tm, D), lambda i, j: (i, 0)),
            pl.BlockSpec((1, D), lambda i, j: (0, 0)),
            pl.BlockSpec((D, TN_IN), lambda i, j: (0, j)),
            pl.BlockSpec((tm, LANES), lambda i, j: (i, 0)),
            pl.BlockSpec((tm, LANES), lambda i, j: (i, 0)),
            pl.BlockSpec((tm, LANES), lambda i, j: (i, 0)),
        ],
        out_specs=pl.BlockSpec((tm, TN_IN), lambda i, j: (i, j)),
        scratch_shapes=[pltpu.VMEM((tm, D), BF16)],
        compiler_params=_cparams(("parallel", "arbitrary")),
        name="inproj",
    )(x, g, w, rope_c, rope_sa, rope_sb)


def _sink_softmax(s, mask, sink):
    s = jnp.where(mask, s * (HD ** -0.5), -jnp.inf)
    m = jnp.maximum(jnp.max(s, -1, keepdims=True), sink)
    p = jnp.exp(s - m)
    return p / (jnp.sum(p, -1, keepdims=True) + jnp.exp(sink - m))


def _dot_nt(a, b, **kw):
    return lax.dot_general(a, b, (((1,), (1,)), ((), ())), preferred_element_type=F32, **kw)


def _dot_tn(a, b, **kw):
    return lax.dot_general(a, b, (((0,), (0,)), ((), ())), preferred_element_type=F32, **kw)


def _attend_group(q2, kc, vc, half, mask, sink_a, sink_b):
    lo = lax.broadcasted_iota(jnp.int32, kc.shape, 1) < HD
    k_sw = pltpu.roll(kc, HD, 1)
    v_sw = pltpu.roll(vc, HD, 1)
    k_lo, k_hi = (kc, k_sw) if half == 0 else (k_sw, kc)
    v_lo, v_hi = (vc, v_sw) if half == 0 else (v_sw, vc)
    ka = jnp.where(lo, k_lo, 0.0).astype(BF16)
    kb = jnp.where(lo, 0.0, k_hi).astype(BF16)
    va = jnp.where(lo, v_lo, 0.0).astype(BF16)
    vb = jnp.where(lo, 0.0, v_hi).astype(BF16)
    pa = _sink_softmax(_dot_nt(q2, ka), mask, sink_a).astype(BF16)
    pb = _sink_softmax(_dot_nt(q2, kb), mask, sink_b).astype(BF16)
    return (jnp.dot(pa, va, preferred_element_type=F32)
            + jnp.dot(pb, vb, preferred_element_type=F32))


def _attn_prompt_kernel(sink_ref, q_ref, kp_ref, kc_ref, vp_ref, vc_ref, o_ref):
    blk = pl.program_id(1)
    w = WINDOW
    q = q_ref[...].astype(BF16)
    k = jnp.concatenate([kp_ref[...], kc_ref[...]], axis=0)
    v = jnp.concatenate([vp_ref[...], vc_ref[...]], axis=0)
    qi = lax.broadcasted_iota(jnp.int32, (2 * w, 2 * w), 0) & (w - 1)
    kj = lax.broadcasted_iota(jnp.int32, (2 * w, 2 * w), 1)
    top = lax.broadcasted_iota(jnp.int32, (2 * w, 1), 0) < w
    mask = (kj > qi) & (kj <= qi + w) & ((kj >= w) | (blk > 0))
    outs = []
    for g in range(N_KV):
        ch, half = divmod(g, 2)
        q2 = jnp.concatenate([q[:, (2 * g) * LANES:(2 * g + 1) * LANES],
                              q[:, (2 * g + 1) * LANES:(2 * g + 2) * LANES]], axis=0)
        sink_a = jnp.where(top, sink_ref[4 * g], sink_ref[4 * g + 2])
        sink_b = jnp.where(top, sink_ref[4 * g + 1], sink_ref[4 * g + 3])
        o = _attend_group(q2, k[:, ch * LANES:(ch + 1) * LANES], v[:, ch * LANES:(ch + 1) * LANES],
                          half, mask, sink_a, sink_b)
        outs += [o[:w], o[w:]]
    o_ref[...] = jnp.concatenate(outs, axis=1).astype(o_ref.dtype)


def _attn_prompt(p_all, sinks, batch, seq):
    nb = seq // WINDOW
    kcol, vcol = C_K // (N_KV * HD), C_V // (N_KV * HD)
    cur = lambda b, i: b * nb + i
    prev = lambda b, i: b * nb + jnp.maximum(i - 1, 0)
    return pl.pallas_call(
        _attn_prompt_kernel,
        out_shape=jax.ShapeDtypeStruct((batch * seq, D), BF16),
        grid=(batch, nb),
        in_specs=[
            pl.BlockSpec(memory_space=pltpu.SMEM),
            pl.BlockSpec((WINDOW, D), lambda b, i: (cur(b, i), C_Q // D)),
            pl.BlockSpec((WINDOW, N_KV * HD), lambda b, i: (prev(b, i), kcol)),
            pl.BlockSpec((WINDOW, N_KV * HD), lambda b, i: (cur(b, i), kcol)),
            pl.BlockSpec((WINDOW, N_KV * HD), lambda b, i: (prev(b, i), vcol)),
            pl.BlockSpec((WINDOW, N_KV * HD), lambda b, i: (cur(b, i), vcol)),
        ],
        out_specs=pl.BlockSpec((WINDOW, D), lambda b, i: (cur(b, i), 0)),
        compiler_params=_cparams(("parallel", "arbitrary")),
        name="attn_prompt",
    )(sinks, p_all, p_all, p_all, p_all, p_all)


ATT_S_BT = 8
T_PAD = 8


def _attn_sample_kernel(sink_ref, q_ref, kn_ref, vn_ref, kc_ref, vc_ref, o_ref):
    tn = q_ref.shape[1]
    nbuf = kc_ref.shape[1]
    rows = 2 * tn
    keys = nbuf + T_PAD
    r = lax.broadcasted_iota(jnp.int32, (rows, keys), 0)
    t = jnp.where(r >= tn, r - tn, r)
    kj = lax.broadcasted_iota(jnp.int32, (rows, keys), 1)
    mask = (kj > t + (nbuf - WINDOW)) & (kj <= t + nbuf)
    top = lax.broadcasted_iota(jnp.int32, (rows, 1), 0) < tn
    zpad = jnp.zeros((T_PAD - tn, N_KV * HD), F32)
    for b in range(ATT_S_BT):
        q = q_ref[b].astype(BF16)
        k = jnp.concatenate([kc_ref[b], kn_ref[b], zpad], axis=0)
        v = jnp.concatenate([vc_ref[b], vn_ref[b], zpad], axis=0)
        outs = []
        for g in range(N_KV):
            ch, half = divmod(g, 2)
            q2 = jnp.concatenate([q[:, (2 * g) * LANES:(2 * g + 1) * LANES],
                                  q[:, (2 * g + 1) * LANES:(2 * g + 2) * LANES]], axis=0)
            sink_a = jnp.where(top, sink_ref[4 * g], sink_ref[4 * g + 2])
            sink_b = jnp.where(top, sink_ref[4 * g + 1], sink_ref[4 * g + 3])
            o = _attend_group(q2, k[:, ch * LANES:(ch + 1) * LANES], v[:, ch * LANES:(ch + 1) * LANES],
                              half, mask, sink_a, sink_b)
            outs += [o[:tn], o[tn:]]
        o_ref[b] = jnp.concatenate(outs, axis=1).astype(o_ref.dtype)


def _attn_sample(q, k_new, v_new, k_cache, v_cache, sinks):
    bsz, tn, _ = q.shape
    nbuf = k_cache.shape[1]
    kvw = N_KV * HD
    blk = lambda w_, r_: pl.BlockSpec((ATT_S_BT, r_, w_), lambda i: (i, 0, 0))
    return pl.pallas_call(
        _attn_sample_kernel,
        out_shape=jax.ShapeDtypeStruct((bsz, tn, D), BF16),
        grid=(bsz // ATT_S_BT,),
        in_specs=[pl.BlockSpec(memory_space=pltpu.SMEM),
                  blk(D, tn), blk(kvw, tn), blk(kvw, tn), blk(kvw, nbuf), blk(kvw, nbuf)],
        out_specs=blk(D, tn),
        compiler_params=_cparams(("parallel",)),
        name="attn_sample",
    )(sinks, q, k_new, v_new, k_cache, v_cache)


def _to_pm(x):
    return jnp.concatenate([x[:, p * LANES:(p + 1) * LANES] for p in range(N_PAIR)], axis=0)


def _from_pm(x):
    r = x.shape[0] // N_PAIR
    return jnp.concatenate([x[p * r:(p + 1) * r] for p in range(N_PAIR)], axis=1)


def _param_pm(v, r):
    return jnp.concatenate(
        [jnp.broadcast_to(v[:, p * LANES:(p + 1) * LANES], (r, LANES)) for p in range(N_PAIR)], axis=0)


def _head_sum(x, ones_bd):
    hi = x.astype(BF16)
    lo = (x - hi.astype(F32)).astype(BF16)
    return (jnp.dot(hi, ones_bd, preferred_element_type=F32)
            + jnp.dot(lo, ones_bd, preferred_element_type=F32))


def _ones_bd():
    r = lax.broadcasted_iota(jnp.int32, (LANES, LANES), 0) // HD
    c = lax.broadcasted_iota(jnp.int32, (LANES, LANES), 1) // HD
    return jnp.where(r == c, 1.0, 0.0).astype(BF16)


def _softplus(z):
    return jnp.maximum(z, 0.0) + jnp.log(1.0 + jnp.exp(-jnp.abs(z)))


def _rwkv_pre(p_rkv, p_lora, prev_rkv, prev_lora, prm):
    rows = p_rkv.shape[0]
    xs = p_rkv + (prev_rkv - p_rkv) * prm["mu_rkv"]
    xl = p_lora + (prev_lora - p_lora) * prm["mu_lora"]
    wd = xl[:, 0:LORA_W]
    ad = xl[:, LORA_W:LORA_W + LORA_A]
    gd = xl[:, LANES:LANES + G_PAD]
    w_pre = prm["w0"] + jnp.dot(jnp.tanh(wd), prm["w_up"], preferred_element_type=F32, precision=HIGHEST)
    a_pre = prm["a0"] + jnp.dot(ad, prm["a_up"], preferred_element_type=F32, precision=HIGHEST)
    g = jnp.dot(_sigmoid(gd), prm["g_up"], preferred_element_type=F32, precision=HIGHEST)
    logw = -jnp.exp(-_softplus(-w_pre) - 0.5)
    a = _to_pm(_sigmoid(a_pre))
    r = _to_pm(xs[:, 0:D])
    k = _to_pm(xs[:, D:2 * D])
    v = _to_pm(xs[:, 2 * D:3 * D])
    kk = k * _param_pm(prm["k_k"], rows)
    nrm = jnp.sqrt(_head_sum(kk * kk, _ones_bd()))
    kk = kk / jnp.maximum(nrm, 1e-12)
    k = k * (1.0 + (a - 1.0) * _param_pm(prm["k_a"], rows))
    return r, k, v, -kk, kk * a, _to_pm(logw), _to_pm(g)


def _rwkv_post(y, r, k, v, g, prm):
    rows = y.shape[0] // N_PAIR
    ones_bd = _ones_bd()
    mean = _head_sum(y, ones_bd) * (1.0 / HD)
    yc = y - mean
    var = _head_sum(yc * yc, ones_bd) * (1.0 / HD)
    yn = yc * lax.rsqrt(var + LNX_EPS) * _param_pm(prm["lnx_g"], rows) + _param_pm(prm["lnx_b"], rows)
    bonus = _head_sum(r * k * _param_pm(prm["r_k"], rows), ones_bd) * v
    return (yn + bonus) * g


_PRM_NAMES = ("mu_rkv", "mu_lora", "w0", "w_up", "a0", "a_up", "g_up", "k_k", "k_a", "r_k", "lnx_g", "lnx_b")


def _prm_specs(prm, n_grid):
    zero = lambda *_: (0, 0)
    return [pl.BlockSpec(prm[n].shape, zero) for n in _PRM_NAMES]


def _stack2(x, lo):
    return jnp.concatenate([jnp.where(lo, x, 0.0), jnp.where(lo, 0.0, x)], axis=1)


def _split_bf16(x):
    hi = x.astype(BF16)
    return hi, (x - hi.astype(F32)).astype(BF16)


def _bdot(a, b, ca, cb, passes):
    dims = (((ca,), (cb,)), ((0,), (0,)))
    if passes == 6:
        return lax.dot_general(a, b, dims, preferred_element_type=F32, precision=HIGHEST)
    dot = lambda x, y: lax.dot_general(x, y, dims, preferred_element_type=F32)
    if passes == 1:
        return dot(a.astype(BF16), b.astype(BF16))
    a_hi, a_lo = _split_bf16(a)
    b_hi, b_lo = _split_bf16(b)
    return dot(a_hi, b_hi) + (dot(a_hi, b_lo) + dot(a_lo, b_hi))


_PASSES = dict(gram=1, inv=1, rhs=1, u=1, y=1, state=1)


def _bmm(a, b, site):
    return _bdot(a, b, 2, 1, _PASSES[site])


def _bmm_nt(a, b, site):
    return _bdot(a, b, 2, 2, _PASSES[site])


def _bmm_tn(a, b, site):
    return _bdot(a, b, 1, 1, _PASSES[site])


def _rwkv_chunk(s, r, k, v, a_, b_, logw):
    c = r.shape[1]
    t_idx = lax.broadcasted_iota(jnp.int32, logw.shape, 1)
    cum = logw
    d = 1
    while d < c:
        cum = cum + jnp.where(t_idx >= d, pltpu.roll(cum, d, 1), 0.0)
        d *= 2
    e_neg = jnp.exp(-cum)
    l_end = cum[:, c - 1:c, :]
    e_end = jnp.exp(l_end)
    lo = lax.broadcasted_iota(jnp.int32, (N_PAIR, c, LANES), 2) < HD
    xa = _stack2(a_ * jnp.exp(cum - logw), lo)
    xr = _stack2(r * jnp.exp(cum), lo)
    bt, kt = b_ * e_neg, k * e_neg
    yb, yk = _stack2(bt, lo), _stack2(kt, lo)
    vs = _stack2(v, lo)
    gmat = _bmm_nt(jnp.concatenate([xa, xr], axis=1), jnp.concatenate([yb, yk], axis=1), "gram")
    n2 = 2 * c
    ri = lax.broadcasted_iota(jnp.int32, (N_PAIR, n2, n2), 1)
    ci = lax.broadcasted_iota(jnp.int32, (N_PAIR, n2, n2), 2)
    same = (ri >= c) == (ci >= c)
    tr, tc = ri & (c - 1), ci & (c - 1)
    strict = same & (tr > tc)
    incl = same & (tr >= tc)
    a_ab = jnp.where(strict, gmat[:, :n2, :n2], 0.0)
    a_ak = jnp.where(strict, gmat[:, :n2, n2:], 0.0)
    a_rb = jnp.where(incl, gmat[:, n2:, :n2], 0.0)
    a_rk = jnp.where(incl, gmat[:, n2:, n2:], 0.0)
    tinv = jnp.where(ri == ci, 1.0, 0.0) + a_ab
    pw = a_ab
    d = 2
    while d < c:
        pw = _bmm(pw, pw, "inv")
        tinv = tinv + _bmm(tinv, pw, "inv")
        d *= 2
    us = _bmm(tinv, _bmm_nt(xa, s, "rhs") + _bmm(a_ak, vs, "rhs"), "u")
    ys = _bmm_nt(xr, s, "y") + _bmm(a_rb, us, "y") + _bmm(a_rk, vs, "y")
    y = ys[:, :c] + ys[:, c:]
    s_new = s * e_end + _bmm_tn(jnp.concatenate([us, vs], axis=1),
                                 jnp.concatenate([yb * e_end, yk * e_end], axis=1), "state")
    return s_new, y


def _rwkv_prompt_kernel(prkv_ref, plora_ref, *rest):
    prm_refs = rest[:len(_PRM_NAMES)]
    o_ref, s_out_ref, s_ref, carry_rkv, carry_lora = rest[len(_PRM_NAMES):]
    ci = pl.program_id(1)
    c = prkv_ref.shape[0]

    @pl.when(ci == 0)
    def _():
        s_ref[...] = jnp.zeros_like(s_ref)
        carry_rkv[...] = jnp.zeros_like(carry_rkv)
        carry_lora[...] = jnp.zeros_like(carry_lora)

    prm = {n: ref[...] for n, ref in zip(_PRM_NAMES, prm_refs)}
    p_rkv, p_lora = prkv_ref[...], plora_ref[...]

    def shifted(x, carry_ref):
        first = lax.broadcasted_iota(jnp.int32, x.shape, 0) == 0
        return jnp.where(first, carry_ref[0:1, :], pltpu.roll(x, 1, 0))

    prev_rkv = shifted(p_rkv, carry_rkv)
    prev_lora = shifted(p_lora, carry_lora)
    carry_rkv[0:1, :] = p_rkv[c - 1:c, :]
    carry_lora[0:1, :] = p_lora[c - 1:c, :]

    r, k, v, a_, b_, logw, g = _rwkv_pre(p_rkv, p_lora, prev_rkv, prev_lora, prm)
    sh = (N_PAIR, c, LANES)
    s_new, y = _rwkv_chunk(s_ref[...], r.reshape(sh), k.reshape(sh), v.reshape(sh),
                           a_.reshape(sh), b_.reshape(sh), logw.reshape(sh))
    s_ref[...] = s_new
    out = _rwkv_post(y.reshape(N_PAIR * c, LANES), r, k, v, g, prm)
    o_ref[...] = _from_pm(out).astype(o_ref.dtype)

    @pl.when(ci == pl.num_programs(1) - 1)
    def _():
        s_out_ref[0] = s_new


def _rwkv_prompt(p_all, prm, batch, seq):
    c = RWKV_CHUNK
    nc = seq // c
    row = lambda b, i: b * nc + i
    return pl.pallas_call(
        _rwkv_prompt_kernel,
        out_shape=(jax.ShapeDtypeStruct((batch * seq, D), BF16),
                   jax.ShapeDtypeStruct((batch, N_PAIR, LANES, LANES), F32)),
        grid=(batch, nc),
        in_specs=[pl.BlockSpec((c, 3 * D), lambda b, i: (row(b, i), C_RKV // (3 * D))),
                  pl.BlockSpec((c, LORA_PAD), lambda b, i: (row(b, i), C_LORA // LORA_PAD))]
                 + _prm_specs(prm, 2),
        out_specs=(pl.BlockSpec((c, D), lambda b, i: (row(b, i), 0)),
                   pl.BlockSpec((1, N_PAIR, LANES, LANES), lambda b, i: (b, 0, 0, 0))),
        scratch_shapes=[pltpu.VMEM((N_PAIR, LANES, LANES), F32),
                        pltpu.VMEM((8, 3 * D), F32),
                        pltpu.VMEM((8, LORA_PAD), F32)],
        compiler_params=_cparams(("parallel", "arbitrary")),
        name="rwkv_prompt",
    )(p_all, p_all, *[prm[n] for n in _PRM_NAMES])


def _rwkv_sample_pre_kernel(prkv_ref, plora_ref, qrkv_ref, qlora_ref, srkv_ref, slora_ref, *rest):
    prm_refs = rest[:len(_PRM_NAMES)]
    outs = rest[len(_PRM_NAMES):]
    t = pl.program_id(0)
    prm = {n: ref[...] for n, ref in zip(_PRM_NAMES, prm_refs)}
    first = t == 0
    prev_rkv = jnp.where(first, srkv_ref[...], qrkv_ref[...])
    prev_lora = jnp.where(first, slora_ref[...], qlora_ref[...])
    r, k, v, a_, b_, logw, g = _rwkv_pre(prkv_ref[...], plora_ref[...], prev_rkv, prev_lora, prm)
    for ref, val in zip(outs, (r, jnp.exp(logw), k, v, a_, b_, g)):
        ref[...] = _from_pm(val)


def _rwkv_sample_pre(p_all, row0, shift_rkv, shift_lora, prm, bsz, tn):
    base = row0 // bsz
    cur = lambda t: base + t
    prv = lambda t: base + jnp.maximum(t - 1, 0)
    out = jax.ShapeDtypeStruct((tn * bsz, D), F32)
    return pl.pallas_call(
        _rwkv_sample_pre_kernel,
        out_shape=(out,) * 7,
        grid=(tn,),
        in_specs=[pl.BlockSpec((bsz, 3 * D), lambda t: (cur(t), C_RKV // (3 * D))),
                  pl.BlockSpec((bsz, LORA_PAD), lambda t: (cur(t), C_LORA // LORA_PAD)),
                  pl.BlockSpec((bsz, 3 * D), lambda t: (prv(t), C_RKV // (3 * D))),
                  pl.BlockSpec((bsz, LORA_PAD), lambda t: (prv(t), C_LORA // LORA_PAD)),
                  pl.BlockSpec((bsz, 3 * D), lambda t: (0, 0)),
                  pl.BlockSpec((bsz, LORA_PAD), lambda t: (0, 0))]
                 + _prm_specs(prm, 1),
        out_specs=tuple(pl.BlockSpec((bsz, D), lambda t: (t, 0)) for _ in range(7)),
        compiler_params=_cparams(("arbitrary",)),
        name="rwkv_sample_pre",
    )(p_all, p_all, p_all, p_all, shift_rkv, shift_lora, *[prm[n] for n in _PRM_NAMES])


def _rwkv_sample_scan_kernel(s0_ref, r_ref, w_ref, k_ref, v_ref, a_ref, b_ref, y_ref, so_ref, s_ref):
    bsz = s0_ref.shape[-1]
    tn = r_ref.shape[1] // bsz
    s_ref[...] = s0_ref[0]
    for t in range(tn):
        cols = slice(t * bsz, (t + 1) * bsz)
        r_t, w_t, k_t = r_ref[:, cols], w_ref[:, cols], k_ref[:, cols]
        a_t, b_t = a_ref[:, cols], b_ref[:, cols]

        def body(i, _):
            s_i = s_ref[i]
            sa = jnp.sum(s_i * a_t, axis=0, keepdims=True)
            v_i = v_ref[i, :, cols]
            s_i = s_i * w_t + sa * b_t + v_i * k_t
            s_ref[i] = s_i
            y_ref[i, :, cols] = jnp.sum(s_i * r_t, axis=0, keepdims=True)
            return 0

        lax.fori_loop(0, HD, body, 0)
    so_ref[0] = s_ref[...]


def _rwkv_sample_scan(s0, r, w, k, v, a_, b_):
    nh, _, _, bsz = s0.shape
    tb = r.shape[1]
    vec = pl.BlockSpec((HD, tb), lambda h: (h, 0))
    vec3 = pl.BlockSpec((HD, 1, tb), lambda h: (h, 0, 0))
    st = pl.BlockSpec((1, HD, HD, bsz), lambda h: (h, 0, 0, 0))
    return pl.pallas_call(
        _rwkv_sample_scan_kernel,
        out_shape=(jax.ShapeDtypeStruct((nh * HD, 1, tb), F32), jax.ShapeDtypeStruct(s0.shape, F32)),
        grid=(nh,),
        in_specs=[st, vec, vec, vec, vec3, vec, vec],
        out_specs=(vec3, st),
        scratch_shapes=[pltpu.VMEM((HD, HD, bsz), F32)],
        compiler_params=_cparams(("parallel",)),
        name="rwkv_sample_scan",
    )(s0, r, w, k, v, a_, b_)


def _rwkv_sample_post_kernel(y_ref, r_ref, k_ref, v_ref, g_ref, *rest):
    prm_refs = rest[:len(_PRM_NAMES)]
    o_ref = rest[len(_PRM_NAMES)]
    prm = {n: ref[...] for n, ref in zip(_PRM_NAMES, prm_refs)}
    out = _rwkv_post(_to_pm(y_ref[...]), _to_pm(r_ref[...]), _to_pm(k_ref[...]),
                     _to_pm(v_ref[...]), _to_pm(g_ref[...]), prm)
    o_ref[...] = _from_pm(out).astype(o_ref.dtype)


def _rwkv_sample_post(y, r, k, v, g, prm, bsz):
    n = y.shape[0]
    blk = pl.BlockSpec((bsz, D), lambda t: (t, 0))
    return pl.pallas_call(
        _rwkv_sample_post_kernel,
        out_shape=jax.ShapeDtypeStruct((n, D), BF16),
        grid=(n // bsz,),
        in_specs=[blk] * 5 + _prm_specs(prm, 1),
        out_specs=blk,
        compiler_params=_cparams(("parallel",)),
        name="rwkv_sample_post",
    )(y, r, k, v, g, *[prm[n] for n in _PRM_NAMES])


def _outproj_kernel(att_ref, rw_ref, ga_ref, gr_ref, x_ref, w_ref, g2_ref, *rest, n_experts):
    m = _sigmoid(ga_ref[...]) * att_ref[...].astype(F32) + _sigmoid(gr_ref[...]) * rw_ref[...].astype(F32)
    xn = x_ref[...] + jnp.dot(m.astype(BF16), w_ref[...], preferred_element_type=F32)
    h2 = _rms(xn, g2_ref[...])
    if n_experts:
        router_ref, xo_ref, h2_ref, gate_ref = rest
        logits = jnp.dot(h2, router_ref[...], preferred_element_type=F32, precision=HIGHEST)
        lane = lax.broadcasted_iota(jnp.int32, logits.shape, 1).astype(F32)
        lg = jnp.where(lane < n_experts, logits, -jnp.inf)
        v1 = jnp.max(lg, -1, keepdims=True)
        i1 = jnp.min(jnp.where(lg == v1, lane, float(LANES)), -1, keepdims=True)
        lg2 = jnp.where(lane == i1, -jnp.inf, lg)
        v2 = jnp.max(lg2, -1, keepdims=True)
        i2 = jnp.min(jnp.where(lg2 == v2, lane, float(LANES)), -1, keepdims=True)
        e2 = jnp.exp(v2 - v1)
        den = 1.0 + e2
        gate_ref[...] = jnp.where(lane == i1, 1.0 / den, 0.0) + jnp.where(lane == i2, e2 / den, 0.0)
    else:
        xo_ref, h2_ref = rest
    xo_ref[...] = xn
    h2_ref[...] = h2.astype(BF16)


def _outproj(att, rw, p_all, x, w_out, g2, router, n_experts, tm):
    n = x.shape[0]
    row = lambda w_: pl.BlockSpec((tm, w_), lambda i: (i, 0))
    in_specs = [row(D), row(D),
                pl.BlockSpec((tm, D), lambda i: (i, C_GATE // D)),
                pl.BlockSpec((tm, D), lambda i: (i, C_GATE // D + 1)),
                row(D),
                pl.BlockSpec((D, D), lambda i: (0, 0)),
                pl.BlockSpec((1, D), lambda i: (0, 0))]
    args = [att, rw, p_all, p_all, x, w_out, g2]
    out_shape = [jax.ShapeDtypeStruct((n, D), F32), jax.ShapeDtypeStruct((n, D), BF16)]
    out_specs = [row(D), row(D)]
    if n_experts:
        in_specs.append(pl.BlockSpec((D, LANES), lambda i: (0, 0)))
        args.append(router)
        out_shape.append(jax.ShapeDtypeStruct((n, LANES), F32))
        out_specs.append(row(LANES))
    return pl.pallas_call(
        functools.partial(_outproj_kernel, n_experts=n_experts),
        out_shape=tuple(out_shape),
        grid=(n // tm,),
        in_specs=in_specs,
        out_specs=tuple(out_specs),
        compiler_params=_cparams(("parallel",)),
        name="outproj",
    )(*args)


def _ffn_kernel(h_ref, wg_ref, wu_ref, wd_ref, x_ref, *rest, gated, final_norm):
    rest = list(rest)
    gate_ref = rest.pop(0) if gated else None
    gf_ref = rest.pop(0) if final_norm else None
    o_ref, acc_ref = rest
    e, f = pl.program_id(1), pl.program_id(2)

    @pl.when((e == 0) & (f == 0))
    def _():
        acc_ref[...] = jnp.zeros_like(acc_ref)

    h = h_ref[...]
    a = jnp.dot(h, wg_ref[0], preferred_element_type=F32)
    b = jnp.dot(h, wu_ref[0], preferred_element_type=F32)
    t = (a * _sigmoid(a) * b).astype(BF16)
    part = jnp.dot(t, wd_ref[0], preferred_element_type=F32)
    if gated:
        gate = gate_ref[...]
        lane = lax.broadcasted_iota(jnp.int32, gate.shape, 1)
        part = part * jnp.sum(jnp.where(lane == e, gate, 0.0), -1, keepdims=True)
    acc_ref[...] += part

    @pl.when((e == pl.num_programs(1) - 1) & (f == pl.num_programs(2) - 1))
    def _():
        y = x_ref[...] + acc_ref[...]
        o_ref[...] = _rms(y, gf_ref[...]) if final_norm else y


def _ffn(h, wg, wu, wd, x, gate, gf, tm, tf):
    n = x.shape[0]
    ne, _, fdim = wg.shape
    gated, final_norm = gate is not None, gf is not None
    in_specs = [pl.BlockSpec((tm, D), lambda i, e, f: (i, 0)),
                pl.BlockSpec((1, D, tf), lambda i, e, f: (e, 0, f)),
                pl.BlockSpec((1, D, tf), lambda i, e, f: (e, 0, f)),
                pl.BlockSpec((1, tf, D), lambda i, e, f: (e, f, 0)),
                pl.BlockSpec((tm, D), lambda i, e, f: (i, 0))]
    args = [h, wg, wu, wd, x]
    if gated:
        in_specs.append(pl.BlockSpec((tm, LANES), lambda i, e, f: (i, 0)))
        args.append(gate)
    if final_norm:
        in_specs.append(pl.BlockSpec((1, D), lambda i, e, f: (0, 0)))
        args.append(gf)
    return pl.pallas_call(
        functools.partial(_ffn_kernel, gated=gated, final_norm=final_norm),
        out_shape=jax.ShapeDtypeStruct((n, D), F32),
        grid=(n // tm, ne, fdim // tf),
        in_specs=in_specs,
        out_specs=pl.BlockSpec((tm, D), lambda i, e, f: (i, 0)),
        scratch_shapes=[pltpu.VMEM((tm, D), F32)],
        compiler_params=_cparams(("parallel", "arbitrary", "arbitrary")),
        name="ffn",
    )(*args)


def _split_shift_cols(a):
    pad = jnp.zeros(a.shape[:-1] + (LORA_PAD - (LORA_W + LORA_A + LORA_G),), a.dtype)
    lead = a[..., 3 * D:3 * D + LORA_W + LORA_A]
    gd = a[..., 3 * D + LORA_W + LORA_A:]
    return a[..., :3 * D], jnp.concatenate([lead, gd, pad], -1)


def _relayout_w_in(w):
    q, k, v = w[:, 0:D], w[:, D:D + 256], w[:, D + 256:D + 512]
    pr = w[:, D + 512:D + 512 + 3360]
    gates = w[:, D + 512 + 3360:]
    rkv, lora = _split_shift_cols(pr)
    return jnp.concatenate([rkv, q, gates, k, v, lora], axis=1).astype(BF16)


def _rope_tables(pos):
    inv = ROPE_THETA ** (-jnp.arange(0, ROT, 2, dtype=F32) / ROT)
    ang = pos.astype(F32)[:, None] * inv[None, :]
    cos, sin = jnp.cos(ang), jnp.sin(ang)
    n = pos.shape[0]
    half = ROT // 2
    one = jnp.ones((n, HD - ROT), F32)
    zero = jnp.zeros((n, HD - half), F32)
    c = jnp.concatenate([cos, cos, one], 1)
    sa = jnp.concatenate([-sin, zero], 1)
    sb = jnp.concatenate([jnp.zeros((n, half), F32), sin, jnp.zeros((n, HD - ROT), F32)], 1)
    tile = lambda a: jnp.concatenate([a, a], 1)
    return tile(c), tile(sa), tile(sb)


def _pair_state_to_heads(s):
    even = s[:, :, :HD, :HD]
    odd = s[:, :, HD:, HD:]
    b = s.shape[0]
    return jnp.stack([even, odd], axis=2).reshape(b, 2 * N_PAIR, HD, HD)


def kernel(x_prompt, x_sample, cache_k_win, cache_v_win, state_wkv, state_shift, norm_mix_g, w_in, w_out, attn_sinks, shift_mu, decay_w0, decay_up, iclr_a0, iclr_up, gate_up, key_kk, key_ka, bonus_rk, lnx_g, lnx_b, norm_ffn_g, ffn_w_gate, ffn_w_up, ffn_w_down, moe_router, moe_w_gate, moe_w_up, moe_w_down, norm_final_g):
    batch, seq, _ = x_prompt.shape
    sb, st, _ = x_sample.shape
    depth = w_in.shape[0]
    n_p, n_s = batch * seq, sb * st
    n = n_p + n_s
    n_buf = cache_k_win.shape[2]
    tm = _pick(n, (512, 256, 128))
    tm_moe = _pick(n, (640, 512, 256, 128))

    x = jnp.concatenate([x_prompt.reshape(n_p, D), x_sample.transpose(1, 0, 2).reshape(n_s, D)], 0)
    pos = jnp.concatenate([jnp.tile(jnp.arange(seq), batch),
                           jnp.repeat(PAST_LEN + jnp.arange(st), sb)])
    rope_c, rope_sa, rope_sb = _rope_tables(pos)
    row = lambda a: a.reshape(1, -1)

    new_p, new_s = [], []
    for l in range(depth):
        mu_rkv, mu_lora = _split_shift_cols(row(shift_mu[l]))
        g_up = jnp.concatenate([gate_up[l], jnp.zeros((G_PAD - LORA_G, D), F32)], 0)
        prm = dict(mu_rkv=mu_rkv, mu_lora=mu_lora, w0=row(decay_w0[l]), w_up=decay_up[l],
                   a0=row(iclr_a0[l]), a_up=iclr_up[l], g_up=g_up, k_k=row(key_kk[l]),
                   k_a=row(key_ka[l]), r_k=row(bonus_rk[l]), lnx_g=row(lnx_g[l]), lnx_b=row(lnx_b[l]))
        p_all = _inproj(x, row(norm_mix_g[l]), _relayout_w_in(w_in[l]), rope_c, rope_sa, rope_sb, tm)

        att_p = _attn_prompt(p_all, attn_sinks[l], batch, seq)
        ps = p_all[n_p:].reshape(st, sb, N_COLS).transpose(1, 0, 2)
        k_new, v_new = ps[..., C_K:C_K + 256], ps[..., C_V:C_V + 256]
        k_cache = cache_k_win[l].reshape(sb, n_buf, 256)
        v_cache = cache_v_win[l].reshape(sb, n_buf, 256)
        att_s = _attn_sample(ps[..., C_Q:C_Q + D], k_new, v_new, k_cache, v_cache, attn_sinks[l])
        att = jnp.concatenate([att_p, att_s.transpose(1, 0, 2).reshape(n_s, D)], 0)

        rw_p, s_pairs = _rwkv_prompt(p_all, prm, batch, seq)
        sh_rkv, sh_lora = _split_shift_cols(state_shift[l])
        r_s, w_s, k_s, v_s, a_s, b_s, g_s = _rwkv_sample_pre(p_all, n_p, sh_rkv, sh_lora, prm, sb, st)
        s0 = state_wkv[l].transpose(1, 2, 3, 0)
        y_t, s_fin = _rwkv_sample_scan(s0, r_s.T, w_s.T, k_s.T, v_s.T[:, None, :], a_s.T, b_s.T)
        rw_s = _rwkv_sample_post(y_t[:, 0, :].T, r_s, k_s, v_s, g_s, prm, sb)
        rw = jnp.concatenate([rw_p, rw_s], 0)

        is_moe = l % 2 == 1
        last = l == depth - 1
        gf = row(norm_final_g) if last else None
        if is_moe:
            ne = moe_router.shape[-1]
            router = jnp.concatenate([moe_router[l // 2], jnp.zeros((D, LANES - ne), F32)], 1)
            x_mid, h2, gate = _outproj(att, rw, p_all, x, w_out[l].astype(BF16), row(norm_ffn_g[l]), router, ne, tm)
            x = _ffn(h2, moe_w_gate[l // 2].astype(BF16), moe_w_up[l // 2].astype(BF16),
                     moe_w_down[l // 2].astype(BF16), x_mid, gate, gf, tm_moe,
                     _pick(moe_w_gate.shape[-1], (896, 512, 256, 128)))
        else:
            x_mid, h2 = _outproj(att, rw, p_all, x, w_out[l].astype(BF16), row(norm_ffn_g[l]), None, 0, tm)
            x = _ffn(h2, ffn_w_gate[l // 2][None].astype(BF16), ffn_w_up[l // 2][None].astype(BF16),
                     ffn_w_down[l // 2][None].astype(BF16), x_mid, None, gf, tm,
                     _pick(ffn_w_gate.shape[-1], (1408, 512, 256, 128)))

        n_win = min(WINDOW, seq)
        tail = lambda rows, c0, w_: jnp.stack(
            [lax.slice(p_all, ((b + 1) * seq - rows, c0), ((b + 1) * seq, c0 + w_)) for b in range(batch)])
        k_p = tail(n_win, C_K, N_KV * HD).reshape(batch, n_win, N_KV, HD)
        v_p = tail(n_win, C_V, N_KV * HD).reshape(batch, n_win, N_KV, HD)
        unsplit = lambda a: jnp.concatenate(
            [a[..., C_RKV:C_RKV + 3 * D], a[..., C_LORA:C_LORA + LORA_W + LORA_A + LORA_G]], -1)
        last_p = jnp.concatenate([tail(1, C_RKV, 3 * D), tail(1, C_LORA, LORA_W + LORA_A + LORA_G)], -1)[:, 0]
        new_p.append((k_p, v_p, _pair_state_to_heads(s_pairs), last_p))
        k_s_win = jnp.concatenate([k_cache, k_new], 1)[:, -n_buf:].reshape(sb, n_buf, N_KV, HD)
        v_s_win = jnp.concatenate([v_cache, v_new], 1)[:, -n_buf:].reshape(sb, n_buf, N_KV, HD)
        new_s.append((k_s_win, v_s_win, s_fin.transpose(3, 0, 1, 2), unsplit(ps[:, -1])))

    if depth == 0:
        raise ValueError("depth must be positive")
    y_p = x[:n_p].reshape(batch, seq, D)
    y_s = x[n_p:].reshape(st, sb, D).transpose(1, 0, 2)
    stk = lambda sts, i: jnp.stack([s[i] for s in sts])
    return (y_p, y_s,
            stk(new_p, 0), stk(new_p, 1), stk(new_p, 2), stk(new_p, 3),
            stk(new_s, 0), stk(new_s, 1), stk(new_s, 2), stk(new_s, 3))
```

```python
import functools

import jax
import jax.numpy as jnp
from jax import lax
from jax.experimental import pallas as pl
from jax.experimental.pallas import tpu as pltpu

F32 = jnp.float32
BF16 = jnp.bfloat16
HIGHEST = lax.Precision.HIGHEST

LANES = 128
SUBLANES = 8
VMEM_LIMIT = 56 * 1024 * 1024

D = 1024
HD = 64
N_Q = 16
N_KV = 4
ROT = 16
ROPE_THETA = 500000.0
WINDOW = 128
RMS_EPS = 1e-5
LNX_EPS = 64e-5
N_PAIR = D // LANES
LORA_W, LORA_A, LORA_G = 64, 64, 160
LORA_PAD = 512
G_PAD = 256

C_RKV = 0
C_Q = 3072
C_GATE = 4096
C_K = 6144
C_V = 6400
C_LORA = 6656
N_COLS = 7168
TN_IN = 1024
Q_TILE = C_Q // TN_IN
KV_TILE = C_K // TN_IN

RWKV_CHUNK = 64


PAST_LEN = 16384


def _pick(n, cands):
    return next(c for c in cands if n % c == 0)


def _cparams(sem, **kw):
    return pltpu.CompilerParams(dimension_semantics=sem, vmem_limit_bytes=VMEM_LIMIT, **kw)


def _rms(x, g):
    return x * lax.rsqrt(jnp.mean(x * x, -1, keepdims=True) + RMS_EPS) * g


def _sigmoid(x):
    return 1.0 / (1.0 + jnp.exp(-x))


def _rope_chunk(a, c, sa, sb):
    return a * c + pltpu.roll(a, LANES - ROT // 2, 1) * sa + pltpu.roll(a, ROT // 2, 1) * sb


def _inproj_kernel(x_ref, g_ref, w_ref, c_ref, sa_ref, sb_ref, o_ref, h_ref):
    j = pl.program_id(1)

    @pl.when(j == 0)
    def _():
        h_ref[...] = _rms(x_ref[...], g_ref[...]).astype(BF16)

    acc = jnp.dot(h_ref[...], w_ref[...], preferred_element_type=F32)

    def roped(n_chunks):
        c, sa, sb = c_ref[...], sa_ref[...], sb_ref[...]
        parts = [_rope_chunk(acc[:, k * LANES:(k + 1) * LANES], c, sa, sb) for k in range(n_chunks)]
        if n_chunks * LANES < TN_IN:
            parts.append(acc[:, n_chunks * LANES:])
        return jnp.concatenate(parts, axis=1)

    @pl.when(j == Q_TILE)
    def _():
        o_ref[...] = roped(TN_IN // LANES)

    @pl.when(j == KV_TILE)
    def _():
        o_ref[...] = roped(N_KV * HD // LANES)

    @pl.when((j != Q_TILE) & (j != KV_TILE))
    def _():
        o_ref[...] = acc


def _inproj(x, g, w, rope_c, rope_sa, rope_sb, tm):
    n = x.shape[0]
    return pl.pallas_call(
        _inproj_kernel,
        out_shape=jax.ShapeDtypeStruct((n, N_COLS), F32),
        grid=(n // tm, N_COLS // TN_IN),
        in_specs=[
            pl.BlockSpec((tm, D), lambda i, j: (i, 0)),
            pl.BlockSpec((1, D), lambda i, j: (0, 0)),
            pl.BlockSpec((D, TN_IN), lambda i, j: (0, j)),
            pl.BlockSpec((tm, LANES), lambda i, j: (i, 0)),
            pl.BlockSpec((tm, LANES), lambda i, j: (i, 0)),
            pl.BlockSpec((tm, LANES), lambda i, j: (i, 0)),
        ],
        out_specs=pl.BlockSpec((tm, TN_IN), lambda i, j: (i, j)),
        scratch_shapes=[pltpu.VMEM((tm, D), BF16)],
        compiler_params=_cparams(("parallel", "arbitrary")),
        name="inproj",
    )(x, g, w, rope_c, rope_sa, rope_sb)


def _sink_softmax(s, mask, sink):
    s = jnp.where(mask, s * (HD ** -0.5), -jnp.inf)
    m = jnp.maximum(jnp.max(s, -1, keepdims=True), sink)
    p = jnp.exp(s - m)
    return p / (jnp.sum(p, -1, keepdims=True) + jnp.exp(sink - m))


def _dot_nt(a, b, **kw):
    return lax.dot_general(a, b, (((1,), (1,)), ((), ())), preferred_element_type=F32, **kw)


def _dot_tn(a, b, **kw):
    return lax.dot_general(a, b, (((0,), (0,)), ((), ())), preferred_element_type=F32, **kw)


def _attend_group(q2, kc, vc, half, mask, sink_a, sink_b):
    lo = lax.broadcasted_iota(jnp.int32, kc.shape, 1) < HD
    k_sw = pltpu.roll(kc, HD, 1)
    v_sw = pltpu.roll(vc, HD, 1)
    k_lo, k_hi = (kc, k_sw) if half == 0 else (k_sw, kc)
    v_lo, v_hi = (vc, v_sw) if half == 0 else (v_sw, vc)
    ka = jnp.where(lo, k_lo, 0.0).astype(BF16)
    kb = jnp.where(lo, 0.0, k_hi).astype(BF16)
    va = jnp.where(lo, v_lo, 0.0).astype(BF16)
    vb = jnp.where(lo, 0.0, v_hi).astype(BF16)
    pa = _sink_softmax(_dot_nt(q2, ka), mask, sink_a).astype(BF16)
    pb = _sink_softmax(_dot_nt(q2, kb), mask, sink_b).astype(BF16)
    return (jnp.dot(pa, va, preferred_element_type=F32)
            + jnp.dot(pb, vb, preferred_element_type=F32))


def _attn_prompt_kernel(sink_ref, q_ref, kp_ref, kc_ref, vp_ref, vc_ref, o_ref):
    blk = pl.program_id(1)
    w = WINDOW
    q = q_ref[...].astype(BF16)
    k = jnp.concatenate([kp_ref[...], kc_ref[...]], axis=0)
    v = jnp.concatenate([vp_ref[...], vc_ref[...]], axis=0)
    qi = lax.broadcasted_iota(jnp.int32, (2 * w, 2 * w), 0) & (w - 1)
    kj = lax.broadcasted_iota(jnp.int32, (2 * w, 2 * w), 1)
    top = lax.broadcasted_iota(jnp.int32, (2 * w, 1), 0) < w
    mask = (kj > qi) & (kj <= qi + w) & ((kj >= w) | (blk > 0))
    outs = []
    for g in range(N_KV):
        ch, half = divmod(g, 2)
        q2 = jnp.concatenate([q[:, (2 * g) * LANES:(2 * g + 1) * LANES],
                              q[:, (2 * g + 1) * LANES:(2 * g + 2) * LANES]], axis=0)
        sink_a = jnp.where(top, sink_ref[4 * g], sink_ref[4 * g + 2])
        sink_b = jnp.where(top, sink_ref[4 * g + 1], sink_ref[4 * g + 3])
        o = _attend_group(q2, k[:, ch * LANES:(ch + 1) * LANES], v[:, ch * LANES:(ch + 1) * LANES],
                          half, mask, sink_a, sink_b)
        outs += [o[:w], o[w:]]
    o_ref[...] = jnp.concatenate(outs, axis=1).astype(o_ref.dtype)


def _attn_prompt(p_all, sinks, batch, seq):
    nb = seq // WINDOW
    kcol, vcol = C_K // (N_KV * HD), C_V // (N_KV * HD)
    cur = lambda b, i: b * nb + i
    prev = lambda b, i: b * nb + jnp.maximum(i - 1, 0)
    return pl.pallas_call(
        _attn_prompt_kernel,
        out_shape=jax.ShapeDtypeStruct((batch * seq, D), BF16),
        grid=(batch, nb),
        in_specs=[
            pl.BlockSpec(memory_space=pltpu.SMEM),
            pl.BlockSpec((WINDOW, D), lambda b, i: (cur(b, i), C_Q // D)),
            pl.BlockSpec((WINDOW, N_KV * HD), lambda b, i: (prev(b, i), kcol)),
            pl.BlockSpec((WINDOW, N_KV * HD), lambda b, i: (cur(b, i), kcol)),
            pl.BlockSpec((WINDOW, N_KV * HD), lambda b, i: (prev(b, i), vcol)),
            pl.BlockSpec((WINDOW, N_KV * HD), lambda b, i: (cur(b, i), vcol)),
        ],
        out_specs=pl.BlockSpec((WINDOW, D), lambda b, i: (cur(b, i), 0)),
        compiler_params=_cparams(("parallel", "arbitrary")),
        name="attn_prompt",
    )(sinks, p_all, p_all, p_all, p_all, p_all)


ATT_S_BT = 8
T_PAD = 8


def _attn_sample_kernel(sink_ref, q_ref, kn_ref, vn_ref, kc_ref, vc_ref, o_ref):
    tn = q_ref.shape[1]
    nbuf = kc_ref.shape[1]
    rows = 2 * tn
    keys = nbuf + T_PAD
    r = lax.broadcasted_iota(jnp.int32, (rows, keys), 0)
    t = jnp.where(r >= tn, r - tn, r)
    kj = lax.broadcasted_iota(jnp.int32, (rows, keys), 1)
    mask = (kj > t + (nbuf - WINDOW)) & (kj <= t + nbuf)
    top = lax.broadcasted_iota(jnp.int32, (rows, 1), 0) < tn
    zpad = jnp.zeros((T_PAD - tn, N_KV * HD), F32)
    for b in range(ATT_S_BT):
        q = q_ref[b].astype(BF16)
        k = jnp.concatenate([kc_ref[b], kn_ref[b], zpad], axis=0)
        v = jnp.concatenate([vc_ref[b], vn_ref[b], zpad], axis=0)
        outs = []
        for g in range(N_KV):
            ch, half = divmod(g, 2)
            q2 = jnp.concatenate([q[:, (2 * g) * LANES:(2 * g + 1) * LANES],
                                  q[:, (2 * g + 1) * LANES:(2 * g + 2) * LANES]], axis=0)
            sink_a = jnp.where(top, sink_ref[4 * g], sink_ref[4 * g + 2])
            sink_b = jnp.where(top, sink_ref[4 * g + 1], sink_ref[4 * g + 3])
            o = _attend_group(q2, k[:, ch * LANES:(ch + 1) * LANES], v[:, ch * LANES:(ch + 1) * LANES],
                              half, mask, sink_a, sink_b)
            outs += [o[:tn], o[tn:]]
        o_ref[b] = jnp.concatenate(outs, axis=1).astype(o_ref.dtype)


def _attn_sample(q, k_new, v_new, k_cache, v_cache, sinks):
    bsz, tn, _ = q.shape
    nbuf = k_cache.shape[1]
    kvw = N_KV * HD
    blk = lambda w_, r_: pl.BlockSpec((ATT_S_BT, r_, w_), lambda i: (i, 0, 0))
    return pl.pallas_call(
        _attn_sample_kernel,
        out_shape=jax.ShapeDtypeStruct((bsz, tn, D), BF16),
        grid=(bsz // ATT_S_BT,),
        in_specs=[pl.BlockSpec(memory_space=pltpu.SMEM),
                  blk(D, tn), blk(kvw, tn), blk(kvw, tn), blk(kvw, nbuf), blk(kvw, nbuf)],
        out_specs=blk(D, tn),
        compiler_params=_cparams(("parallel",)),
        name="attn_sample",
    )(sinks, q, k_new, v_new, k_cache, v_cache)


def _to_pm(x):
    return jnp.concatenate([x[:, p * LANES:(p + 1) * LANES] for p in range(N_PAIR)], axis=0)


def _from_pm(x):
    r = x.shape[0] // N_PAIR
    return jnp.concatenate([x[p * r:(p + 1) * r] for p in range(N_PAIR)], axis=1)


def _param_pm(v, r):
    return jnp.concatenate(
        [jnp.broadcast_to(v[:, p * LANES:(p + 1) * LANES], (r, LANES)) for p in range(N_PAIR)], axis=0)


def _head_sum(x, ones_bd):
    hi = x.astype(BF16)
    lo = (x - hi.astype(F32)).astype(BF16)
    return (jnp.dot(hi, ones_bd, preferred_element_type=F32)
            + jnp.dot(lo, ones_bd, preferred_element_type=F32))


def _ones_bd():
    r = lax.broadcasted_iota(jnp.int32, (LANES, LANES), 0) // HD
    c = lax.broadcasted_iota(jnp.int32, (LANES, LANES), 1) // HD
    return jnp.where(r == c, 1.0, 0.0).astype(BF16)


def _softplus(z):
    return jnp.maximum(z, 0.0) + jnp.log(1.0 + jnp.exp(-jnp.abs(z)))


def _rwkv_pre(p_rkv, p_lora, prev_rkv, prev_lora, prm):
    rows = p_rkv.shape[0]
    xs = p_rkv + (prev_rkv - p_rkv) * prm["mu_rkv"]
    xl = p_lora + (prev_lora - p_lora) * prm["mu_lora"]
    wd = xl[:, 0:LORA_W]
    ad = xl[:, LORA_W:LORA_W + LORA_A]
    gd = xl[:, LANES:LANES + G_PAD]
    w_pre = prm["w0"] + jnp.dot(jnp.tanh(wd), prm["w_up"], preferred_element_type=F32, precision=HIGHEST)
    a_pre = prm["a0"] + jnp.dot(ad, prm["a_up"], preferred_element_type=F32, precision=HIGHEST)
    g = jnp.dot(_sigmoid(gd), prm["g_up"], preferred_element_type=F32, precision=HIGHEST)
    logw = -jnp.exp(-_softplus(-w_pre) - 0.5)
    a = _to_pm(_sigmoid(a_pre))
    r = _to_pm(xs[:, 0:D])
    k = _to_pm(xs[:, D:2 * D])
    v = _to_pm(xs[:, 2 * D:3 * D])
    kk = k * _param_pm(prm["k_k"], rows)
    nrm = jnp.sqrt(_head_sum(kk * kk, _ones_bd()))
    kk = kk / jnp.maximum(nrm, 1e-12)
    k = k * (1.0 + (a - 1.0) * _param_pm(prm["k_a"], rows))
    return r, k, v, -kk, kk * a, _to_pm(logw), _to_pm(g)


def _rwkv_post(y, r, k, v, g, prm):
    rows = y.shape[0] // N_PAIR
    ones_bd = _ones_bd()
    mean = _head_sum(y, ones_bd) * (1.0 / HD)
    yc = y - mean
    var = _head_sum(yc * yc, ones_bd) * (1.0 / HD)
    yn = yc * lax.rsqrt(var + LNX_EPS) * _param_pm(prm["lnx_g"], rows) + _param_pm(prm["lnx_b"], rows)
    bonus = _head_sum(r * k * _param_pm(prm["r_k"], rows), ones_bd) * v
    return (yn + bonus) * g


_PRM_NAMES = ("mu_rkv", "mu_lora", "w0", "w_up", "a0", "a_up", "g_up", "k_k", "k_a", "r_k", "lnx_g", "lnx_b")


def _prm_specs(prm, n_grid):
    zero = lambda *_: (0, 0)
    return [pl.BlockSpec(prm[n].shape, zero) for n in _PRM_NAMES]


def _stack2(x, lo):
    return jnp.concatenate([jnp.where(lo, x, 0.0), jnp.where(lo, 0.0, x)], axis=1)


def _split_bf16(x):
    hi = x.astype(BF16)
    return hi, (x - hi.astype(F32)).astype(BF16)


def _bdot(a, b, ca, cb, passes):
    dims = (((ca,), (cb,)), ((0,), (0,)))
    if passes == 6:
        return lax.dot_general(a, b, dims, preferred_element_type=F32, precision=HIGHEST)
    dot = lambda x, y: lax.dot_general(x, y, dims, preferred_element_type=F32)
    if passes == 1:
        return dot(a.astype(BF16), b.astype(BF16))
    a_hi, a_lo = _split_bf16(a)
    b_hi, b_lo = _split_bf16(b)
    return dot(a_hi, b_hi) + (dot(a_hi, b_lo) + dot(a_lo, b_hi))


_PASSES = dict(gram=1, inv=1, rhs=1, u=1, y=1, state=1)


def _bmm(a, b, site):
    return _bdot(a, b, 2, 1, _PASSES[site])


def _bmm_nt(a, b, site):
    return _bdot(a, b, 2, 2, _PASSES[site])


def _bmm_tn(a, b, site):
    return _bdot(a, b, 1, 1, _PASSES[site])


def _rwkv_chunk(s, r, k, v, a_, b_, logw):
    c = r.shape[1]
    t_idx = lax.broadcasted_iota(jnp.int32, logw.shape, 1)
    cum = logw
    d = 1
    while d < c:
        cum = cum + jnp.where(t_idx >= d, pltpu.roll(cum, d, 1), 0.0)
        d *= 2
    e_neg = jnp.exp(-cum)
    l_end = cum[:, c - 1:c, :]
    e_end = jnp.exp(l_end)
    lo = lax.broadcasted_iota(jnp.int32, (N_PAIR, c, LANES), 2) < HD
    xa = _stack2(a_ * jnp.exp(cum - logw), lo)
    xr = _stack2(r * jnp.exp(cum), lo)
    bt, kt = b_ * e_neg, k * e_neg
    yb, yk = _stack2(bt, lo), _stack2(kt, lo)
    vs = _stack2(v, lo)
    gmat = _bmm_nt(jnp.concatenate([xa, xr], axis=1), jnp.concatenate([yb, yk], axis=1), "gram")
    n2 = 2 * c
    ri = lax.broadcasted_iota(jnp.int32, (N_PAIR, n2, n2), 1)
    ci = lax.broadcasted_iota(jnp.int32, (N_PAIR, n2, n2), 2)
    same = (ri >= c) == (ci >= c)
    tr, tc = ri & (c - 1), ci & (c - 1)
    strict = same & (tr > tc)
    incl = same & (tr >= tc)
    a_ab = jnp.where(strict, gmat[:, :n2, :n2], 0.0)
    a_ak = jnp.where(strict, gmat[:, :n2, n2:], 0.0)
    a_rb = jnp.where(incl, gmat[:, n2:, :n2], 0.0)
    a_rk = jnp.where(incl, gmat[:, n2:, n2:], 0.0)
    tinv = jnp.where(ri == ci, 1.0, 0.0) + a_ab
    pw = a_ab
    d = 2
    while d < c:
        pw = _bmm(pw, pw, "inv")
        tinv = tinv + _bmm(tinv, pw, "inv")
        d *= 2
    us = _bmm(tinv, _bmm_nt(xa, s, "rhs") + _bmm(a_ak, vs, "rhs"), "u")
    ys = _bmm_nt(xr, s, "y") + _bmm(a_rb, us, "y") + _bmm(a_rk, vs, "y")
    y = ys[:, :c] + ys[:, c:]
    s_new = s * e_end + _bmm_tn(jnp.concatenate([us, vs], axis=1),
                                 jnp.concatenate([yb * e_end, yk * e_end], axis=1), "state")
    return s_new, y


def _rwkv_prompt_kernel(prkv_ref, plora_ref, *rest):
    prm_refs = rest[:len(_PRM_NAMES)]
    o_ref, s_out_ref, s_ref, carry_rkv, carry_lora = rest[len(_PRM_NAMES):]
    ci = pl.program_id(1)
    c = prkv_ref.shape[0]

    @pl.when(ci == 0)
    def _():
        s_ref[...] = jnp.zeros_like(s_ref)
        carry_rkv[...] = jnp.zeros_like(carry_rkv)
        carry_lora[...] = jnp.zeros_like(carry_lora)

    prm = {n: ref[...] for n, ref in zip(_PRM_NAMES, prm_refs)}
    p_rkv, p_lora = prkv_ref[...], plora_ref[...]

    def shifted(x, carry_ref):
        first = lax.broadcasted_iota(jnp.int32, x.shape, 0) == 0
        return jnp.where(first, carry_ref[0:1, :], pltpu.roll(x, 1, 0))

    prev_rkv = shifted(p_rkv, carry_rkv)
    prev_lora = shifted(p_lora, carry_lora)
    carry_rkv[0:1, :] = p_rkv[c - 1:c, :]
    carry_lora[0:1, :] = p_lora[c - 1:c, :]

    r, k, v, a_, b_, logw, g = _rwkv_pre(p_rkv, p_lora, prev_rkv, prev_lora, prm)
    sh = (N_PAIR, c, LANES)
    s_new, y = _rwkv_chunk(s_ref[...], r.reshape(sh), k.reshape(sh), v.reshape(sh),
                           a_.reshape(sh), b_.reshape(sh), logw.reshape(sh))
    s_ref[...] = s_new
    out = _rwkv_post(y.reshape(N_PAIR * c, LANES), r, k, v, g, prm)
    o_ref[...] = _from_pm(out).astype(o_ref.dtype)

    @pl.when(ci == pl.num_programs(1) - 1)
    def _():
        s_out_ref[0] = s_new


def _rwkv_prompt(p_all, prm, batch, seq):
    c = RWKV_CHUNK
    nc = seq // c
    row = lambda b, i: b * nc + i
    return pl.pallas_call(
        _rwkv_prompt_kernel,
        out_shape=(jax.ShapeDtypeStruct((batch * seq, D), BF16),
                   jax.ShapeDtypeStruct((batch, N_PAIR, LANES, LANES), F32)),
        grid=(batch, nc),
        in_specs=[pl.BlockSpec((c, 3 * D), lambda b, i: (row(b, i), C_RKV // (3 * D))),
                  pl.BlockSpec((c, LORA_PAD), lambda b, i: (row(b, i), C_LORA // LORA_PAD))]
                 + _prm_specs(prm, 2),
        out_specs=(pl.BlockSpec((c, D), lambda b, i: (row(b, i), 0)),
                   pl.BlockSpec((1, N_PAIR, LANES, LANES), lambda b, i: (b, 0, 0, 0))),
        scratch_shapes=[pltpu.VMEM((N_PAIR, LANES, LANES), F32),
                        pltpu.VMEM((8, 3 * D), F32),
                        pltpu.VMEM((8, LORA_PAD), F32)],
        compiler_params=_cparams(("parallel", "arbitrary")),
        name="rwkv_prompt",
    )(p_all, p_all, *[prm[n] for n in _PRM_NAMES])


def _rwkv_sample_pre_kernel(prkv_ref, plora_ref, qrkv_ref, qlora_ref, srkv_ref, slora_ref, *rest):
    prm_refs = rest[:len(_PRM_NAMES)]
    outs = rest[len(_PRM_NAMES):]
    t = pl.program_id(0)
    prm = {n: ref[...] for n, ref in zip(_PRM_NAMES, prm_refs)}
    first = t == 0
    prev_rkv = jnp.where(first, srkv_ref[...], qrkv_ref[...])
    prev_lora = jnp.where(first, slora_ref[...], qlora_ref[...])
    r, k, v, a_, b_, logw, g = _rwkv_pre(prkv_ref[...], plora_ref[...], prev_rkv, prev_lora, prm)
    for ref, val in zip(outs, (r, jnp.exp(logw), k, v, a_, b_, g)):
        ref[...] = _from_pm(val)


def _rwkv_sample_pre(p_all, row0, shift_rkv, shift_lora, prm, bsz, tn):
    base = row0 // bsz
    cur = lambda t: base + t
    prv = lambda t: base + jnp.maximum(t - 1, 0)
    out = jax.ShapeDtypeStruct((tn * bsz, D), F32)
    return pl.pallas_call(
        _rwkv_sample_pre_kernel,
        out_shape=(out,) * 7,
        grid=(tn,),
        in_specs=[pl.BlockSpec((bsz, 3 * D), lambda t: (cur(t), C_RKV // (3 * D))),
                  pl.BlockSpec((bsz, LORA_PAD), lambda t: (cur(t), C_LORA // LORA_PAD)),
                  pl.BlockSpec((bsz, 3 * D), lambda t: (prv(t), C_RKV // (3 * D))),
                  pl.BlockSpec((bsz, LORA_PAD), lambda t: (prv(t), C_LORA // LORA_PAD)),
                  pl.BlockSpec((bsz, 3 * D), lambda t: (0, 0)),
                  pl.BlockSpec((bsz, LORA_PAD), lambda t: (0, 0))]
                 + _prm_specs(prm, 1),
        out_specs=tuple(pl.BlockSpec((bsz, D), lambda t: (t, 0)) for _ in range(7)),
        compiler_params=_cparams(("arbitrary",)),
        name="rwkv_sample_pre",
    )(p_all, p_all, p_all, p_all, shift_rkv, shift_lora, *[prm[n] for n in _PRM_NAMES])


def _rwkv_sample_scan_kernel(s0_ref, r_ref, w_ref, k_ref, v_ref, a_ref, b_ref, y_ref, so_ref, s_ref):
    bsz = s0_ref.shape[-1]
    tn = r_ref.shape[1] // bsz
    s_ref[...] = s0_ref[0]
    for t in range(tn):
        cols = slice(t * bsz, (t + 1) * bsz)
        r_t, w_t, k_t = r_ref[:, cols], w_ref[:, cols], k_ref[:, cols]
        a_t, b_t = a_ref[:, cols], b_ref[:, cols]

        def body(i, _):
            s_i = s_ref[i]
            sa = jnp.sum(s_i * a_t, axis=0, keepdims=True)
            v_i = v_ref[i, :, cols]
            s_i = s_i * w_t + sa * b_t + v_i * k_t
            s_ref[i] = s_i
            y_ref[i, :, cols] = jnp.sum(s_i * r_t, axis=0, keepdims=True)
            return 0

        lax.fori_loop(0, HD, body, 0)
    so_ref[0] = s_ref[...]


def _rwkv_sample_scan(s0, r, w, k, v, a_, b_):
    nh, _, _, bsz = s0.shape
    tb = r.shape[1]
    vec = pl.BlockSpec((HD, tb), lambda h: (h, 0))
    vec3 = pl.BlockSpec((HD, 1, tb), lambda h: (h, 0, 0))
    st = pl.BlockSpec((1, HD, HD, bsz), lambda h: (h, 0, 0, 0))
    return pl.pallas_call(
        _rwkv_sample_scan_kernel,
        out_shape=(jax.ShapeDtypeStruct((nh * HD, 1, tb), F32), jax.ShapeDtypeStruct(s0.shape, F32)),
        grid=(nh,),
        in_specs=[st, vec, vec, vec, vec3, vec, vec],
        out_specs=(vec3, st),
        scratch_shapes=[pltpu.VMEM((HD, HD, bsz), F32)],
        compiler_params=_cparams(("parallel",)),
        name="rwkv_sample_scan",
    )(s0, r, w, k, v, a_, b_)


def _rwkv_sample_post_kernel(y_ref, r_ref, k_ref, v_ref, g_ref, *rest):
    prm_refs = rest[:len(_PRM_NAMES)]
    o_ref = rest[len(_PRM_NAMES)]
    prm = {n: ref[...] for n, ref in zip(_PRM_NAMES, prm_refs)}
    out = _rwkv_post(_to_pm(y_ref[...]), _to_pm(r_ref[...]), _to_pm(k_ref[...]),
                     _to_pm(v_ref[...]), _to_pm(g_ref[...]), prm)
    o_ref[...] = _from_pm(out).astype(o_ref.dtype)


def _rwkv_sample_post(y, r, k, v, g, prm, bsz):
    n = y.shape[0]
    blk = pl.BlockSpec((bsz, D), lambda t: (t, 0))
    return pl.pallas_call(
        _rwkv_sample_post_kernel,
        out_shape=jax.ShapeDtypeStruct((n, D), BF16),
        grid=(n // bsz,),
        in_specs=[blk] * 5 + _prm_specs(prm, 1),
        out_specs=blk,
        compiler_params=_cparams(("parallel",)),
        name="rwkv_sample_post",
    )(y, r, k, v, g, *[prm[n] for n in _PRM_NAMES])


def _outproj_kernel(att_ref, rw_ref, ga_ref, gr_ref, x_ref, w_ref, g2_ref, *rest, n_experts):
    m = _sigmoid(ga_ref[...]) * att_ref[...].astype(F32) + _sigmoid(gr_ref[...]) * rw_ref[...].astype(F32)
    xn = x_ref[...] + jnp.dot(m.astype(BF16), w_ref[...], preferred_element_type=F32)
    h2 = _rms(xn, g2_ref[...])
    if n_experts:
        router_ref, xo_ref, h2_ref, gate_ref = rest
        logits = jnp.dot(h2, router_ref[...], preferred_element_type=F32, precision=HIGHEST)
        lane = lax.broadcasted_iota(jnp.int32, logits.shape, 1).astype(F32)
        lg = jnp.where(lane < n_experts, logits, -jnp.inf)
        v1 = jnp.max(lg, -1, keepdims=True)
        i1 = jnp.min(jnp.where(lg == v1, lane, float(LANES)), -1, keepdims=True)
        lg2 = jnp.where(lane == i1, -jnp.inf, lg)
        v2 = jnp.max(lg2, -1, keepdims=True)
        i2 = jnp.min(jnp.where(lg2 == v2, lane, float(LANES)), -1, keepdims=True)
        e2 = jnp.exp(v2 - v1)
        den = 1.0 + e2
        gate_ref[...] = (jnp.where(lane == 0.0, i1, 0.0) + jnp.where(lane == 1.0, i2, 0.0)
                         + jnp.where(lane == 2.0, 1.0 / den, 0.0) + jnp.where(lane == 3.0, e2 / den, 0.0))
        h2_ref[...] = h2
    else:
        xo_ref, h2_ref = rest
        h2_ref[...] = h2.astype(BF16)
    xo_ref[...] = xn


def _outproj(att, rw, p_all, x, w_out, g2, router, n_experts, tm):
    n = x.shape[0]
    row = lambda w_: pl.BlockSpec((tm, w_), lambda i: (i, 0))
    in_specs = [row(D), row(D),
                pl.BlockSpec((tm, D), lambda i: (i, C_GATE // D)),
                pl.BlockSpec((tm, D), lambda i: (i, C_GATE // D + 1)),
                row(D),
                pl.BlockSpec((D, D), lambda i: (0, 0)),
                pl.BlockSpec((1, D), lambda i: (0, 0))]
    args = [att, rw, p_all, p_all, x, w_out, g2]
    out_shape = [jax.ShapeDtypeStruct((n, D), F32), jax.ShapeDtypeStruct((n, D), F32 if n_experts else BF16)]
    out_specs = [row(D), row(D)]
    if n_experts:
        in_specs.append(pl.BlockSpec((D, LANES), lambda i: (0, 0)))
        args.append(router)
        out_shape.append(jax.ShapeDtypeStruct((n, LANES), F32))
        out_specs.append(row(LANES))
    return pl.pallas_call(
        functools.partial(_outproj_kernel, n_experts=n_experts),
        out_shape=tuple(out_shape),
        grid=(n // tm,),
        in_specs=in_specs,
        out_specs=tuple(out_specs),
        compiler_params=_cparams(("parallel",)),
        name="outproj",
    )(*args)


def _swiglu_part(h, wg, wu, wd):
    a = jnp.dot(h, wg, preferred_element_type=F32)
    b = jnp.dot(h, wu, preferred_element_type=F32)
    t = (a * _sigmoid(a) * b).astype(BF16)
    return jnp.dot(t, wd, preferred_element_type=F32)


def _ffn_kernel(h_ref, wg_ref, wu_ref, wd_ref, x_ref, *rest, final_norm):
    rest = list(rest)
    gf_ref = rest.pop(0) if final_norm else None
    o_ref, acc_ref = rest
    f = pl.program_id(1)

    @pl.when(f == 0)
    def _():
        acc_ref[...] = jnp.zeros_like(acc_ref)

    acc_ref[...] += _swiglu_part(h_ref[...], wg_ref[...], wu_ref[...], wd_ref[...])

    @pl.when(f == pl.num_programs(1) - 1)
    def _():
        y = x_ref[...] + acc_ref[...]
        o_ref[...] = _rms(y, gf_ref[...]) if final_norm else y


def _ffn(h, wg, wu, wd, x, gf, tm, tf):
    n = x.shape[0]
    fdim = wg.shape[1]
    final_norm = gf is not None
    in_specs = [pl.BlockSpec((tm, D), lambda i, f: (i, 0)),
                pl.BlockSpec((D, tf), lambda i, f: (0, f)),
                pl.BlockSpec((D, tf), lambda i, f: (0, f)),
                pl.BlockSpec((tf, D), lambda i, f: (f, 0)),
                pl.BlockSpec((tm, D), lambda i, f: (i, 0))]
    args = [h, wg, wu, wd, x]
    if final_norm:
        in_specs.append(pl.BlockSpec((1, D), lambda i, f: (0, 0)))
        args.append(gf)
    return pl.pallas_call(
        functools.partial(_ffn_kernel, final_norm=final_norm),
        out_shape=jax.ShapeDtypeStruct((n, D), F32),
        grid=(n // tm, fdim // tf),
        in_specs=in_specs,
        out_specs=pl.BlockSpec((tm, D), lambda i, f: (i, 0)),
        scratch_shapes=[pltpu.VMEM((tm, D), F32)],
        compiler_params=_cparams(("parallel", "arbitrary")),
        name="ffn",
    )(*args)


MOE_TM = 512


def _moe_plan(route, n_experts):
    n = route.shape[0]
    e_flat = route[:, :2].astype(jnp.int32).T.reshape(-1)
    n_asg = e_flat.shape[0]
    onehot = (e_flat[:, None] == jnp.arange(n_experts)[None, :]).astype(jnp.int32)
    csum = jnp.cumsum(onehot, axis=0)
    rank = jnp.take_along_axis(csum, e_flat[:, None], axis=1)[:, 0] - 1
    counts = csum[-1]
    padded = (counts + MOE_TM - 1) // MOE_TM * MOE_TM
    ends = jnp.cumsum(padded)
    offs = ends - padded
    dest = offs[e_flat] + rank
    n_tiles = -(-n_asg // MOE_TM) + n_experts
    n_pos = n_tiles * MOE_TM
    tok = jnp.zeros((n_pos,), jnp.int32).at[dest].set(jnp.tile(jnp.arange(n, dtype=jnp.int32), 2))
    dst = jnp.zeros((n_pos,), jnp.int32).at[dest].set(jnp.arange(n_asg, dtype=jnp.int32))
    start = jnp.arange(n_tiles, dtype=jnp.int32) * MOE_TM
    n_active = ends[-1] // MOE_TM
    tile_e = jnp.sum((start[:, None] >= ends[None, :]).astype(jnp.int32), axis=1)
    last_e = jnp.sum(((n_active - 1) * MOE_TM >= ends).astype(jnp.int32))
    tile_e = jnp.minimum(tile_e, last_e)
    n_valid = jnp.clip(counts[tile_e] - (start - offs[tile_e]), 0, MOE_TM)
    n_valid = jnp.where(start < ends[-1], n_valid, 0)
    return tile_e, n_valid.astype(jnp.int32), n_active.reshape(1).astype(jnp.int32), tok, dst


def _moe_kernel(te_ref, nv_ref, na_ref, tok_ref, dst_ref, h_hbm, wg_ref, wu_ref, wd_ref, o_hbm,
                xbuf, xb, acc, gsem, ssem):
    del te_ref
    j, f = pl.program_id(0), pl.program_id(1)
    nf = pl.num_programs(1)
    n_active = na_ref[0]
    active = j < n_active
    slot = j % 2

    def gather_row(tile, slot_, r):
        tok = tok_ref[tile * MOE_TM + r]
        return pltpu.make_async_copy(h_hbm.at[pl.ds(tok, 1)], xbuf.at[slot_, pl.ds(r, 1)], gsem.at[slot_])

    def gather_start(tile, slot_):
        def body(r, _):
            gather_row(tile, slot_, r).start()
            return 0
        lax.fori_loop(0, MOE_TM, body, 0)

    def gather_wait(slot_):
        pltpu.make_async_copy(h_hbm.at[pl.ds(0, MOE_TM)], xbuf.at[slot_], gsem.at[slot_]).wait()

    def scatter_start(tile):
        def body(r, _):
            row = dst_ref[tile * MOE_TM + r]
            pltpu.make_async_copy(acc.at[pl.ds(r, 1)], o_hbm.at[pl.ds(row, 1)], ssem.at[0]).start()
            return 0
        lax.fori_loop(0, nv_ref[tile], body, 0)

    def scatter_wait(tile):
        n_rows = nv_ref[tile]
        n_grp = pl.multiple_of(n_rows // SUBLANES * SUBLANES, SUBLANES)

        @pl.when(n_grp > 0)
        def _():
            pltpu.make_async_copy(acc.at[pl.ds(0, n_grp)], o_hbm.at[pl.ds(0, n_grp)], ssem.at[0]).wait()

        def body(r, _):
            pltpu.make_async_copy(acc.at[pl.ds(0, 1)], o_hbm.at[pl.ds(0, 1)], ssem.at[0]).wait()
            return 0

        lax.fori_loop(0, n_rows - n_grp, body, 0)

    @pl.when(active & (f == 0))
    def _():
        @pl.when(j == 0)
        def _():
            gather_start(0, 0)

        gather_wait(slot)

        @pl.when(j + 1 < n_active)
        def _():
            gather_start(j + 1, 1 - slot)

        xb[...] = xbuf[slot].astype(BF16)

    @pl.when(active)
    def _():
        part = _swiglu_part(xb[...], wg_ref[0], wu_ref[0], wd_ref[0])

        @pl.when(f == 0)
        def _():
            @pl.when(j > 0)
            def _():
                scatter_wait(j - 1)

            acc[...] = part

        @pl.when(f > 0)
        def _():
            acc[...] += part

        @pl.when(f == nf - 1)
        def _():
            scatter_start(j)

            @pl.when(j == n_active - 1)
            def _():
                scatter_wait(j)


def _moe_experts(h, plan, wg, wu, wd, tf):
    tile_e, n_valid, n_active, tok, dst = plan
    n = h.shape[0]
    fdim = wg.shape[2]
    n_tiles = tile_e.shape[0]
    nf = fdim // tf

    def wmap(j, f, te, nv, na, tok_, dst_):
        return te[j], jnp.where(j < na[0], f, nf - 1)

    grid_spec = pltpu.PrefetchScalarGridSpec(
        num_scalar_prefetch=5,
        grid=(n_tiles, nf),
        in_specs=[pl.BlockSpec(memory_space=pl.ANY),
                  pl.BlockSpec((1, D, tf), lambda *a: (wmap(*a)[0], 0, wmap(*a)[1])),
                  pl.BlockSpec((1, D, tf), lambda *a: (wmap(*a)[0], 0, wmap(*a)[1])),
                  pl.BlockSpec((1, tf, D), lambda *a: (wmap(*a)[0], wmap(*a)[1], 0))],
        out_specs=pl.BlockSpec(memory_space=pl.ANY),
        scratch_shapes=[pltpu.VMEM((2, MOE_TM, D), F32),
                        pltpu.VMEM((MOE_TM, D), BF16),
                        pltpu.VMEM((MOE_TM, D), F32),
                        pltpu.SemaphoreType.DMA((2,)),
                        pltpu.SemaphoreType.DMA((1,))],
    )
    return pl.pallas_call(
        _moe_kernel,
        out_shape=jax.ShapeDtypeStruct((2 * n, D), F32),
        grid_spec=grid_spec,
        compiler_params=_cparams(("arbitrary", "arbitrary"), disable_bounds_checks=True),
        name="moe_experts",
    )(tile_e, n_valid, n_active, tok, dst, h, wg, wu, wd)


def _moe_combine_kernel(x_ref, o1_ref, o2_ref, route_ref, *rest, final_norm):
    rest = list(rest)
    gf_ref = rest.pop(0) if final_norm else None
    (o_ref,) = rest
    route = route_ref[...]
    lane = lax.broadcasted_iota(jnp.int32, route.shape, 1)
    w1 = jnp.sum(jnp.where(lane == 2, route, 0.0), -1, keepdims=True)
    w2 = jnp.sum(jnp.where(lane == 3, route, 0.0), -1, keepdims=True)
    y = x_ref[...] + (w1 * o1_ref[...] + w2 * o2_ref[...])
    o_ref[...] = _rms(y, gf_ref[...]) if final_norm else y


def _moe_combine(x, o, route, gf, tm):
    n = x.shape[0]
    nb = n // tm
    final_norm = gf is not None
    in_specs = [pl.BlockSpec((tm, D), lambda i: (i, 0)),
                pl.BlockSpec((tm, D), lambda i: (i, 0)),
                pl.BlockSpec((tm, D), lambda i: (i + nb, 0)),
                pl.BlockSpec((tm, LANES), lambda i: (i, 0))]
    args = [x, o, o, route]
    if final_norm:
        in_specs.append(pl.BlockSpec((1, D), lambda i: (0, 0)))
        args.append(gf)
    return pl.pallas_call(
        functools.partial(_moe_combine_kernel, final_norm=final_norm),
        out_shape=jax.ShapeDtypeStruct((n, D), F32),
        grid=(nb,),
        in_specs=in_specs,
        out_specs=pl.BlockSpec((tm, D), lambda i: (i, 0)),
        compiler_params=_cparams(("parallel",)),
        name="moe_combine",
    )(*args)


def _split_shift_cols(a):
    pad = jnp.zeros(a.shape[:-1] + (LORA_PAD - (LORA_W + LORA_A + LORA_G),), a.dtype)
    lead = a[..., 3 * D:3 * D + LORA_W + LORA_A]
    gd = a[..., 3 * D + LORA_W + LORA_A:]
    return a[..., :3 * D], jnp.concatenate([lead, gd, pad], -1)


def _relayout_w_in(w):
    q, k, v = w[:, 0:D], w[:, D:D + 256], w[:, D + 256:D + 512]
    pr = w[:, D + 512:D + 512 + 3360]
    gates = w[:, D + 512 + 3360:]
    rkv, lora = _split_shift_cols(pr)
    return jnp.concatenate([rkv, q, gates, k, v, lora], axis=1).astype(BF16)


def _rope_tables(pos):
    inv = ROPE_THETA ** (-jnp.arange(0, ROT, 2, dtype=F32) / ROT)
    ang = pos.astype(F32)[:, None] * inv[None, :]
    cos, sin = jnp.cos(ang), jnp.sin(ang)
    n = pos.shape[0]
    half = ROT // 2
    one = jnp.ones((n, HD - ROT), F32)
    zero = jnp.zeros((n, HD - half), F32)
    c = jnp.concatenate([cos, cos, one], 1)
    sa = jnp.concatenate([-sin, zero], 1)
    sb = jnp.concatenate([jnp.zeros((n, half), F32), sin, jnp.zeros((n, HD - ROT), F32)], 1)
    tile = lambda a: jnp.concatenate([a, a], 1)
    return tile(c), tile(sa), tile(sb)


def _pair_state_to_heads(s):
    even = s[:, :, :HD, :HD]
    odd = s[:, :, HD:, HD:]
    b = s.shape[0]
    return jnp.stack([even, odd], axis=2).reshape(b, 2 * N_PAIR, HD, HD)


def kernel(x_prompt, x_sample, cache_k_win, cache_v_win, state_wkv, state_shift, norm_mix_g, w_in, w_out, attn_sinks, shift_mu, decay_w0, decay_up, iclr_a0, iclr_up, gate_up, key_kk, key_ka, bonus_rk, lnx_g, lnx_b, norm_ffn_g, ffn_w_gate, ffn_w_up, ffn_w_down, moe_router, moe_w_gate, moe_w_up, moe_w_down, norm_final_g):
    batch, seq, _ = x_prompt.shape
    sb, st, _ = x_sample.shape
    depth = w_in.shape[0]
    n_p, n_s = batch * seq, sb * st
    n = n_p + n_s
    n_buf = cache_k_win.shape[2]
    tm = _pick(n, (512, 256, 128))
    tm_moe = _pick(n, (640, 512, 256, 128))

    x = jnp.concatenate([x_prompt.reshape(n_p, D), x_sample.transpose(1, 0, 2).reshape(n_s, D)], 0)
    pos = jnp.concatenate([jnp.tile(jnp.arange(seq), batch),
                           jnp.repeat(PAST_LEN + jnp.arange(st), sb)])
    rope_c, rope_sa, rope_sb = _rope_tables(pos)
    row = lambda a: a.reshape(1, -1)

    new_p, new_s = [], []
    for l in range(depth):
        mu_rkv, mu_lora = _split_shift_cols(row(shift_mu[l]))
        g_up = jnp.concatenate([gate_up[l], jnp.zeros((G_PAD - LORA_G, D), F32)], 0)
        prm = dict(mu_rkv=mu_rkv, mu_lora=mu_lora, w0=row(decay_w0[l]), w_up=decay_up[l],
                   a0=row(iclr_a0[l]), a_up=iclr_up[l], g_up=g_up, k_k=row(key_kk[l]),
                   k_a=row(key_ka[l]), r_k=row(bonus_rk[l]), lnx_g=row(lnx_g[l]), lnx_b=row(lnx_b[l]))
        p_all = _inproj(x, row(norm_mix_g[l]), _relayout_w_in(w_in[l]), rope_c, rope_sa, rope_sb, tm)

        att_p = _attn_prompt(p_all, attn_sinks[l], batch, seq)
        ps = p_all[n_p:].reshape(st, sb, N_COLS).transpose(1, 0, 2)
        k_new, v_new = ps[..., C_K:C_K + 256], ps[..., C_V:C_V + 256]
        k_cache = cache_k_win[l].reshape(sb, n_buf, 256)
        v_cache = cache_v_win[l].reshape(sb, n_buf, 256)
        att_s = _attn_sample(ps[..., C_Q:C_Q + D], k_new, v_new, k_cache, v_cache, attn_sinks[l])
        att = jnp.concatenate([att_p, att_s.transpose(1, 0, 2).reshape(n_s, D)], 0)

        rw_p, s_pairs = _rwkv_prompt(p_all, prm, batch, seq)
        sh_rkv, sh_lora = _split_shift_cols(state_shift[l])
        r_s, w_s, k_s, v_s, a_s, b_s, g_s = _rwkv_sample_pre(p_all, n_p, sh_rkv, sh_lora, prm, sb, st)
        s0 = state_wkv[l].transpose(1, 2, 3, 0)
        y_t, s_fin = _rwkv_sample_scan(s0, r_s.T, w_s.T, k_s.T, v_s.T[:, None, :], a_s.T, b_s.T)
        rw_s = _rwkv_sample_post(y_t[:, 0, :].T, r_s, k_s, v_s, g_s, prm, sb)
        rw = jnp.concatenate([rw_p, rw_s], 0)

        is_moe = l % 2 == 1
        last = l == depth - 1
        gf = row(norm_final_g) if last else None
        if is_moe:
            ne = moe_router.shape[-1]
            router = jnp.concatenate([moe_router[l // 2], jnp.zeros((D, LANES - ne), F32)], 1)
            x_mid, h2, route = _outproj(att, rw, p_all, x, w_out[l].astype(BF16), row(norm_ffn_g[l]), router, ne, tm)
            o_exp = _moe_experts(h2, _moe_plan(route, ne), moe_w_gate[l // 2].astype(BF16),
                                 moe_w_up[l // 2].astype(BF16), moe_w_down[l // 2].astype(BF16),
                                 _pick(moe_w_gate.shape[-1], (896, 512, 256, 128)))
            x = _moe_combine(x_mid, o_exp, route, gf, tm)
        else:
            x_mid, h2 = _outproj(att, rw, p_all, x, w_out[l].astype(BF16), row(norm_ffn_g[l]), None, 0, tm)
            x = _ffn(h2, ffn_w_gate[l // 2].astype(BF16), ffn_w_up[l // 2].astype(BF16),
                     ffn_w_down[l // 2].astype(BF16), x_mid, gf, tm,
                     _pick(ffn_w_gate.shape[-1], (1408, 512, 256, 128)))

        n_win = min(WINDOW, seq)
        tail = lambda rows, c0, w_: jnp.stack(
            [lax.slice(p_all, ((b + 1) * seq - rows, c0), ((b + 1) * seq, c0 + w_)) for b in range(batch)])
        k_p = tail(n_win, C_K, N_KV * HD).reshape(batch, n_win, N_KV, HD)
        v_p = tail(n_win, C_V, N_KV * HD).reshape(batch, n_win, N_KV, HD)
        unsplit = lambda a: jnp.concatenate(
            [a[..., C_RKV:C_RKV + 3 * D], a[..., C_LORA:C_LORA + LORA_W + LORA_A + LORA_G]], -1)
        last_p = jnp.concatenate([tail(1, C_RKV, 3 * D), tail(1, C_LORA, LORA_W + LORA_A + LORA_G)], -1)[:, 0]
        new_p.append((k_p, v_p, _pair_state_to_heads(s_pairs), last_p))
        k_s_win = jnp.concatenate([k_cache, k_new], 1)[:, -n_buf:].reshape(sb, n_buf, N_KV, HD)
        v_s_win = jnp.concatenate([v_cache, v_new], 1)[:, -n_buf:].reshape(sb, n_buf, N_KV, HD)
        new_s.append((k_s_win, v_s_win, s_fin.transpose(3, 0, 1, 2), unsplit(ps[:, -1])))

    if depth == 0:
        raise ValueError("depth must be positive")
    y_p = x[:n_p].reshape(batch, seq, D)
    y_s = x[n_p:].reshape(st, sb, D).transpose(1, 0, 2)
    stk = lambda sts, i: jnp.stack([s[i] for s in sts])
    return (y_p, y_s,
            stk(new_p, 0), stk(new_p, 1), stk(new_p, 2), stk(new_p, 3),
            stk(new_s, 0), stk(new_s, 1), stk(new_s, 2), stk(new_s, 3))
```

```python
import functools

import jax
import jax.numpy as jnp
from jax import lax
from jax.experimental import pallas as pl
from jax.experimental.pallas import tpu as pltpu

F32 = jnp.float32
BF16 = jnp.bfloat16
HIGHEST = lax.Precision.HIGHEST

LANES = 128
SUBLANES = 8
VMEM_LIMIT = 56 * 1024 * 1024

D = 1024
HD = 64
N_Q = 16
N_KV = 4
ROT = 16
ROPE_THETA = 500000.0
WINDOW = 128
RMS_EPS = 1e-5
LNX_EPS = 64e-5
N_PAIR = D // LANES
LORA_W, LORA_A, LORA_G = 64, 64, 160
LORA_PAD = 512
G_PAD = 256

C_RKV = 0
C_Q = 3072
C_GATE = 4096
C_K = 6144
C_V = 6400
C_LORA = 6656
N_COLS = 7168
TN_IN = 1024
Q_TILE = C_Q // TN_IN
KV_TILE = C_K // TN_IN

RWKV_CHUNK = 64


PAST_LEN = 16384


def _pick(n, cands):
    return next(c for c in cands if n % c == 0)


def _cparams(sem, **kw):
    return pltpu.CompilerParams(dimension_semantics=sem, vmem_limit_bytes=VMEM_LIMIT, **kw)


def _rms(x, g):
    return x * lax.rsqrt(jnp.mean(x * x, -1, keepdims=True) + RMS_EPS) * g


def _sigmoid(x):
    return 1.0 / (1.0 + jnp.exp(-x))


def _rope_chunk(a, c, sa, sb):
    return a * c + pltpu.roll(a, LANES - ROT // 2, 1) * sa + pltpu.roll(a, ROT // 2, 1) * sb


def _inproj_kernel(xp_ref, xs_ref, g_ref, w_ref, cp_ref, sap_ref, sbp_ref, cs_ref, sas_ref, sbs_ref,
                   o_ref, h_ref, *, n_prompt_tiles):
    i, j = pl.program_id(0), pl.program_id(1)
    is_prompt = i < n_prompt_tiles

    @pl.when(j == 0)
    def _():
        x = jnp.where(is_prompt, xp_ref[...], xs_ref[...])
        h_ref[...] = _rms(x, g_ref[...]).astype(BF16)

    acc = jnp.dot(h_ref[...], w_ref[...], preferred_element_type=F32)

    def roped(n_chunks):
        c = jnp.where(is_prompt, cp_ref[...], cs_ref[...])
        sa = jnp.where(is_prompt, sap_ref[...], sas_ref[...])
        sb = jnp.where(is_prompt, sbp_ref[...], sbs_ref[...])
        parts = [_rope_chunk(acc[:, k * LANES:(k + 1) * LANES], c, sa, sb) for k in range(n_chunks)]
        if n_chunks * LANES < TN_IN:
            parts.append(acc[:, n_chunks * LANES:])
        return jnp.concatenate(parts, axis=1)

    @pl.when(j == Q_TILE)
    def _():
        o_ref[...] = roped(TN_IN // LANES)

    @pl.when(j == KV_TILE)
    def _():
        o_ref[...] = roped(N_KV * HD // LANES)

    @pl.when((j != Q_TILE) & (j != KV_TILE))
    def _():
        o_ref[...] = acc


def _two_part_specs(block, npt, s_base, period=None):
    def p_map(i, *_):
        ip = jnp.minimum(i, npt - 1)
        return (ip % period if period else ip, 0)

    def s_map(i, *_):
        return (s_base + jnp.maximum(i - npt, 0), 0)

    return pl.BlockSpec(block, p_map), pl.BlockSpec(block, s_map)


def _inproj(x_p, x_s, s_base, npt, n, g, w, rope_p, rope_s, seq, tm):
    xp_spec, xs_spec = _two_part_specs((tm, D), npt, s_base)
    rp_spec, rs_spec = _two_part_specs((tm, LANES), npt, 0, period=seq // tm)
    return pl.pallas_call(
        functools.partial(_inproj_kernel, n_prompt_tiles=npt),
        out_shape=jax.ShapeDtypeStruct((n, N_COLS), F32),
        grid=(n // tm, N_COLS // TN_IN),
        in_specs=[xp_spec, xs_spec,
                  pl.BlockSpec((1, D), lambda i, j: (0, 0)),
                  pl.BlockSpec((D, TN_IN), lambda i, j: (0, j)),
                  rp_spec, rp_spec, rp_spec, rs_spec, rs_spec, rs_spec],
        out_specs=pl.BlockSpec((tm, TN_IN), lambda i, j: (i, j)),
        scratch_shapes=[pltpu.VMEM((tm, D), BF16)],
        compiler_params=_cparams(("parallel", "arbitrary")),
        name="inproj",
    )(x_p, x_s, g, w, *rope_p, *rope_s)


def _sink_softmax(s, mask, sink):
    s = jnp.where(mask, s * (HD ** -0.5), -jnp.inf)
    m = jnp.maximum(jnp.max(s, -1, keepdims=True), sink)
    p = jnp.exp(s - m)
    return p / (jnp.sum(p, -1, keepdims=True) + jnp.exp(sink - m))


def _dot_nt(a, b, **kw):
    return lax.dot_general(a, b, (((1,), (1,)), ((), ())), preferred_element_type=F32, **kw)


def _dot_tn(a, b, **kw):
    return lax.dot_general(a, b, (((0,), (0,)), ((), ())), preferred_element_type=F32, **kw)


def _attend_group(q2, kc, vc, half, mask, sink_a, sink_b):
    lo = lax.broadcasted_iota(jnp.int32, kc.shape, 1) < HD
    k_sw = pltpu.roll(kc, HD, 1)
    v_sw = pltpu.roll(vc, HD, 1)
    k_lo, k_hi = (kc, k_sw) if half == 0 else (k_sw, kc)
    v_lo, v_hi = (vc, v_sw) if half == 0 else (v_sw, vc)
    ka = jnp.where(lo, k_lo, 0.0).astype(BF16)
    kb = jnp.where(lo, 0.0, k_hi).astype(BF16)
    va = jnp.where(lo, v_lo, 0.0).astype(BF16)
    vb = jnp.where(lo, 0.0, v_hi).astype(BF16)
    pa = _sink_softmax(_dot_nt(q2, ka), mask, sink_a).astype(BF16)
    pb = _sink_softmax(_dot_nt(q2, kb), mask, sink_b).astype(BF16)
    return (jnp.dot(pa, va, preferred_element_type=F32)
            + jnp.dot(pb, vb, preferred_element_type=F32))


def _attn_prompt_kernel(sink_ref, q_ref, kp_ref, kc_ref, vp_ref, vc_ref, o_ref):
    blk = pl.program_id(1)
    w = WINDOW
    q = q_ref[...].astype(BF16)
    k = jnp.concatenate([kp_ref[...], kc_ref[...]], axis=0)
    v = jnp.concatenate([vp_ref[...], vc_ref[...]], axis=0)
    qi = lax.broadcasted_iota(jnp.int32, (2 * w, 2 * w), 0) & (w - 1)
    kj = lax.broadcasted_iota(jnp.int32, (2 * w, 2 * w), 1)
    top = lax.broadcasted_iota(jnp.int32, (2 * w, 1), 0) < w
    mask = (kj > qi) & (kj <= qi + w) & ((kj >= w) | (blk > 0))
    outs = []
    for g in range(N_KV):
        ch, half = divmod(g, 2)
        q2 = jnp.concatenate([q[:, (2 * g) * LANES:(2 * g + 1) * LANES],
                              q[:, (2 * g + 1) * LANES:(2 * g + 2) * LANES]], axis=0)
        sink_a = jnp.where(top, sink_ref[4 * g], sink_ref[4 * g + 2])
        sink_b = jnp.where(top, sink_ref[4 * g + 1], sink_ref[4 * g + 3])
        o = _attend_group(q2, k[:, ch * LANES:(ch + 1) * LANES], v[:, ch * LANES:(ch + 1) * LANES],
                          half, mask, sink_a, sink_b)
        outs += [o[:w], o[w:]]
    o_ref[...] = jnp.concatenate(outs, axis=1).astype(o_ref.dtype)


def _attn_prompt(p_all, sinks, batch, seq):
    nb = seq // WINDOW
    kcol, vcol = C_K // (N_KV * HD), C_V // (N_KV * HD)
    cur = lambda b, i: b * nb + i
    prev = lambda b, i: b * nb + jnp.maximum(i - 1, 0)
    return pl.pallas_call(
        _attn_prompt_kernel,
        out_shape=jax.ShapeDtypeStruct((batch * seq, D), BF16),
        grid=(batch, nb),
        in_specs=[
            pl.BlockSpec(memory_space=pltpu.SMEM),
            pl.BlockSpec((WINDOW, D), lambda b, i: (cur(b, i), C_Q // D)),
            pl.BlockSpec((WINDOW, N_KV * HD), lambda b, i: (prev(b, i), kcol)),
            pl.BlockSpec((WINDOW, N_KV * HD), lambda b, i: (cur(b, i), kcol)),
            pl.BlockSpec((WINDOW, N_KV * HD), lambda b, i: (prev(b, i), vcol)),
            pl.BlockSpec((WINDOW, N_KV * HD), lambda b, i: (cur(b, i), vcol)),
        ],
        out_specs=pl.BlockSpec((WINDOW, D), lambda b, i: (cur(b, i), 0)),
        compiler_params=_cparams(("parallel", "arbitrary")),
        name="attn_prompt",
    )(sinks, p_all, p_all, p_all, p_all, p_all)


ATT_S_BT = 8
T_PAD = 8


def _attn_sample_kernel(sink_ref, q_ref, kn_ref, vn_ref, kc_ref, vc_ref, o_ref):
    tn = q_ref.shape[1]
    nbuf = kc_ref.shape[1]
    rows = 2 * tn
    keys = nbuf + T_PAD
    r = lax.broadcasted_iota(jnp.int32, (rows, keys), 0)
    t = jnp.where(r >= tn, r - tn, r)
    kj = lax.broadcasted_iota(jnp.int32, (rows, keys), 1)
    mask = (kj > t + (nbuf - WINDOW)) & (kj <= t + nbuf)
    top = lax.broadcasted_iota(jnp.int32, (rows, 1), 0) < tn
    zpad = jnp.zeros((T_PAD - tn, N_KV * HD), F32)
    for b in range(ATT_S_BT):
        q = q_ref[b].astype(BF16)
        k = jnp.concatenate([kc_ref[b], kn_ref[b], zpad], axis=0)
        v = jnp.concatenate([vc_ref[b], vn_ref[b], zpad], axis=0)
        outs = []
        for g in range(N_KV):
            ch, half = divmod(g, 2)
            q2 = jnp.concatenate([q[:, (2 * g) * LANES:(2 * g + 1) * LANES],
                                  q[:, (2 * g + 1) * LANES:(2 * g + 2) * LANES]], axis=0)
            sink_a = jnp.where(top, sink_ref[4 * g], sink_ref[4 * g + 2])
            sink_b = jnp.where(top, sink_ref[4 * g + 1], sink_ref[4 * g + 3])
            o = _attend_group(q2, k[:, ch * LANES:(ch + 1) * LANES], v[:, ch * LANES:(ch + 1) * LANES],
                              half, mask, sink_a, sink_b)
            outs += [o[:tn], o[tn:]]
        o_ref[b] = jnp.concatenate(outs, axis=1).astype(o_ref.dtype)


def _attn_sample(q, k_new, v_new, k_cache, v_cache, sinks):
    bsz, tn, _ = q.shape
    nbuf = k_cache.shape[1]
    kvw = N_KV * HD
    blk = lambda w_, r_: pl.BlockSpec((ATT_S_BT, r_, w_), lambda i: (i, 0, 0))
    return pl.pallas_call(
        _attn_sample_kernel,
        out_shape=jax.ShapeDtypeStruct((bsz, tn, D), BF16),
        grid=(bsz // ATT_S_BT,),
        in_specs=[pl.BlockSpec(memory_space=pltpu.SMEM),
                  blk(D, tn), blk(kvw, tn), blk(kvw, tn), blk(kvw, nbuf), blk(kvw, nbuf)],
        out_specs=blk(D, tn),
        compiler_params=_cparams(("parallel",)),
        name="attn_sample",
    )(sinks, q, k_new, v_new, k_cache, v_cache)


def _to_pm(x):
    return jnp.concatenate([x[:, p * LANES:(p + 1) * LANES] for p in range(N_PAIR)], axis=0)


def _from_pm(x):
    r = x.shape[0] // N_PAIR
    return jnp.concatenate([x[p * r:(p + 1) * r] for p in range(N_PAIR)], axis=1)


def _param_pm(v, r):
    return jnp.concatenate(
        [jnp.broadcast_to(v[:, p * LANES:(p + 1) * LANES], (r, LANES)) for p in range(N_PAIR)], axis=0)


def _head_sum(x, ones_bd):
    hi = x.astype(BF16)
    lo = (x - hi.astype(F32)).astype(BF16)
    return (jnp.dot(hi, ones_bd, preferred_element_type=F32)
            + jnp.dot(lo, ones_bd, preferred_element_type=F32))


def _ones_bd():
    r = lax.broadcasted_iota(jnp.int32, (LANES, LANES), 0) // HD
    c = lax.broadcasted_iota(jnp.int32, (LANES, LANES), 1) // HD
    return jnp.where(r == c, 1.0, 0.0).astype(BF16)


def _softplus(z):
    return jnp.maximum(z, 0.0) + jnp.log(1.0 + jnp.exp(-jnp.abs(z)))


def _rwkv_pre(p_rkv, p_lora, prev_rkv, prev_lora, prm):
    rows = p_rkv.shape[0]
    xs = p_rkv + (prev_rkv - p_rkv) * prm["mu_rkv"]
    xl = p_lora + (prev_lora - p_lora) * prm["mu_lora"]
    wd = xl[:, 0:LORA_W]
    ad = xl[:, LORA_W:LORA_W + LORA_A]
    gd = xl[:, LANES:LANES + G_PAD]
    mm = (((1,), (0,)), ((), ()))
    w_pre = prm["w0"] + _pdot(jnp.tanh(wd), prm["w_up"], mm, _PASSES["lora_w"])
    a_pre = prm["a0"] + _pdot(ad, prm["a_up"], mm, _PASSES["lora_a"])
    g = _pdot(_sigmoid(gd), prm["g_up"], mm, _PASSES["lora_g"])
    logw = -jnp.exp(-_softplus(-w_pre) - 0.5)
    a = _to_pm(_sigmoid(a_pre))
    r = _to_pm(xs[:, 0:D])
    k = _to_pm(xs[:, D:2 * D])
    v = _to_pm(xs[:, 2 * D:3 * D])
    kk = k * _param_pm(prm["k_k"], rows)
    nrm = jnp.sqrt(_head_sum(kk * kk, _ones_bd()))
    kk = kk / jnp.maximum(nrm, 1e-12)
    k = k * (1.0 + (a - 1.0) * _param_pm(prm["k_a"], rows))
    return r, k, v, -kk, kk * a, _to_pm(logw), _to_pm(g)


def _rwkv_post(y, r, k, v, g, prm):
    rows = y.shape[0] // N_PAIR
    ones_bd = _ones_bd()
    mean = _head_sum(y, ones_bd) * (1.0 / HD)
    yc = y - mean
    var = _head_sum(yc * yc, ones_bd) * (1.0 / HD)
    yn = yc * lax.rsqrt(var + LNX_EPS) * _param_pm(prm["lnx_g"], rows) + _param_pm(prm["lnx_b"], rows)
    bonus = _head_sum(r * k * _param_pm(prm["r_k"], rows), ones_bd) * v
    return (yn + bonus) * g


_PRM_NAMES = ("mu_rkv", "mu_lora", "w0", "w_up", "a0", "a_up", "g_up", "k_k", "k_a", "r_k", "lnx_g", "lnx_b")


def _prm_specs(prm, n_grid):
    zero = lambda *_: (0, 0)
    return [pl.BlockSpec(prm[n].shape, zero) for n in _PRM_NAMES]


def _stack2(x, lo):
    return jnp.concatenate([jnp.where(lo, x, 0.0), jnp.where(lo, 0.0, x)], axis=1)


def _split_bf16(x):
    hi = x.astype(BF16)
    return hi, (x - hi.astype(F32)).astype(BF16)


def _pdot(a, b, dims, passes):
    if passes == 6:
        return lax.dot_general(a, b, dims, preferred_element_type=F32, precision=HIGHEST)
    dot = lambda x, y: lax.dot_general(x, y, dims, preferred_element_type=F32)
    if passes == 1:
        return dot(a.astype(BF16), b.astype(BF16))
    a_hi, a_lo = _split_bf16(a)
    b_hi, b_lo = _split_bf16(b)
    return dot(a_hi, b_hi) + (dot(a_hi, b_lo) + dot(a_lo, b_hi))


_PASSES = dict(lora_w=1, lora_a=1, lora_g=1, gram=1, inv=1, rhs=1, u=1, y=1, state=1)


def _bdot(a, b, ca, cb, site):
    return _pdot(a, b, (((ca,), (cb,)), ((0,), (0,))), _PASSES[site])


def _bmm(a, b, site):
    return _bdot(a, b, 2, 1, site)


def _bmm_nt(a, b, site):
    return _bdot(a, b, 2, 2, site)


def _bmm_tn(a, b, site):
    return _bdot(a, b, 1, 1, site)


def _rwkv_chunk(s, r, k, v, a_, b_, logw):
    c = r.shape[1]
    t_idx = lax.broadcasted_iota(jnp.int32, logw.shape, 1)
    cum = logw
    d = 1
    while d < c:
        cum = cum + jnp.where(t_idx >= d, pltpu.roll(cum, d, 1), 0.0)
        d *= 2
    e_neg = jnp.exp(-cum)
    l_end = cum[:, c - 1:c, :]
    e_end = jnp.exp(l_end)
    lo = lax.broadcasted_iota(jnp.int32, (N_PAIR, c, LANES), 2) < HD
    xa = _stack2(a_ * jnp.exp(cum - logw), lo)
    xr = _stack2(r * jnp.exp(cum), lo)
    bt, kt = b_ * e_neg, k * e_neg
    yb, yk = _stack2(bt, lo), _stack2(kt, lo)
    vs = _stack2(v, lo)
    gmat = _bmm_nt(jnp.concatenate([xa, xr], axis=1), jnp.concatenate([yb, yk], axis=1), "gram")
    n2 = 2 * c
    ri = lax.broadcasted_iota(jnp.int32, (N_PAIR, n2, n2), 1)
    ci = lax.broadcasted_iota(jnp.int32, (N_PAIR, n2, n2), 2)
    same = (ri >= c) == (ci >= c)
    tr, tc = ri & (c - 1), ci & (c - 1)
    strict = same & (tr > tc)
    incl = same & (tr >= tc)
    a_ab = jnp.where(strict, gmat[:, :n2, :n2], 0.0)
    a_ak = jnp.where(strict, gmat[:, :n2, n2:], 0.0)
    a_rb = jnp.where(incl, gmat[:, n2:, :n2], 0.0)
    a_rk = jnp.where(incl, gmat[:, n2:, n2:], 0.0)
    tinv = jnp.where(ri == ci, 1.0, 0.0) + a_ab
    pw = a_ab
    d = 2
    while d < c:
        pw = _bmm(pw, pw, "inv")
        tinv = tinv + _bmm(tinv, pw, "inv")
        d *= 2
    us = _bmm(tinv, _bmm_nt(xa, s, "rhs") + _bmm(a_ak, vs, "rhs"), "u")
    ys = _bmm_nt(xr, s, "y") + _bmm(a_rb, us, "y") + _bmm(a_rk, vs, "y")
    y = ys[:, :c] + ys[:, c:]
    s_new = s * e_end + _bmm_tn(jnp.concatenate([us, vs], axis=1),
                                 jnp.concatenate([yb * e_end, yk * e_end], axis=1), "state")
    return s_new, y


def _rwkv_prompt_kernel(prkv_ref, plora_ref, *rest):
    prm_refs = rest[:len(_PRM_NAMES)]
    o_ref, s_out_ref, s_ref, carry_rkv, carry_lora = rest[len(_PRM_NAMES):]
    ci = pl.program_id(1)
    c = prkv_ref.shape[0]

    @pl.when(ci == 0)
    def _():
        s_ref[...] = jnp.zeros_like(s_ref)
        carry_rkv[...] = jnp.zeros_like(carry_rkv)
        carry_lora[...] = jnp.zeros_like(carry_lora)

    prm = {n: ref[...] for n, ref in zip(_PRM_NAMES, prm_refs)}
    p_rkv, p_lora = prkv_ref[...], plora_ref[...]

    def shifted(x, carry_ref):
        first = lax.broadcasted_iota(jnp.int32, x.shape, 0) == 0
        return jnp.where(first, carry_ref[0:1, :], pltpu.roll(x, 1, 0))

    prev_rkv = shifted(p_rkv, carry_rkv)
    prev_lora = shifted(p_lora, carry_lora)
    carry_rkv[0:1, :] = p_rkv[c - 1:c, :]
    carry_lora[0:1, :] = p_lora[c - 1:c, :]

    r, k, v, a_, b_, logw, g = _rwkv_pre(p_rkv, p_lora, prev_rkv, prev_lora, prm)
    sh = (N_PAIR, c, LANES)
    s_new, y = _rwkv_chunk(s_ref[...], r.reshape(sh), k.reshape(sh), v.reshape(sh),
                           a_.reshape(sh), b_.reshape(sh), logw.reshape(sh))
    s_ref[...] = s_new
    out = _rwkv_post(y.reshape(N_PAIR * c, LANES), r, k, v, g, prm)
    o_ref[...] = _from_pm(out).astype(o_ref.dtype)

    @pl.when(ci == pl.num_programs(1) - 1)
    def _():
        s_out_ref[0] = s_new


def _rwkv_prompt(p_all, prm, batch, seq):
    c = RWKV_CHUNK
    nc = seq // c
    row = lambda b, i: b * nc + i
    return pl.pallas_call(
        _rwkv_prompt_kernel,
        out_shape=(jax.ShapeDtypeStruct((batch * seq, D), BF16),
                   jax.ShapeDtypeStruct((batch, N_PAIR, LANES, LANES), F32)),
        grid=(batch, nc),
        in_specs=[pl.BlockSpec((c, 3 * D), lambda b, i: (row(b, i), C_RKV // (3 * D))),
                  pl.BlockSpec((c, LORA_PAD), lambda b, i: (row(b, i), C_LORA // LORA_PAD))]
                 + _prm_specs(prm, 2),
        out_specs=(pl.BlockSpec((c, D), lambda b, i: (row(b, i), 0)),
                   pl.BlockSpec((1, N_PAIR, LANES, LANES), lambda b, i: (b, 0, 0, 0))),
        scratch_shapes=[pltpu.VMEM((N_PAIR, LANES, LANES), F32),
                        pltpu.VMEM((8, 3 * D), F32),
                        pltpu.VMEM((8, LORA_PAD), F32)],
        compiler_params=_cparams(("parallel", "arbitrary")),
        name="rwkv_prompt",
    )(p_all, p_all, *[prm[n] for n in _PRM_NAMES])


def _rwkv_sample_pre_kernel(prkv_ref, plora_ref, qrkv_ref, qlora_ref, srkv_ref, slora_ref, *rest):
    prm_refs = rest[:len(_PRM_NAMES)]
    outs = rest[len(_PRM_NAMES):]
    t = pl.program_id(0)
    prm = {n: ref[...] for n, ref in zip(_PRM_NAMES, prm_refs)}
    first = t == 0
    prev_rkv = jnp.where(first, srkv_ref[...], qrkv_ref[...])
    prev_lora = jnp.where(first, slora_ref[...], qlora_ref[...])
    r, k, v, a_, b_, logw, g = _rwkv_pre(prkv_ref[...], plora_ref[...], prev_rkv, prev_lora, prm)
    for ref, val in zip(outs, (r, jnp.exp(logw), k, v, a_, b_, g)):
        ref[...] = _from_pm(val)


def _rwkv_sample_pre(p_all, row0, shift_rkv, shift_lora, prm, bsz, tn):
    base = row0 // bsz
    cur = lambda t: base + t
    prv = lambda t: base + jnp.maximum(t - 1, 0)
    out = jax.ShapeDtypeStruct((tn * bsz, D), F32)
    return pl.pallas_call(
        _rwkv_sample_pre_kernel,
        out_shape=(out,) * 7,
        grid=(tn,),
        in_specs=[pl.BlockSpec((bsz, 3 * D), lambda t: (cur(t), C_RKV // (3 * D))),
                  pl.BlockSpec((bsz, LORA_PAD), lambda t: (cur(t), C_LORA // LORA_PAD)),
                  pl.BlockSpec((bsz, 3 * D), lambda t: (prv(t), C_RKV // (3 * D))),
                  pl.BlockSpec((bsz, LORA_PAD), lambda t: (prv(t), C_LORA // LORA_PAD)),
                  pl.BlockSpec((bsz, 3 * D), lambda t: (0, 0)),
                  pl.BlockSpec((bsz, LORA_PAD), lambda t: (0, 0))]
                 + _prm_specs(prm, 1),
        out_specs=tuple(pl.BlockSpec((bsz, D), lambda t: (t, 0)) for _ in range(7)),
        compiler_params=_cparams(("arbitrary",)),
        name="rwkv_sample_pre",
    )(p_all, p_all, p_all, p_all, shift_rkv, shift_lora, *[prm[n] for n in _PRM_NAMES])


def _rwkv_sample_scan_kernel(s0_ref, r_ref, w_ref, k_ref, v_ref, a_ref, b_ref, y_ref, so_ref, s_ref):
    bsz = s0_ref.shape[-1]
    tn = r_ref.shape[1] // bsz
    s_ref[...] = s0_ref[0]
    for t in range(tn):
        cols = slice(t * bsz, (t + 1) * bsz)
        r_t, w_t, k_t = r_ref[:, cols], w_ref[:, cols], k_ref[:, cols]
        a_t, b_t = a_ref[:, cols], b_ref[:, cols]

        def body(i, _):
            s_i = s_ref[i]
            sa = jnp.sum(s_i * a_t, axis=0, keepdims=True)
            v_i = v_ref[i, :, cols]
            s_i = s_i * w_t + sa * b_t + v_i * k_t
            s_ref[i] = s_i
            y_ref[i, :, cols] = jnp.sum(s_i * r_t, axis=0, keepdims=True)
            return 0

        lax.fori_loop(0, HD, body, 0)
    so_ref[0] = s_ref[...]


def _rwkv_sample_scan(s0, r, w, k, v, a_, b_):
    nh, _, _, bsz = s0.shape
    tb = r.shape[1]
    vec = pl.BlockSpec((HD, tb), lambda h: (h, 0))
    vec3 = pl.BlockSpec((HD, 1, tb), lambda h: (h, 0, 0))
    st = pl.BlockSpec((1, HD, HD, bsz), lambda h: (h, 0, 0, 0))
    return pl.pallas_call(
        _rwkv_sample_scan_kernel,
        out_shape=(jax.ShapeDtypeStruct((nh * HD, 1, tb), F32), jax.ShapeDtypeStruct(s0.shape, F32)),
        grid=(nh,),
        in_specs=[st, vec, vec, vec, vec3, vec, vec],
        out_specs=(vec3, st),
        scratch_shapes=[pltpu.VMEM((HD, HD, bsz), F32)],
        compiler_params=_cparams(("parallel",)),
        name="rwkv_sample_scan",
    )(s0, r, w, k, v, a_, b_)


def _rwkv_sample_post_kernel(y_ref, r_ref, k_ref, v_ref, g_ref, *rest):
    prm_refs = rest[:len(_PRM_NAMES)]
    o_ref = rest[len(_PRM_NAMES)]
    prm = {n: ref[...] for n, ref in zip(_PRM_NAMES, prm_refs)}
    out = _rwkv_post(_to_pm(y_ref[...]), _to_pm(r_ref[...]), _to_pm(k_ref[...]),
                     _to_pm(v_ref[...]), _to_pm(g_ref[...]), prm)
    o_ref[...] = _from_pm(out).astype(o_ref.dtype)


def _rwkv_sample_post(y, r, k, v, g, prm, bsz):
    n = y.shape[0]
    blk = pl.BlockSpec((bsz, D), lambda t: (t, 0))
    return pl.pallas_call(
        _rwkv_sample_post_kernel,
        out_shape=jax.ShapeDtypeStruct((n, D), BF16),
        grid=(n // bsz,),
        in_specs=[blk] * 5 + _prm_specs(prm, 1),
        out_specs=blk,
        compiler_params=_cparams(("parallel",)),
        name="rwkv_sample_post",
    )(y, r, k, v, g, *[prm[n] for n in _PRM_NAMES])


def _outproj_kernel(attp_ref, atts_ref, rwp_ref, rws_ref, ga_ref, gr_ref, xp_ref, xs_ref, w_ref, g2_ref, *rest,
                    n_experts, n_prompt_tiles):
    is_prompt = pl.program_id(0) < n_prompt_tiles
    att = jnp.where(is_prompt, attp_ref[...], atts_ref[...]).astype(F32)
    rw = jnp.where(is_prompt, rwp_ref[...], rws_ref[...]).astype(F32)
    x = jnp.where(is_prompt, xp_ref[...], xs_ref[...])
    m = _sigmoid(ga_ref[...]) * att + _sigmoid(gr_ref[...]) * rw
    xn = x + jnp.dot(m.astype(BF16), w_ref[...], preferred_element_type=F32)
    h2 = _rms(xn, g2_ref[...])
    if n_experts:
        router_ref, xo_ref, h2_ref, gate_ref = rest
        logits = jnp.dot(h2, router_ref[...], preferred_element_type=F32, precision=HIGHEST)
        lane = lax.broadcasted_iota(jnp.int32, logits.shape, 1).astype(F32)
        lg = jnp.where(lane < n_experts, logits, -jnp.inf)
        v1 = jnp.max(lg, -1, keepdims=True)
        i1 = jnp.min(jnp.where(lg == v1, lane, float(LANES)), -1, keepdims=True)
        lg2 = jnp.where(lane == i1, -jnp.inf, lg)
        v2 = jnp.max(lg2, -1, keepdims=True)
        i2 = jnp.min(jnp.where(lg2 == v2, lane, float(LANES)), -1, keepdims=True)
        e2 = jnp.exp(v2 - v1)
        den = 1.0 + e2
        gate_ref[...] = (jnp.where(lane == 0.0, i1, 0.0) + jnp.where(lane == 1.0, i2, 0.0)
                         + jnp.where(lane == 2.0, 1.0 / den, 0.0) + jnp.where(lane == 3.0, e2 / den, 0.0))
        h2_ref[...] = h2
    else:
        xo_ref, h2_ref = rest
        h2_ref[...] = h2.astype(BF16)
    xo_ref[...] = xn


def _outproj(att_p, att_s, rw_p, rw_s, p_all, x_p, x_s, s_base, w_out, g2, router, n_experts, tm):
    n = p_all.shape[0]
    npt = att_p.shape[0] // tm
    row = lambda w_: pl.BlockSpec((tm, w_), lambda i: (i, 0))
    p_spec, s_spec = _two_part_specs((tm, D), npt, 0)
    xp_spec, xs_spec = _two_part_specs((tm, D), npt, s_base)
    in_specs = [p_spec, s_spec, p_spec, s_spec,
                pl.BlockSpec((tm, D), lambda i: (i, C_GATE // D)),
                pl.BlockSpec((tm, D), lambda i: (i, C_GATE // D + 1)),
                xp_spec, xs_spec,
                pl.BlockSpec((D, D), lambda i: (0, 0)),
                pl.BlockSpec((1, D), lambda i: (0, 0))]
    args = [att_p, att_s, rw_p, rw_s, p_all, p_all, x_p, x_s, w_out, g2]
    out_shape = [jax.ShapeDtypeStruct((n, D), F32), jax.ShapeDtypeStruct((n, D), F32 if n_experts else BF16)]
    out_specs = [row(D), row(D)]
    if n_experts:
        in_specs.append(pl.BlockSpec((D, LANES), lambda i: (0, 0)))
        args.append(router)
        out_shape.append(jax.ShapeDtypeStruct((n, LANES), F32))
        out_specs.append(row(LANES))
    return pl.pallas_call(
        functools.partial(_outproj_kernel, n_experts=n_experts, n_prompt_tiles=npt),
        out_shape=tuple(out_shape),
        grid=(n // tm,),
        in_specs=in_specs,
        out_specs=tuple(out_specs),
        compiler_params=_cparams(("parallel",)),
        name="outproj",
    )(*args)


def _swiglu_part(h, wg, wu, wd):
    a = jnp.dot(h, wg, preferred_element_type=F32)
    b = jnp.dot(h, wu, preferred_element_type=F32)
    t = (a * _sigmoid(a) * b).astype(BF16)
    return jnp.dot(t, wd, preferred_element_type=F32)


def _final_store(y, gf_ref, out_refs, n_prompt_tiles):
    if gf_ref is None:
        (o_ref,) = out_refs
        o_ref[...] = y
        return
    op_ref, os_ref = out_refs
    y = _rms(y, gf_ref[...])
    is_prompt = pl.program_id(0) < n_prompt_tiles

    @pl.when(is_prompt)
    def _():
        op_ref[...] = y

    @pl.when(jnp.logical_not(is_prompt))
    def _():
        os_ref[...] = y


def _final_out(n, n_p, tm, final):
    if not final:
        return jax.ShapeDtypeStruct((n, D), F32), pl.BlockSpec((tm, D), lambda i, *_: (i, 0))
    return ((jax.ShapeDtypeStruct((n_p, D), F32), jax.ShapeDtypeStruct((n - n_p, D), F32)),
            _two_part_specs((tm, D), n_p // tm, 0))


def _ffn_kernel(h_ref, wg_ref, wu_ref, wd_ref, x_ref, *rest, final, n_prompt_tiles):
    rest = list(rest)
    gf_ref = rest.pop(0) if final else None
    acc_ref = rest.pop()
    f = pl.program_id(1)

    @pl.when(f == 0)
    def _():
        acc_ref[...] = jnp.zeros_like(acc_ref)

    acc_ref[...] += _swiglu_part(h_ref[...], wg_ref[...], wu_ref[...], wd_ref[...])

    @pl.when(f == pl.num_programs(1) - 1)
    def _():
        _final_store(x_ref[...] + acc_ref[...], gf_ref, rest, n_prompt_tiles)


def _ffn(h, wg, wu, wd, x, gf, n_p, tm, tf):
    n = x.shape[0]
    fdim = wg.shape[1]
    final = gf is not None
    in_specs = [pl.BlockSpec((tm, D), lambda i, f: (i, 0)),
                pl.BlockSpec((D, tf), lambda i, f: (0, f)),
                pl.BlockSpec((D, tf), lambda i, f: (0, f)),
                pl.BlockSpec((tf, D), lambda i, f: (f, 0)),
                pl.BlockSpec((tm, D), lambda i, f: (i, 0))]
    args = [h, wg, wu, wd, x]
    if final:
        in_specs.append(pl.BlockSpec((1, D), lambda i, f: (0, 0)))
        args.append(gf)
    out_shape, out_specs = _final_out(n, n_p, tm, final)
    return pl.pallas_call(
        functools.partial(_ffn_kernel, final=final, n_prompt_tiles=n_p // tm),
        out_shape=out_shape,
        grid=(n // tm, fdim // tf),
        in_specs=in_specs,
        out_specs=out_specs,
        scratch_shapes=[pltpu.VMEM((tm, D), F32)],
        compiler_params=_cparams(("arbitrary" if final else "parallel", "arbitrary")),
        name="ffn",
    )(*args)


MOE_TM = 512


def _moe_plan(route, n_experts):
    n = route.shape[0]
    e_flat = route[:, :2].astype(jnp.int32).T.reshape(-1)
    n_asg = e_flat.shape[0]
    order = jnp.argsort(e_flat, stable=True).astype(jnp.int32)
    counts = jnp.sum((e_flat[:, None] == jnp.arange(n_experts)[None, :]).astype(jnp.int32), axis=0)
    first = jnp.cumsum(counts) - counts
    padded = (counts + MOE_TM - 1) // MOE_TM * MOE_TM
    ends = jnp.cumsum(padded)
    offs = ends - padded
    n_tiles = -(-n_asg // MOE_TM) + n_experts
    start = jnp.arange(n_tiles, dtype=jnp.int32) * MOE_TM
    n_active = ends[-1] // MOE_TM
    tile_e = jnp.sum((start[:, None] >= ends[None, :]).astype(jnp.int32), axis=1)
    last_e = jnp.sum(((n_active - 1) * MOE_TM >= ends).astype(jnp.int32))
    tile_e = jnp.minimum(tile_e, last_e)
    n_valid = jnp.clip(counts[tile_e] - (start - offs[tile_e]), 0, MOE_TM)
    n_valid = jnp.where(start < ends[-1], n_valid, 0)
    r_in_tile = jnp.arange(MOE_TM, dtype=jnp.int32)[None, :]
    src = (first[tile_e] + start - offs[tile_e])[:, None] + r_in_tile
    real = r_in_tile < n_valid[:, None]
    dst = jnp.where(real, order[jnp.where(real, src, 0)], 0).reshape(-1)
    tok = jnp.where(dst >= n, dst - n, dst)
    return (tile_e.astype(jnp.int32), n_valid.astype(jnp.int32), n_active.reshape(1).astype(jnp.int32),
            tok.astype(jnp.int32), dst.astype(jnp.int32))


def _moe_kernel(te_ref, nv_ref, na_ref, tok_ref, dst_ref, h_hbm, wg_ref, wu_ref, wd_ref, o_hbm,
                xbuf, xb, acc, gsem, ssem):
    del te_ref
    j, f = pl.program_id(0), pl.program_id(1)
    nf = pl.num_programs(1)
    n_active = na_ref[0]
    active = j < n_active
    slot = j % 2

    def gather_row(tile, slot_, r):
        tok = tok_ref[tile * MOE_TM + r]
        return pltpu.make_async_copy(h_hbm.at[pl.ds(tok, 1)], xbuf.at[slot_, pl.ds(r, 1)], gsem.at[slot_])

    def gather_start(tile, slot_):
        def body(r, _):
            gather_row(tile, slot_, r).start()
            return 0
        lax.fori_loop(0, MOE_TM, body, 0, unroll=SUBLANES)

    def gather_wait(slot_):
        pltpu.make_async_copy(h_hbm.at[pl.ds(0, MOE_TM)], xbuf.at[slot_], gsem.at[slot_]).wait()

    def scatter_row(tile, r):
        row = dst_ref[tile * MOE_TM + r]
        pltpu.make_async_copy(acc.at[pl.ds(r, 1)], o_hbm.at[pl.ds(row, 1)], ssem.at[0]).start()

    def scatter_start(tile):
        n_rows = nv_ref[tile]
        n_grp = n_rows // SUBLANES

        def group(gi, _):
            for u in range(SUBLANES):
                scatter_row(tile, gi * SUBLANES + u)
            return 0

        def single(r, _):
            scatter_row(tile, r)
            return 0

        lax.fori_loop(0, n_grp, group, 0)
        lax.fori_loop(n_grp * SUBLANES, n_rows, single, 0)

    def scatter_wait(tile):
        n_rows = nv_ref[tile]
        n_grp = pl.multiple_of(n_rows // SUBLANES * SUBLANES, SUBLANES)

        @pl.when(n_grp > 0)
        def _():
            pltpu.make_async_copy(acc.at[pl.ds(0, n_grp)], o_hbm.at[pl.ds(0, n_grp)], ssem.at[0]).wait()

        def body(r, _):
            pltpu.make_async_copy(acc.at[pl.ds(0, 1)], o_hbm.at[pl.ds(0, 1)], ssem.at[0]).wait()
            return 0

        lax.fori_loop(0, n_rows - n_grp, body, 0)

    @pl.when(active & (f == 0))
    def _():
        @pl.when(j == 0)
        def _():
            gather_start(0, 0)

        gather_wait(slot)

        @pl.when(j + 1 < n_active)
        def _():
            gather_start(j + 1, 1 - slot)

        xb[...] = xbuf[slot].astype(BF16)

    @pl.when(active)
    def _():
        part = _swiglu_part(xb[...], wg_ref[0], wu_ref[0], wd_ref[0])

        @pl.when(f == 0)
        def _():
            @pl.when(j > 0)
            def _():
                scatter_wait(j - 1)

            acc[...] = part

        @pl.when(f > 0)
        def _():
            acc[...] += part

        @pl.when(f == nf - 1)
        def _():
            scatter_start(j)

            @pl.when(j == n_active - 1)
            def _():
                scatter_wait(j)


def _moe_experts(h, plan, wg, wu, wd, tf):
    tile_e, n_valid, n_active, tok, dst = plan
    n = h.shape[0]
    fdim = wg.shape[2]
    n_tiles = tile_e.shape[0]
    nf = fdim // tf

    def wmap(j, f, te, nv, na, tok_, dst_):
        return te[j], jnp.where(j < na[0], f, nf - 1)

    grid_spec = pltpu.PrefetchScalarGridSpec(
        num_scalar_prefetch=5,
        grid=(n_tiles, nf),
        in_specs=[pl.BlockSpec(memory_space=pl.ANY),
                  pl.BlockSpec((1, D, tf), lambda *a: (wmap(*a)[0], 0, wmap(*a)[1])),
                  pl.BlockSpec((1, D, tf), lambda *a: (wmap(*a)[0], 0, wmap(*a)[1])),
                  pl.BlockSpec((1, tf, D), lambda *a: (wmap(*a)[0], wmap(*a)[1], 0))],
        out_specs=pl.BlockSpec(memory_space=pl.ANY),
        scratch_shapes=[pltpu.VMEM((2, MOE_TM, D), F32),
                        pltpu.VMEM((MOE_TM, D), BF16),
                        pltpu.VMEM((MOE_TM, D), F32),
                        pltpu.SemaphoreType.DMA((2,)),
                        pltpu.SemaphoreType.DMA((1,))],
    )
    return pl.pallas_call(
        _moe_kernel,
        out_shape=jax.ShapeDtypeStruct((2 * n, D), F32),
        grid_spec=grid_spec,
        compiler_params=_cparams(("arbitrary", "arbitrary"), disable_bounds_checks=True),
        name="moe_experts",
    )(tile_e, n_valid, n_active, tok, dst, h, wg, wu, wd)


def _moe_combine_kernel(x_ref, o1_ref, o2_ref, route_ref, *rest, final, n_prompt_tiles):
    rest = list(rest)
    gf_ref = rest.pop(0) if final else None
    route = route_ref[...]
    lane = lax.broadcasted_iota(jnp.int32, route.shape, 1)
    w1 = jnp.sum(jnp.where(lane == 2, route, 0.0), -1, keepdims=True)
    w2 = jnp.sum(jnp.where(lane == 3, route, 0.0), -1, keepdims=True)
    _final_store(x_ref[...] + (w1 * o1_ref[...] + w2 * o2_ref[...]), gf_ref, rest, n_prompt_tiles)


def _moe_combine(x, o, route, gf, n_p, tm):
    n = x.shape[0]
    nb = n // tm
    final = gf is not None
    in_specs = [pl.BlockSpec((tm, D), lambda i: (i, 0)),
                pl.BlockSpec((tm, D), lambda i: (i, 0)),
                pl.BlockSpec((tm, D), lambda i: (i + nb, 0)),
                pl.BlockSpec((tm, LANES), lambda i: (i, 0))]
    args = [x, o, o, route]
    if final:
        in_specs.append(pl.BlockSpec((1, D), lambda i: (0, 0)))
        args.append(gf)
    out_shape, out_specs = _final_out(n, n_p, tm, final)
    return pl.pallas_call(
        functools.partial(_moe_combine_kernel, final=final, n_prompt_tiles=n_p // tm),
        out_shape=out_shape,
        grid=(nb,),
        in_specs=in_specs,
        out_specs=out_specs,
        compiler_params=_cparams(("arbitrary",)),
        name="moe_combine",
    )(*args)


def _split_shift_cols(a):
    pad = jnp.zeros(a.shape[:-1] + (LORA_PAD - (LORA_W + LORA_A + LORA_G),), a.dtype)
    lead = a[..., 3 * D:3 * D + LORA_W + LORA_A]
    gd = a[..., 3 * D + LORA_W + LORA_A:]
    return a[..., :3 * D], jnp.concatenate([lead, gd, pad], -1)


def _relayout_w_in(w):
    q, k, v = w[:, 0:D], w[:, D:D + 256], w[:, D + 256:D + 512]
    pr = w[:, D + 512:D + 512 + 3360]
    gates = w[:, D + 512 + 3360:]
    rkv, lora = _split_shift_cols(pr)
    return jnp.concatenate([rkv, q, gates, k, v, lora], axis=1).astype(BF16)


def _rope_tables(pos):
    inv = ROPE_THETA ** (-jnp.arange(0, ROT, 2, dtype=F32) / ROT)
    ang = pos.astype(F32)[:, None] * inv[None, :]
    cos, sin = jnp.cos(ang), jnp.sin(ang)
    n = pos.shape[0]
    half = ROT // 2
    one = jnp.ones((n, HD - ROT), F32)
    zero = jnp.zeros((n, HD - half), F32)
    c = jnp.concatenate([cos, cos, one], 1)
    sa = jnp.concatenate([-sin, zero], 1)
    sb = jnp.concatenate([jnp.zeros((n, half), F32), sin, jnp.zeros((n, HD - ROT), F32)], 1)
    tile = lambda a: jnp.concatenate([a, a], 1)
    return tile(c), tile(sa), tile(sb)


def _pair_state_to_heads(s):
    even = s[:, :, :HD, :HD]
    odd = s[:, :, HD:, HD:]
    b = s.shape[0]
    return jnp.stack([even, odd], axis=2).reshape(b, 2 * N_PAIR, HD, HD)


def kernel(x_prompt, x_sample, cache_k_win, cache_v_win, state_wkv, state_shift, norm_mix_g, w_in, w_out, attn_sinks, shift_mu, decay_w0, decay_up, iclr_a0, iclr_up, gate_up, key_kk, key_ka, bonus_rk, lnx_g, lnx_b, norm_ffn_g, ffn_w_gate, ffn_w_up, ffn_w_down, moe_router, moe_w_gate, moe_w_up, moe_w_down, norm_final_g):
    batch, seq, _ = x_prompt.shape
    sb, st, _ = x_sample.shape
    depth = w_in.shape[0]
    n_p, n_s = batch * seq, sb * st
    n = n_p + n_s
    n_buf = cache_k_win.shape[2]
    if depth == 0:
        raise ValueError("depth must be positive")
    tm = next(c for c in (512, 256, 128) if n_p % c == 0 and n_s % c == 0 and seq % c == 0)
    npt = n_p // tm

    x_p, x_s, s_base = x_prompt.reshape(n_p, D), x_sample.transpose(1, 0, 2).reshape(n_s, D), 0
    rope_p = _rope_tables(jnp.arange(seq))
    rope_s = _rope_tables(jnp.repeat(PAST_LEN + jnp.arange(st), sb))
    row = lambda a: a.reshape(1, -1)

    new_p, new_s = [], []
    for l in range(depth):
        mu_rkv, mu_lora = _split_shift_cols(row(shift_mu[l]))
        g_up = jnp.concatenate([gate_up[l], jnp.zeros((G_PAD - LORA_G, D), F32)], 0)
        prm = dict(mu_rkv=mu_rkv, mu_lora=mu_lora, w0=row(decay_w0[l]), w_up=decay_up[l],
                   a0=row(iclr_a0[l]), a_up=iclr_up[l], g_up=g_up, k_k=row(key_kk[l]),
                   k_a=row(key_ka[l]), r_k=row(bonus_rk[l]), lnx_g=row(lnx_g[l]), lnx_b=row(lnx_b[l]))
        p_all = _inproj(x_p, x_s, s_base, npt, n, row(norm_mix_g[l]), _relayout_w_in(w_in[l]),
                        rope_p, rope_s, seq, tm)

        att_p = _attn_prompt(p_all, attn_sinks[l], batch, seq)
        ps = p_all[n_p:].reshape(st, sb, N_COLS).transpose(1, 0, 2)
        k_new, v_new = ps[..., C_K:C_K + 256], ps[..., C_V:C_V + 256]
        k_cache = cache_k_win[l].reshape(sb, n_buf, 256)
        v_cache = cache_v_win[l].reshape(sb, n_buf, 256)
        att_s = _attn_sample(ps[..., C_Q:C_Q + D], k_new, v_new, k_cache, v_cache, attn_sinks[l])
        att_s = att_s.transpose(1, 0, 2).reshape(n_s, D)

        rw_p, s_pairs = _rwkv_prompt(p_all, prm, batch, seq)
        sh_rkv, sh_lora = _split_shift_cols(state_shift[l])
        r_s, w_s, k_s, v_s, a_s, b_s, g_s = _rwkv_sample_pre(p_all, n_p, sh_rkv, sh_lora, prm, sb, st)
        s0 = state_wkv[l].transpose(1, 2, 3, 0)
        y_t, s_fin = _rwkv_sample_scan(s0, r_s.T, w_s.T, k_s.T, v_s.T[:, None, :], a_s.T, b_s.T)
        rw_s = _rwkv_sample_post(y_t[:, 0, :].T, r_s, k_s, v_s, g_s, prm, sb)

        is_moe = l % 2 == 1
        last = l == depth - 1
        gf = row(norm_final_g) if last else None
        if is_moe:
            ne = moe_router.shape[-1]
            router = jnp.concatenate([moe_router[l // 2], jnp.zeros((D, LANES - ne), F32)], 1)
            x_mid, h2, route = _outproj(att_p, att_s, rw_p, rw_s, p_all, x_p, x_s, s_base, w_out[l].astype(BF16),
                                        row(norm_ffn_g[l]), router, ne, tm)
            o_exp = _moe_experts(h2, _moe_plan(route, ne), moe_w_gate[l // 2].astype(BF16),
                                 moe_w_up[l // 2].astype(BF16), moe_w_down[l // 2].astype(BF16),
                                 _pick(moe_w_gate.shape[-1], (896, 512, 256, 128)))
            x = _moe_combine(x_mid, o_exp, route, gf, n_p, tm)
        else:
            x_mid, h2 = _outproj(att_p, att_s, rw_p, rw_s, p_all, x_p, x_s, s_base, w_out[l].astype(BF16),
                                 row(norm_ffn_g[l]), None, 0, tm)
            x = _ffn(h2, ffn_w_gate[l // 2].astype(BF16), ffn_w_up[l // 2].astype(BF16),
                     ffn_w_down[l // 2].astype(BF16), x_mid, gf, n_p, tm,
                     _pick(ffn_w_gate.shape[-1], (1408, 512, 256, 128)))
        if not last:
            x_p, x_s, s_base = x, x, npt

        n_win = min(WINDOW, seq)
        tail = lambda rows, c0, w_: jnp.stack(
            [lax.slice(p_all, ((b + 1) * seq - rows, c0), ((b + 1) * seq, c0 + w_)) for b in range(batch)])
        k_p = tail(n_win, C_K, N_KV * HD).reshape(batch, n_win, N_KV, HD)
        v_p = tail(n_win, C_V, N_KV * HD).reshape(batch, n_win, N_KV, HD)
        unsplit = lambda a: jnp.concatenate(
            [a[..., C_RKV:C_RKV + 3 * D], a[..., C_LORA:C_LORA + LORA_W + LORA_A + LORA_G]], -1)
        last_p = jnp.concatenate([tail(1, C_RKV, 3 * D), tail(1, C_LORA, LORA_W + LORA_A + LORA_G)], -1)[:, 0]
        new_p.append((k_p, v_p, _pair_state_to_heads(s_pairs), last_p))
        k_s_win = jnp.concatenate([k_cache, k_new], 1)[:, -n_buf:].reshape(sb, n_buf, N_KV, HD)
        v_s_win = jnp.concatenate([v_cache, v_new], 1)[:, -n_buf:].reshape(sb, n_buf, N_KV, HD)
        new_s.append((k_s_win, v_s_win, s_fin.transpose(3, 0, 1, 2), unsplit(ps[:, -1])))

    y_p, y_s = x
    y_p = y_p.reshape(batch, seq, D)
    y_s = y_s.reshape(st, sb, D).transpose(1, 0, 2)
    stk = lambda sts, i: jnp.stack([s[i] for s in sts])
    return (y_p, y_s,
            stk(new_p, 0), stk(new_p, 1), stk(new_p, 2), stk(new_p, 3),
            stk(new_s, 0), stk(new_s, 1), stk(new_s, 2), stk(new_s, 3))
```

```python
import functools

import jax
import jax.numpy as jnp
from jax import lax
from jax.experimental import pallas as pl
from jax.experimental.pallas import tpu as pltpu

F32 = jnp.float32
BF16 = jnp.bfloat16
HIGHEST = lax.Precision.HIGHEST

LANES = 128
SUBLANES = 8
VMEM_LIMIT = 56 * 1024 * 1024

D = 1024
HD = 64
N_Q = 16
N_KV = 4
ROT = 16
ROPE_THETA = 500000.0
WINDOW = 128
RMS_EPS = 1e-5
LNX_EPS = 64e-5
N_PAIR = D // LANES
LORA_W, LORA_A, LORA_G = 64, 64, 160
LORA_PAD = 512
G_PAD = 256

C_RKV = 0
C_Q = 3072
C_GATE = 4096
C_K = 6144
C_V = 6400
C_LORA = 6656
N_COLS = 7168
TN_IN = 1024
Q_TILE = C_Q // TN_IN
KV_TILE = C_K // TN_IN

RWKV_CHUNK = 64


PAST_LEN = 16384


def _pick(n, cands):
    return next(c for c in cands if n % c == 0)


def _cparams(sem, **kw):
    return pltpu.CompilerParams(dimension_semantics=sem, vmem_limit_bytes=VMEM_LIMIT, **kw)


def _rms(x, g):
    return x * lax.rsqrt(jnp.mean(x * x, -1, keepdims=True) + RMS_EPS) * g


def _sigmoid(x):
    return 1.0 / (1.0 + jnp.exp(-x))


def _rope_chunk(a, c, sa, sb):
    return a * c + pltpu.roll(a, LANES - ROT // 2, 1) * sa + pltpu.roll(a, ROT // 2, 1) * sb


def _inproj_kernel(xp_ref, xs_ref, g_ref, w_ref, cp_ref, sap_ref, sbp_ref, cs_ref, sas_ref, sbs_ref,
                   o_ref, h_ref, *, n_prompt_tiles):
    i, j = pl.program_id(0), pl.program_id(1)
    is_prompt = i < n_prompt_tiles

    @pl.when(j == 0)
    def _():
        x = jnp.where(is_prompt, xp_ref[...], xs_ref[...])
        h_ref[...] = _rms(x, g_ref[...]).astype(BF16)

    acc = jnp.dot(h_ref[...], w_ref[...], preferred_element_type=F32)

    def roped(n_chunks):
        c = jnp.where(is_prompt, cp_ref[...], cs_ref[...])
        sa = jnp.where(is_prompt, sap_ref[...], sas_ref[...])
        sb = jnp.where(is_prompt, sbp_ref[...], sbs_ref[...])
        parts = [_rope_chunk(acc[:, k * LANES:(k + 1) * LANES], c, sa, sb) for k in range(n_chunks)]
        if n_chunks * LANES < TN_IN:
            parts.append(acc[:, n_chunks * LANES:])
        return jnp.concatenate(parts, axis=1)

    @pl.when(j == Q_TILE)
    def _():
        o_ref[...] = roped(TN_IN // LANES)

    @pl.when(j == KV_TILE)
    def _():
        o_ref[...] = roped(N_KV * HD // LANES)

    @pl.when((j != Q_TILE) & (j != KV_TILE))
    def _():
        o_ref[...] = acc


def _two_part_specs(block, npt, s_base, period=None):
    def p_map(i, *_):
        ip = jnp.minimum(i, npt - 1)
        return (ip % period if period else ip, 0)

    def s_map(i, *_):
        return (s_base + jnp.maximum(i - npt, 0), 0)

    return pl.BlockSpec(block, p_map), pl.BlockSpec(block, s_map)


def _inproj(x_p, x_s, s_base, npt, n, g, w, rope_p, rope_s, seq, tm):
    xp_spec, xs_spec = _two_part_specs((tm, D), npt, s_base)
    rp_spec, rs_spec = _two_part_specs((tm, LANES), npt, 0, period=seq // tm)
    return pl.pallas_call(
        functools.partial(_inproj_kernel, n_prompt_tiles=npt),
        out_shape=jax.ShapeDtypeStruct((n, N_COLS), F32),
        grid=(n // tm, N_COLS // TN_IN),
        in_specs=[xp_spec, xs_spec,
                  pl.BlockSpec((1, D), lambda i, j: (0, 0)),
                  pl.BlockSpec((D, TN_IN), lambda i, j: (0, j)),
                  rp_spec, rp_spec, rp_spec, rs_spec, rs_spec, rs_spec],
        out_specs=pl.BlockSpec((tm, TN_IN), lambda i, j: (i, j)),
        scratch_shapes=[pltpu.VMEM((tm, D), BF16)],
        compiler_params=_cparams(("parallel", "arbitrary")),
        name="inproj",
    )(x_p, x_s, g, w, *rope_p, *rope_s)


def _sink_softmax(s, mask, sink):
    s = jnp.where(mask, s * (HD ** -0.5), -jnp.inf)
    m = jnp.maximum(jnp.max(s, -1, keepdims=True), sink)
    p = jnp.exp(s - m)
    return p / (jnp.sum(p, -1, keepdims=True) + jnp.exp(sink - m))


def _dot_nt(a, b, **kw):
    return lax.dot_general(a, b, (((1,), (1,)), ((), ())), preferred_element_type=F32, **kw)


def _dot_tn(a, b, **kw):
    return lax.dot_general(a, b, (((0,), (0,)), ((), ())), preferred_element_type=F32, **kw)


def _head_operands(x, half):
    lo = lax.broadcasted_iota(jnp.int32, x.shape, 1) < HD
    sw = pltpu.roll(x, HD, 1)
    x_lo, x_hi = (x, sw) if half == 0 else (sw, x)
    return jnp.where(lo, x_lo, 0.0).astype(BF16), jnp.where(lo, 0.0, x_hi).astype(BF16)


def _group_queries(q, g):
    return jnp.concatenate([q[:, (2 * g) * LANES:(2 * g + 1) * LANES],
                            q[:, (2 * g + 1) * LANES:(2 * g + 2) * LANES]], axis=0)


def _group_sinks(sink_ref, g, top):
    return (jnp.where(top, sink_ref[4 * g], sink_ref[4 * g + 2]),
            jnp.where(top, sink_ref[4 * g + 1], sink_ref[4 * g + 3]))


def _attn_prompt_kernel(sink_ref, q_ref, kp_ref, kc_ref, vp_ref, vc_ref, o_ref):
    blk = pl.program_id(1)
    w = WINDOW
    q = q_ref[...].astype(BF16)
    k = jnp.concatenate([kp_ref[...], kc_ref[...]], axis=0)
    v = jnp.concatenate([vp_ref[...], vc_ref[...]], axis=0)
    qi = lax.broadcasted_iota(jnp.int32, (w, w), 0)
    kj = lax.broadcasted_iota(jnp.int32, (w, w), 1)
    band = jnp.concatenate([kj <= qi] * 2, axis=0)
    mask = band | (blk > 0)
    top = lax.broadcasted_iota(jnp.int32, (2 * w, 1), 0) < w
    outs = []
    for g in range(N_KV):
        ch, half = divmod(g, 2)
        q2 = _group_queries(q, g)
        o = 0.0
        for k_, v_, sink in zip(_head_operands(k[:, ch * LANES:(ch + 1) * LANES], half),
                                _head_operands(v[:, ch * LANES:(ch + 1) * LANES], half),
                                _group_sinks(sink_ref, g, top)):
            s = _dot_nt(q2, k_)
            p = _sink_softmax(jnp.where(band, s[:, w:], s[:, :w]), mask, sink)
            p_cat = jnp.concatenate([jnp.where(band, 0.0, p), jnp.where(band, p, 0.0)], axis=1)
            o = o + jnp.dot(p_cat.astype(BF16), v_, preferred_element_type=F32)
        outs += [o[:w], o[w:]]
    o_ref[...] = jnp.concatenate(outs, axis=1).astype(o_ref.dtype)


def _attn_prompt(p_all, sinks, batch, seq):
    nb = seq // WINDOW
    kcol, vcol = C_K // (N_KV * HD), C_V // (N_KV * HD)
    cur = lambda b, i: b * nb + i
    prev = lambda b, i: b * nb + jnp.maximum(i - 1, 0)
    return pl.pallas_call(
        _attn_prompt_kernel,
        out_shape=jax.ShapeDtypeStruct((batch * seq, D), BF16),
        grid=(batch, nb),
        in_specs=[
            pl.BlockSpec(memory_space=pltpu.SMEM),
            pl.BlockSpec((WINDOW, D), lambda b, i: (cur(b, i), C_Q // D)),
            pl.BlockSpec((WINDOW, N_KV * HD), lambda b, i: (prev(b, i), kcol)),
            pl.BlockSpec((WINDOW, N_KV * HD), lambda b, i: (cur(b, i), kcol)),
            pl.BlockSpec((WINDOW, N_KV * HD), lambda b, i: (prev(b, i), vcol)),
            pl.BlockSpec((WINDOW, N_KV * HD), lambda b, i: (cur(b, i), vcol)),
        ],
        out_specs=pl.BlockSpec((WINDOW, D), lambda b, i: (cur(b, i), 0)),
        compiler_params=_cparams(("parallel", "arbitrary")),
        name="attn_prompt",
    )(sinks, p_all, p_all, p_all, p_all, p_all)


ATT_S_BT = 8
T_PAD = 8


def _attn_sample_kernel(sink_ref, q_ref, kn_ref, vn_ref, kc_ref, vc_ref, o_ref):
    tn = q_ref.shape[1]
    nbuf = kc_ref.shape[1]
    rows = 2 * tn
    keys = nbuf + T_PAD
    r = lax.broadcasted_iota(jnp.int32, (rows, keys), 0)
    t = jnp.where(r >= tn, r - tn, r)
    kj = lax.broadcasted_iota(jnp.int32, (rows, keys), 1)
    mask = (kj > t + (nbuf - WINDOW)) & (kj <= t + nbuf)
    top = lax.broadcasted_iota(jnp.int32, (rows, 1), 0) < tn
    zpad = jnp.zeros((T_PAD - tn, N_KV * HD), F32)
    scores, sinks, values = [], [], []
    for b in range(ATT_S_BT):
        q = q_ref[b].astype(BF16)
        k = jnp.concatenate([kc_ref[b], kn_ref[b], zpad], axis=0)
        v = jnp.concatenate([vc_ref[b], vn_ref[b], zpad], axis=0)
        for g in range(N_KV):
            ch, half = divmod(g, 2)
            q2 = _group_queries(q, g)
            values += list(_head_operands(v[:, ch * LANES:(ch + 1) * LANES], half))
            sinks += list(_group_sinks(sink_ref, g, top))
            scores += [_dot_nt(q2, k_) for k_ in _head_operands(k[:, ch * LANES:(ch + 1) * LANES], half)]
    p = _sink_softmax(jnp.stack(scores), mask[None], jnp.stack(sinks)).astype(BF16)
    for b in range(ATT_S_BT):
        outs = []
        for g in range(N_KV):
            i = 2 * (b * N_KV + g)
            o = (jnp.dot(p[i], values[i], preferred_element_type=F32)
                 + jnp.dot(p[i + 1], values[i + 1], preferred_element_type=F32))
            outs += [o[:tn], o[tn:]]
        o_ref[b] = jnp.concatenate(outs, axis=1).astype(o_ref.dtype)


def _attn_sample(q, k_new, v_new, k_cache, v_cache, sinks):
    bsz, tn, _ = q.shape
    nbuf = k_cache.shape[1]
    kvw = N_KV * HD
    blk = lambda w_, r_: pl.BlockSpec((ATT_S_BT, r_, w_), lambda i: (i, 0, 0))
    return pl.pallas_call(
        _attn_sample_kernel,
        out_shape=jax.ShapeDtypeStruct((bsz, tn, D), BF16),
        grid=(bsz // ATT_S_BT,),
        in_specs=[pl.BlockSpec(memory_space=pltpu.SMEM),
                  blk(D, tn), blk(kvw, tn), blk(kvw, tn), blk(kvw, nbuf), blk(kvw, nbuf)],
        out_specs=blk(D, tn),
        compiler_params=_cparams(("parallel",)),
        name="attn_sample",
    )(sinks, q, k_new, v_new, k_cache, v_cache)


def _to_pm(x):
    return jnp.concatenate([x[:, p * LANES:(p + 1) * LANES] for p in range(N_PAIR)], axis=0)


def _from_pm(x):
    r = x.shape[0] // N_PAIR
    return jnp.concatenate([x[p * r:(p + 1) * r] for p in range(N_PAIR)], axis=1)


def _param_pm(v, r):
    return jnp.concatenate(
        [jnp.broadcast_to(v[:, p * LANES:(p + 1) * LANES], (r, LANES)) for p in range(N_PAIR)], axis=0)


def _head_sum(x, ones_bd):
    hi = x.astype(BF16)
    lo = (x - hi.astype(F32)).astype(BF16)
    return (jnp.dot(hi, ones_bd, preferred_element_type=F32)
            + jnp.dot(lo, ones_bd, preferred_element_type=F32))


def _ones_bd():
    r = lax.broadcasted_iota(jnp.int32, (LANES, LANES), 0) // HD
    c = lax.broadcasted_iota(jnp.int32, (LANES, LANES), 1) // HD
    return jnp.where(r == c, 1.0, 0.0).astype(BF16)


def _softplus(z):
    return jnp.maximum(z, 0.0) + jnp.log(1.0 + jnp.exp(-jnp.abs(z)))


def _rwkv_pre(p_rkv, p_lora, prev_rkv, prev_lora, prm):
    rows = p_rkv.shape[0]
    xs = p_rkv + (prev_rkv - p_rkv) * prm["mu_rkv"]
    xl = p_lora + (prev_lora - p_lora) * prm["mu_lora"]
    wd = xl[:, 0:LORA_W]
    ad = xl[:, LORA_W:LORA_W + LORA_A]
    gd = xl[:, LANES:LANES + G_PAD]
    mm = (((1,), (0,)), ((), ()))
    w_pre = prm["w0"] + _pdot(jnp.tanh(wd), prm["w_up"], mm, _PASSES["lora_w"])
    a_pre = prm["a0"] + _pdot(ad, prm["a_up"], mm, _PASSES["lora_a"])
    g = _pdot(_sigmoid(gd), prm["g_up"], mm, _PASSES["lora_g"])
    logw = -jnp.exp(-_softplus(-w_pre) - 0.5)
    a = _to_pm(_sigmoid(a_pre))
    r = _to_pm(xs[:, 0:D])
    k = _to_pm(xs[:, D:2 * D])
    v = _to_pm(xs[:, 2 * D:3 * D])
    kk = k * _param_pm(prm["k_k"], rows)
    nrm = jnp.sqrt(_head_sum(kk * kk, _ones_bd()))
    kk = kk / jnp.maximum(nrm, 1e-12)
    k = k * (1.0 + (a - 1.0) * _param_pm(prm["k_a"], rows))
    return r, k, v, -kk, kk * a, _to_pm(logw), _to_pm(g)


def _rwkv_post(y, r, k, v, g, prm):
    rows = y.shape[0] // N_PAIR
    ones_bd = _ones_bd()
    mean = _head_sum(y, ones_bd) * (1.0 / HD)
    yc = y - mean
    var = _head_sum(yc * yc, ones_bd) * (1.0 / HD)
    yn = yc * lax.rsqrt(var + LNX_EPS) * _param_pm(prm["lnx_g"], rows) + _param_pm(prm["lnx_b"], rows)
    bonus = _head_sum(r * k * _param_pm(prm["r_k"], rows), ones_bd) * v
    return (yn + bonus) * g


_PRM_NAMES = ("mu_rkv", "mu_lora", "w0", "w_up", "a0", "a_up", "g_up", "k_k", "k_a", "r_k", "lnx_g", "lnx_b")


def _prm_specs(prm, n_grid):
    zero = lambda *_: (0, 0)
    return [pl.BlockSpec(prm[n].shape, zero) for n in _PRM_NAMES]


def _stack2(x, lo):
    return jnp.concatenate([jnp.where(lo, x, 0.0), jnp.where(lo, 0.0, x)], axis=1)


def _split_bf16(x):
    hi = x.astype(BF16)
    return hi, (x - hi.astype(F32)).astype(BF16)


def _pdot(a, b, dims, passes):
    if passes == 6:
        return lax.dot_general(a, b, dims, preferred_element_type=F32, precision=HIGHEST)
    dot = lambda x, y: lax.dot_general(x, y, dims, preferred_element_type=F32)
    if passes == 1:
        return dot(a.astype(BF16), b.astype(BF16))
    a_hi, a_lo = _split_bf16(a)
    b_hi, b_lo = _split_bf16(b)
    return dot(a_hi, b_hi) + (dot(a_hi, b_lo) + dot(a_lo, b_hi))


_PASSES = dict(lora_w=1, lora_a=1, lora_g=1, gram=1, inv=1, rhs=1, u=1, y=1, state=1)


def _bdot(a, b, ca, cb, site):
    return _pdot(a, b, (((ca,), (cb,)), ((0,), (0,))), _PASSES[site])


def _bmm(a, b, site):
    return _bdot(a, b, 2, 1, site)


def _bmm_nt(a, b, site):
    return _bdot(a, b, 2, 2, site)


def _bmm_tn(a, b, site):
    return _bdot(a, b, 1, 1, site)


def _rwkv_chunk(s, r, k, v, a_, b_, logw):
    c = r.shape[1]
    t_idx = lax.broadcasted_iota(jnp.int32, logw.shape, 1)
    cum = logw
    d = 1
    while d < c:
        cum = cum + jnp.where(t_idx >= d, pltpu.roll(cum, d, 1), 0.0)
        d *= 2
    e_neg = jnp.exp(-cum)
    l_end = cum[:, c - 1:c, :]
    e_end = jnp.exp(l_end)
    assert c == HD, "the lane split of (C, 2C) time matrices reuses the head mask"
    lo = lax.broadcasted_iota(jnp.int32, (N_PAIR, c, LANES), 2) < HD
    at = a_ * jnp.exp(cum - logw)
    rt = r * jnp.exp(cum)
    yb, yk = _stack2(b_ * e_neg, lo), _stack2(k * e_neg, lo)
    vs = _stack2(v, lo)
    gmat = _bmm_nt(jnp.concatenate([at, rt], axis=1), jnp.concatenate([yb, yk], axis=1), "gram")
    n2 = 2 * c
    tr = lax.broadcasted_iota(jnp.int32, (N_PAIR, c, n2), 1)
    tc = lax.broadcasted_iota(jnp.int32, (N_PAIR, c, n2), 2) & (c - 1)
    strict, incl = tr > tc, tr >= tc
    a_ab = jnp.where(strict, gmat[:, :c, :n2], 0.0)
    a_ak = jnp.where(strict, gmat[:, :c, n2:], 0.0)
    a_rb = jnp.where(incl, gmat[:, c:, :n2], 0.0)
    a_rk = jnp.where(incl, gmat[:, c:, n2:], 0.0)
    tinv = jnp.where(tr == tc, 1.0, 0.0) + a_ab
    pw = _bmm(a_ab, _stack2(a_ab, lo), "inv")
    d = 4
    while d < c:
        res = _bmm(jnp.concatenate([tinv, pw], axis=1), _stack2(pw, lo), "inv")
        tinv, pw = tinv + res[:, :c], res[:, c:]
        d *= 2
    tinv = tinv + _bmm(tinv, _stack2(pw, lo), "inv")
    rhs = _bmm_nt(at, s, "rhs") + _bmm(a_ak, vs, "rhs")
    us = _stack2(_bmm(tinv, _stack2(rhs, lo), "u"), lo)
    uv = jnp.concatenate([us, vs], axis=1)
    y = _bmm_nt(rt, s, "y") + _bmm(jnp.concatenate([a_rb, a_rk], axis=2), uv, "y")
    s_new = s * e_end + _bmm_tn(uv, jnp.concatenate([yb * e_end, yk * e_end], axis=1), "state")
    return s_new, y


def _rwkv_prompt_kernel(prkv_ref, plora_ref, *rest):
    prm_refs = rest[:len(_PRM_NAMES)]
    o_ref, s_out_ref, s_ref, carry_rkv, carry_lora = rest[len(_PRM_NAMES):]
    ci = pl.program_id(1)
    c = prkv_ref.shape[0]

    @pl.when(ci == 0)
    def _():
        s_ref[...] = jnp.zeros_like(s_ref)
        carry_rkv[...] = jnp.zeros_like(carry_rkv)
        carry_lora[...] = jnp.zeros_like(carry_lora)

    prm = {n: ref[...] for n, ref in zip(_PRM_NAMES, prm_refs)}
    p_rkv, p_lora = prkv_ref[...], plora_ref[...]

    def shifted(x, carry_ref):
        first = lax.broadcasted_iota(jnp.int32, x.shape, 0) == 0
        return jnp.where(first, carry_ref[0:1, :], pltpu.roll(x, 1, 0))

    prev_rkv = shifted(p_rkv, carry_rkv)
    prev_lora = shifted(p_lora, carry_lora)
    carry_rkv[0:1, :] = p_rkv[c - 1:c, :]
    carry_lora[0:1, :] = p_lora[c - 1:c, :]

    r, k, v, a_, b_, logw, g = _rwkv_pre(p_rkv, p_lora, prev_rkv, prev_lora, prm)
    sh = (N_PAIR, c, LANES)
    s_new, y = _rwkv_chunk(s_ref[...], r.reshape(sh), k.reshape(sh), v.reshape(sh),
                           a_.reshape(sh), b_.reshape(sh), logw.reshape(sh))
    s_ref[...] = s_new
    out = _rwkv_post(y.reshape(N_PAIR * c, LANES), r, k, v, g, prm)
    o_ref[...] = _from_pm(out).astype(o_ref.dtype)

    @pl.when(ci == pl.num_programs(1) - 1)
    def _():
        s_out_ref[0] = s_new


def _rwkv_prompt(p_all, prm, batch, seq):
    c = RWKV_CHUNK
    nc = seq // c
    row = lambda b, i: b * nc + i
    return pl.pallas_call(
        _rwkv_prompt_kernel,
        out_shape=(jax.ShapeDtypeStruct((batch * seq, D), BF16),
                   jax.ShapeDtypeStruct((batch, N_PAIR, LANES, LANES), F32)),
        grid=(batch, nc),
        in_specs=[pl.BlockSpec((c, 3 * D), lambda b, i: (row(b, i), C_RKV // (3 * D))),
                  pl.BlockSpec((c, LORA_PAD), lambda b, i: (row(b, i), C_LORA // LORA_PAD))]
                 + _prm_specs(prm, 2),
        out_specs=(pl.BlockSpec((c, D), lambda b, i: (row(b, i), 0)),
                   pl.BlockSpec((1, N_PAIR, LANES, LANES), lambda b, i: (b, 0, 0, 0))),
        scratch_shapes=[pltpu.VMEM((N_PAIR, LANES, LANES), F32),
                        pltpu.VMEM((8, 3 * D), F32),
                        pltpu.VMEM((8, LORA_PAD), F32)],
        compiler_params=_cparams(("parallel", "arbitrary")),
        name="rwkv_prompt",
    )(p_all, p_all, *[prm[n] for n in _PRM_NAMES])


def _rwkv_sample_pre_kernel(prkv_ref, plora_ref, qrkv_ref, qlora_ref, srkv_ref, slora_ref, *rest):
    prm_refs = rest[:len(_PRM_NAMES)]
    outs = rest[len(_PRM_NAMES):]
    t = pl.program_id(0)
    prm = {n: ref[...] for n, ref in zip(_PRM_NAMES, prm_refs)}
    first = t == 0
    prev_rkv = jnp.where(first, srkv_ref[...], qrkv_ref[...])
    prev_lora = jnp.where(first, slora_ref[...], qlora_ref[...])
    r, k, v, a_, b_, logw, g = _rwkv_pre(prkv_ref[...], plora_ref[...], prev_rkv, prev_lora, prm)
    for ref, val in zip(outs, (r, jnp.exp(logw), k, v, a_, b_, g)):
        ref[...] = _from_pm(val)


def _rwkv_sample_pre(p_all, row0, shift_rkv, shift_lora, prm, bsz, tn):
    base = row0 // bsz
    cur = lambda t: base + t
    prv = lambda t: base + jnp.maximum(t - 1, 0)
    out = jax.ShapeDtypeStruct((tn * bsz, D), F32)
    return pl.pallas_call(
        _rwkv_sample_pre_kernel,
        out_shape=(out,) * 7,
        grid=(tn,),
        in_specs=[pl.BlockSpec((bsz, 3 * D), lambda t: (cur(t), C_RKV // (3 * D))),
                  pl.BlockSpec((bsz, LORA_PAD), lambda t: (cur(t), C_LORA // LORA_PAD)),
                  pl.BlockSpec((bsz, 3 * D), lambda t: (prv(t), C_RKV // (3 * D))),
                  pl.BlockSpec((bsz, LORA_PAD), lambda t: (prv(t), C_LORA // LORA_PAD)),
                  pl.BlockSpec((bsz, 3 * D), lambda t: (0, 0)),
                  pl.BlockSpec((bsz, LORA_PAD), lambda t: (0, 0))]
                 + _prm_specs(prm, 1),
        out_specs=tuple(pl.BlockSpec((bsz, D), lambda t: (t, 0)) for _ in range(7)),
        compiler_params=_cparams(("arbitrary",)),
        name="rwkv_sample_pre",
    )(p_all, p_all, p_all, p_all, shift_rkv, shift_lora, *[prm[n] for n in _PRM_NAMES])


def _rwkv_sample_scan_kernel(s0_ref, r_ref, w_ref, k_ref, v_ref, a_ref, b_ref, y_ref, so_ref, s_ref):
    bsz = s0_ref.shape[-1]
    tn = r_ref.shape[1] // bsz
    s_ref[...] = s0_ref[0]
    for t in range(tn):
        cols = slice(t * bsz, (t + 1) * bsz)
        r_t, w_t, k_t = r_ref[:, cols], w_ref[:, cols], k_ref[:, cols]
        a_t, b_t = a_ref[:, cols], b_ref[:, cols]

        def body(i, _):
            s_i = s_ref[i]
            sa = jnp.sum(s_i * a_t, axis=0, keepdims=True)
            v_i = v_ref[i, :, cols]
            s_i = s_i * w_t + sa * b_t + v_i * k_t
            s_ref[i] = s_i
            y_ref[i, :, cols] = jnp.sum(s_i * r_t, axis=0, keepdims=True)
            return 0

        lax.fori_loop(0, HD, body, 0)
    so_ref[0] = s_ref[...]


def _rwkv_sample_scan(s0, r, w, k, v, a_, b_):
    nh, _, _, bsz = s0.shape
    tb = r.shape[1]
    vec = pl.BlockSpec((HD, tb), lambda h: (h, 0))
    vec3 = pl.BlockSpec((HD, 1, tb), lambda h: (h, 0, 0))
    st = pl.BlockSpec((1, HD, HD, bsz), lambda h: (h, 0, 0, 0))
    return pl.pallas_call(
        _rwkv_sample_scan_kernel,
        out_shape=(jax.ShapeDtypeStruct((nh * HD, 1, tb), F32), jax.ShapeDtypeStruct(s0.shape, F32)),
        grid=(nh,),
        in_specs=[st, vec, vec, vec, vec3, vec, vec],
        out_specs=(vec3, st),
        scratch_shapes=[pltpu.VMEM((HD, HD, bsz), F32)],
        compiler_params=_cparams(("parallel",)),
        name="rwkv_sample_scan",
    )(s0, r, w, k, v, a_, b_)


def _rwkv_sample_post_kernel(y_ref, r_ref, k_ref, v_ref, g_ref, *rest):
    prm_refs = rest[:len(_PRM_NAMES)]
    o_ref = rest[len(_PRM_NAMES)]
    prm = {n: ref[...] for n, ref in zip(_PRM_NAMES, prm_refs)}
    out = _rwkv_post(_to_pm(y_ref[...]), _to_pm(r_ref[...]), _to_pm(k_ref[...]),
                     _to_pm(v_ref[...]), _to_pm(g_ref[...]), prm)
    o_ref[...] = _from_pm(out).astype(o_ref.dtype)


def _rwkv_sample_post(y, r, k, v, g, prm, bsz):
    n = y.shape[0]
    blk = pl.BlockSpec((bsz, D), lambda t: (t, 0))
    return pl.pallas_call(
        _rwkv_sample_post_kernel,
        out_shape=jax.ShapeDtypeStruct((n, D), BF16),
        grid=(n // bsz,),
        in_specs=[blk] * 5 + _prm_specs(prm, 1),
        out_specs=blk,
        compiler_params=_cparams(("parallel",)),
        name="rwkv_sample_post",
    )(y, r, k, v, g, *[prm[n] for n in _PRM_NAMES])


def _outproj_kernel(attp_ref, atts_ref, rwp_ref, rws_ref, ga_ref, gr_ref, xp_ref, xs_ref, w_ref, g2_ref, *rest,
                    n_experts, n_prompt_tiles):
    is_prompt = pl.program_id(0) < n_prompt_tiles
    att = jnp.where(is_prompt, attp_ref[...], atts_ref[...]).astype(F32)
    rw = jnp.where(is_prompt, rwp_ref[...], rws_ref[...]).astype(F32)
    x = jnp.where(is_prompt, xp_ref[...], xs_ref[...])
    m = _sigmoid(ga_ref[...]) * att + _sigmoid(gr_ref[...]) * rw
    xn = x + jnp.dot(m.astype(BF16), w_ref[...], preferred_element_type=F32)
    h2 = _rms(xn, g2_ref[...])
    if n_experts:
        router_ref, xo_ref, h2_ref, gate_ref = rest
        logits = jnp.dot(h2, router_ref[...], preferred_element_type=F32, precision=HIGHEST)
        lane = lax.broadcasted_iota(jnp.int32, logits.shape, 1).astype(F32)
        lg = jnp.where(lane < n_experts, logits, -jnp.inf)
        v1 = jnp.max(lg, -1, keepdims=True)
        i1 = jnp.min(jnp.where(lg == v1, lane, float(LANES)), -1, keepdims=True)
        lg2 = jnp.where(lane == i1, -jnp.inf, lg)
        v2 = jnp.max(lg2, -1, keepdims=True)
        i2 = jnp.min(jnp.where(lg2 == v2, lane, float(LANES)), -1, keepdims=True)
        e2 = jnp.exp(v2 - v1)
        den = 1.0 + e2
        gate_ref[...] = (jnp.where(lane == 0.0, i1, 0.0) + jnp.where(lane == 1.0, i2, 0.0)
                         + jnp.where(lane == 2.0, 1.0 / den, 0.0) + jnp.where(lane == 3.0, e2 / den, 0.0))
        h2_ref[...] = h2
    else:
        xo_ref, h2_ref = rest
        h2_ref[...] = h2.astype(BF16)
    xo_ref[...] = xn


def _outproj(att_p, att_s, rw_p, rw_s, p_all, x_p, x_s, s_base, w_out, g2, router, n_experts, tm):
    n = att_p.shape[0] + att_s.shape[0]
    npt = att_p.shape[0] // tm
    row = lambda w_: pl.BlockSpec((tm, w_), lambda i: (i, 0))
    p_spec, s_spec = _two_part_specs((tm, D), npt, 0)
    xp_spec, xs_spec = _two_part_specs((tm, D), npt, s_base)
    in_specs = [p_spec, s_spec, p_spec, s_spec,
                pl.BlockSpec((tm, D), lambda i: (i, C_GATE // D)),
                pl.BlockSpec((tm, D), lambda i: (i, C_GATE // D + 1)),
                xp_spec, xs_spec,
                pl.BlockSpec((D, D), lambda i: (0, 0)),
                pl.BlockSpec((1, D), lambda i: (0, 0))]
    args = [att_p, att_s, rw_p, rw_s, p_all, p_all, x_p, x_s, w_out, g2]
    out_shape = [jax.ShapeDtypeStruct((n, D), F32), jax.ShapeDtypeStruct((n, D), F32 if n_experts else BF16)]
    out_specs = [row(D), row(D)]
    if n_experts:
        in_specs.append(pl.BlockSpec((D, LANES), lambda i: (0, 0)))
        args.append(router)
        out_shape.append(jax.ShapeDtypeStruct((n, LANES), F32))
        out_specs.append(row(LANES))
    return pl.pallas_call(
        functools.partial(_outproj_kernel, n_experts=n_experts, n_prompt_tiles=npt),
        out_shape=tuple(out_shape),
        grid=(n // tm,),
        in_specs=in_specs,
        out_specs=tuple(out_specs),
        compiler_params=_cparams(("parallel",)),
        name="outproj",
    )(*args)


def _swiglu_part(h, wg, wu, wd):
    a = jnp.dot(h, wg, preferred_element_type=F32)
    b = jnp.dot(h, wu, preferred_element_type=F32)
    t = (a * _sigmoid(a) * b).astype(BF16)
    return jnp.dot(t, wd, preferred_element_type=F32)


def _final_store(y, gf_ref, out_refs, n_prompt_tiles):
    if gf_ref is None:
        (o_ref,) = out_refs
        o_ref[...] = y
        return
    op_ref, os_ref = out_refs
    y = _rms(y, gf_ref[...])
    is_prompt = pl.program_id(0) < n_prompt_tiles

    @pl.when(is_prompt)
    def _():
        op_ref[...] = y

    @pl.when(jnp.logical_not(is_prompt))
    def _():
        os_ref[...] = y


def _final_out(n, n_p, tm, final):
    if not final:
        return jax.ShapeDtypeStruct((n, D), F32), pl.BlockSpec((tm, D), lambda i, *_: (i, 0))
    return ((jax.ShapeDtypeStruct((n_p, D), F32), jax.ShapeDtypeStruct((n - n_p, D), F32)),
            _two_part_specs((tm, D), n_p // tm, 0))


def _ffn_kernel(h_ref, wg_ref, wu_ref, wd_ref, x_ref, *rest, final, n_prompt_tiles):
    rest = list(rest)
    gf_ref = rest.pop(0) if final else None
    acc_ref = rest.pop()
    f = pl.program_id(1)

    @pl.when(f == 0)
    def _():
        acc_ref[...] = jnp.zeros_like(acc_ref)

    acc_ref[...] += _swiglu_part(h_ref[...], wg_ref[...], wu_ref[...], wd_ref[...])

    @pl.when(f == pl.num_programs(1) - 1)
    def _():
        _final_store(x_ref[...] + acc_ref[...], gf_ref, rest, n_prompt_tiles)


def _ffn(h, wg, wu, wd, x, gf, n_p, tm, tf):
    n = x.shape[0]
    fdim = wg.shape[1]
    final = gf is not None
    in_specs = [pl.BlockSpec((tm, D), lambda i, f: (i, 0)),
                pl.BlockSpec((D, tf), lambda i, f: (0, f)),
                pl.BlockSpec((D, tf), lambda i, f: (0, f)),
                pl.BlockSpec((tf, D), lambda i, f: (f, 0)),
                pl.BlockSpec((tm, D), lambda i, f: (i, 0))]
    args = [h, wg, wu, wd, x]
    if final:
        in_specs.append(pl.BlockSpec((1, D), lambda i, f: (0, 0)))
        args.append(gf)
    out_shape, out_specs = _final_out(n, n_p, tm, final)
    return pl.pallas_call(
        functools.partial(_ffn_kernel, final=final, n_prompt_tiles=n_p // tm),
        out_shape=out_shape,
        grid=(n // tm, fdim // tf),
        in_specs=in_specs,
        out_specs=out_specs,
        scratch_shapes=[pltpu.VMEM((tm, D), F32)],
        compiler_params=_cparams(("arbitrary" if final else "parallel", "arbitrary")),
        name="ffn",
    )(*args)


MOE_TM = 512


def _moe_plan(route, n_experts):
    n = route.shape[0]
    e_flat = route[:, :2].astype(jnp.int32).T.reshape(-1)
    n_asg = e_flat.shape[0]
    order = jnp.argsort(e_flat, stable=True).astype(jnp.int32)
    counts = jnp.sum((e_flat[:, None] == jnp.arange(n_experts)[None, :]).astype(jnp.int32), axis=0)
    first = jnp.cumsum(counts) - counts
    padded = (counts + MOE_TM - 1) // MOE_TM * MOE_TM
    ends = jnp.cumsum(padded)
    offs = ends - padded
    n_tiles = -(-n_asg // MOE_TM) + n_experts
    start = jnp.arange(n_tiles, dtype=jnp.int32) * MOE_TM
    n_active = ends[-1] // MOE_TM
    tile_e = jnp.sum((start[:, None] >= ends[None, :]).astype(jnp.int32), axis=1)
    last_e = jnp.sum(((n_active - 1) * MOE_TM >= ends).astype(jnp.int32))
    tile_e = jnp.minimum(tile_e, last_e)
    n_valid = jnp.clip(counts[tile_e] - (start - offs[tile_e]), 0, MOE_TM)
    n_valid = jnp.where(start < ends[-1], n_valid, 0)
    r_in_tile = jnp.arange(MOE_TM, dtype=jnp.int32)[None, :]
    src = (first[tile_e] + start - offs[tile_e])[:, None] + r_in_tile
    real = r_in_tile < n_valid[:, None]
    dst = jnp.where(real, order[jnp.where(real, src, 0)], 0).reshape(-1)
    tok = jnp.where(dst >= n, dst - n, dst)
    return (tile_e.astype(jnp.int32), n_valid.astype(jnp.int32), n_active.reshape(1).astype(jnp.int32),
            tok.astype(jnp.int32), dst.astype(jnp.int32))


def _moe_kernel(te_ref, nv_ref, na_ref, tok_ref, dst_ref, h_hbm, wg_ref, wu_ref, wd_ref, o_hbm,
                xbuf, xb, acc, gsem, ssem):
    del te_ref
    j, f = pl.program_id(0), pl.program_id(1)
    nf = pl.num_programs(1)
    n_active = na_ref[0]
    active = j < n_active
    slot = j % 2

    def gather_row(tile, slot_, r):
        tok = tok_ref[tile * MOE_TM + r]
        return pltpu.make_async_copy(h_hbm.at[pl.ds(tok, 1)], xbuf.at[slot_, pl.ds(r, 1)], gsem.at[slot_])

    def gather_start(tile, slot_):
        def body(r, _):
            gather_row(tile, slot_, r).start()
            return 0
        lax.fori_loop(0, MOE_TM, body, 0, unroll=SUBLANES)

    def gather_wait(slot_):
        pltpu.make_async_copy(h_hbm.at[pl.ds(0, MOE_TM)], xbuf.at[slot_], gsem.at[slot_]).wait()

    def scatter_row(tile, r):
        row = dst_ref[tile * MOE_TM + r]
        pltpu.make_async_copy(acc.at[pl.ds(r, 1)], o_hbm.at[pl.ds(row, 1)], ssem.at[0]).start()

    def scatter_start(tile):
        n_rows = nv_ref[tile]
        n_grp = n_rows // SUBLANES

        def group(gi, _):
            for u in range(SUBLANES):
                scatter_row(tile, gi * SUBLANES + u)
            return 0

        def single(r, _):
            scatter_row(tile, r)
            return 0

        lax.fori_loop(0, n_grp, group, 0)
        lax.fori_loop(n_grp * SUBLANES, n_rows, single, 0)

    def scatter_wait(tile):
        n_rows = nv_ref[tile]
        n_grp = pl.multiple_of(n_rows // SUBLANES * SUBLANES, SUBLANES)

        @pl.when(n_grp > 0)
        def _():
            pltpu.make_async_copy(acc.at[pl.ds(0, n_grp)], o_hbm.at[pl.ds(0, n_grp)], ssem.at[0]).wait()

        def body(r, _):
            pltpu.make_async_copy(acc.at[pl.ds(0, 1)], o_hbm.at[pl.ds(0, 1)], ssem.at[0]).wait()
            return 0

        lax.fori_loop(0, n_rows - n_grp, body, 0)

    @pl.when(active & (f == 0))
    def _():
        @pl.when(j == 0)
        def _():
            gather_start(0, 0)

        gather_wait(slot)

        @pl.when(j + 1 < n_active)
        def _():
            gather_start(j + 1, 1 - slot)

        xb[...] = xbuf[slot].astype(BF16)

    @pl.when(active)
    def _():
        part = _swiglu_part(xb[...], wg_ref[0], wu_ref[0], wd_ref[0])

        @pl.when(f == 0)
        def _():
            @pl.when(j > 0)
            def _():
                scatter_wait(j - 1)

            acc[...] = part

        @pl.when(f > 0)
        def _():
            acc[...] += part

        @pl.when(f == nf - 1)
        def _():
            scatter_start(j)

            @pl.when(j == n_active - 1)
            def _():
                scatter_wait(j)


def _moe_experts(h, plan, wg, wu, wd, tf):
    tile_e, n_valid, n_active, tok, dst = plan
    n = h.shape[0]
    fdim = wg.shape[2]
    n_tiles = tile_e.shape[0]
    nf = fdim // tf

    def wmap(j, f, te, nv, na, tok_, dst_):
        return te[j], jnp.where(j < na[0], f, nf - 1)

    grid_spec = pltpu.PrefetchScalarGridSpec(
        num_scalar_prefetch=5,
        grid=(n_tiles, nf),
        in_specs=[pl.BlockSpec(memory_space=pl.ANY),
                  pl.BlockSpec((1, D, tf), lambda *a: (wmap(*a)[0], 0, wmap(*a)[1])),
                  pl.BlockSpec((1, D, tf), lambda *a: (wmap(*a)[0], 0, wmap(*a)[1])),
                  pl.BlockSpec((1, tf, D), lambda *a: (wmap(*a)[0], wmap(*a)[1], 0))],
        out_specs=pl.BlockSpec(memory_space=pl.ANY),
        scratch_shapes=[pltpu.VMEM((2, MOE_TM, D), F32),
                        pltpu.VMEM((MOE_TM, D), BF16),
                        pltpu.VMEM((MOE_TM, D), F32),
                        pltpu.SemaphoreType.DMA((2,)),
                        pltpu.SemaphoreType.DMA((1,))],
    )
    return pl.pallas_call(
        _moe_kernel,
        out_shape=jax.ShapeDtypeStruct((2 * n, D), F32),
        grid_spec=grid_spec,
        compiler_params=_cparams(("arbitrary", "arbitrary"), disable_bounds_checks=True),
        name="moe_experts",
    )(tile_e, n_valid, n_active, tok, dst, h, wg, wu, wd)


def _moe_combine_kernel(x_ref, o1_ref, o2_ref, route_ref, *rest, final, n_prompt_tiles):
    rest = list(rest)
    gf_ref = rest.pop(0) if final else None
    route = route_ref[...]
    lane = lax.broadcasted_iota(jnp.int32, route.shape, 1)
    w1 = jnp.sum(jnp.where(lane == 2, route, 0.0), -1, keepdims=True)
    w2 = jnp.sum(jnp.where(lane == 3, route, 0.0), -1, keepdims=True)
    _final_store(x_ref[...] + (w1 * o1_ref[...] + w2 * o2_ref[...]), gf_ref, rest, n_prompt_tiles)


def _moe_combine(x, o, route, gf, n_p, tm):
    n = x.shape[0]
    nb = n // tm
    final = gf is not None
    in_specs = [pl.BlockSpec((tm, D), lambda i: (i, 0)),
                pl.BlockSpec((tm, D), lambda i: (i, 0)),
                pl.BlockSpec((tm, D), lambda i: (i + nb, 0)),
                pl.BlockSpec((tm, LANES), lambda i: (i, 0))]
    args = [x, o, o, route]
    if final:
        in_specs.append(pl.BlockSpec((1, D), lambda i: (0, 0)))
        args.append(gf)
    out_shape, out_specs = _final_out(n, n_p, tm, final)
    return pl.pallas_call(
        functools.partial(_moe_combine_kernel, final=final, n_prompt_tiles=n_p // tm),
        out_shape=out_shape,
        grid=(nb,),
        in_specs=in_specs,
        out_specs=out_specs,
        compiler_params=_cparams(("arbitrary",)),
        name="moe_combine",
    )(*args)


def _split_shift_cols(a):
    pad = jnp.zeros(a.shape[:-1] + (LORA_PAD - (LORA_W + LORA_A + LORA_G),), a.dtype)
    lead = a[..., 3 * D:3 * D + LORA_W + LORA_A]
    gd = a[..., 3 * D + LORA_W + LORA_A:]
    return a[..., :3 * D], jnp.concatenate([lead, gd, pad], -1)


def _relayout_w_in(w):
    q, k, v = w[:, 0:D], w[:, D:D + 256], w[:, D + 256:D + 512]
    pr = w[:, D + 512:D + 512 + 3360]
    gates = w[:, D + 512 + 3360:]
    rkv, lora = _split_shift_cols(pr)
    return jnp.concatenate([rkv, q, gates, k, v, lora], axis=1).astype(BF16)


def _rope_tables(pos):
    inv = ROPE_THETA ** (-jnp.arange(0, ROT, 2, dtype=F32) / ROT)
    ang = pos.astype(F32)[:, None] * inv[None, :]
    cos, sin = jnp.cos(ang), jnp.sin(ang)
    n = pos.shape[0]
    half = ROT // 2
    one = jnp.ones((n, HD - ROT), F32)
    zero = jnp.zeros((n, HD - half), F32)
    c = jnp.concatenate([cos, cos, one], 1)
    sa = jnp.concatenate([-sin, zero], 1)
    sb = jnp.concatenate([jnp.zeros((n, half), F32), sin, jnp.zeros((n, HD - ROT), F32)], 1)
    tile = lambda a: jnp.concatenate([a, a], 1)
    return tile(c), tile(sa), tile(sb)


def _pair_state_to_heads(s):
    even = s[:, :, :HD, :HD]
    odd = s[:, :, HD:, HD:]
    b = s.shape[0]
    return jnp.stack([even, odd], axis=2).reshape(b, 2 * N_PAIR, HD, HD)


def kernel(x_prompt, x_sample, cache_k_win, cache_v_win, state_wkv, state_shift, norm_mix_g, w_in, w_out, attn_sinks, shift_mu, decay_w0, decay_up, iclr_a0, iclr_up, gate_up, key_kk, key_ka, bonus_rk, lnx_g, lnx_b, norm_ffn_g, ffn_w_gate, ffn_w_up, ffn_w_down, moe_router, moe_w_gate, moe_w_up, moe_w_down, norm_final_g):
    batch, seq, _ = x_prompt.shape
    sb, st, _ = x_sample.shape
    depth = w_in.shape[0]
    n_p, n_s = batch * seq, sb * st
    n = n_p + n_s
    n_buf = cache_k_win.shape[2]
    if depth == 0:
        raise ValueError("depth must be positive")
    tm = next(c for c in (512, 256, 128) if n_p % c == 0 and n_s % c == 0 and seq % c == 0)
    npt = n_p // tm
    tm_in = 2 * tm if n_p % (2 * tm) == 0 and seq % (2 * tm) == 0 else tm
    n_in = -(-n // tm_in) * tm_in
    pad_rows = lambda a: jnp.pad(a, ((0, n_in - n_p - a.shape[0]), (0, 0)))

    x_p, x_s, s_base = x_prompt.reshape(n_p, D), x_sample.transpose(1, 0, 2).reshape(n_s, D), 0
    rope_p = _rope_tables(jnp.arange(seq))
    rope_s = tuple(pad_rows(t) for t in _rope_tables(jnp.repeat(PAST_LEN + jnp.arange(st), sb)))
    row = lambda a: a.reshape(1, -1)

    new_p, new_s = [], []
    for l in range(depth):
        mu_rkv, mu_lora = _split_shift_cols(row(shift_mu[l]))
        g_up = jnp.concatenate([gate_up[l], jnp.zeros((G_PAD - LORA_G, D), F32)], 0)
        prm = dict(mu_rkv=mu_rkv, mu_lora=mu_lora, w0=row(decay_w0[l]), w_up=decay_up[l],
                   a0=row(iclr_a0[l]), a_up=iclr_up[l], g_up=g_up, k_k=row(key_kk[l]),
                   k_a=row(key_ka[l]), r_k=row(bonus_rk[l]), lnx_g=row(lnx_g[l]), lnx_b=row(lnx_b[l]))
        p_all = _inproj(x_p, x_s if l else pad_rows(x_s), n_p // tm_in if l else 0, n_p // tm_in, n_in,
                        row(norm_mix_g[l]), _relayout_w_in(w_in[l]), rope_p, rope_s, seq, tm_in)

        att_p = _attn_prompt(p_all, attn_sinks[l], batch, seq)
        ps = p_all[n_p:n].reshape(st, sb, N_COLS).transpose(1, 0, 2)
        k_new, v_new = ps[..., C_K:C_K + 256], ps[..., C_V:C_V + 256]
        k_cache = cache_k_win[l].reshape(sb, n_buf, 256)
        v_cache = cache_v_win[l].reshape(sb, n_buf, 256)
        att_s = _attn_sample(ps[..., C_Q:C_Q + D], k_new, v_new, k_cache, v_cache, attn_sinks[l])
        att_s = att_s.transpose(1, 0, 2).reshape(n_s, D)

        rw_p, s_pairs = _rwkv_prompt(p_all, prm, batch, seq)
        sh_rkv, sh_lora = _split_shift_cols(state_shift[l])
        r_s, w_s, k_s, v_s, a_s, b_s, g_s = _rwkv_sample_pre(p_all, n_p, sh_rkv, sh_lora, prm, sb, st)
        s0 = state_wkv[l].transpose(1, 2, 3, 0)
        y_t, s_fin = _rwkv_sample_scan(s0, r_s.T, w_s.T, k_s.T, v_s.T[:, None, :], a_s.T, b_s.T)
        rw_s = _rwkv_sample_post(y_t[:, 0, :].T, r_s, k_s, v_s, g_s, prm, sb)

        is_moe = l % 2 == 1
        last = l == depth - 1
        gf = row(norm_final_g) if last else None
        if is_moe:
            ne = moe_router.shape[-1]
            router = jnp.concatenate([moe_router[l // 2], jnp.zeros((D, LANES - ne), F32)], 1)
            x_mid, h2, route = _outproj(att_p, att_s, rw_p, rw_s, p_all, x_p, x_s, s_base, w_out[l].astype(BF16),
                                        row(norm_ffn_g[l]), router, ne, tm)
            o_exp = _moe_experts(h2, _moe_plan(route, ne), moe_w_gate[l // 2].astype(BF16),
                                 moe_w_up[l // 2].astype(BF16), moe_w_down[l // 2].astype(BF16),
                                 _pick(moe_w_gate.shape[-1], (896, 512, 256, 128)))
            x = _moe_combine(x_mid, o_exp, route, gf, n_p, tm)
        else:
            x_mid, h2 = _outproj(att_p, att_s, rw_p, rw_s, p_all, x_p, x_s, s_base, w_out[l].astype(BF16),
                                 row(norm_ffn_g[l]), None, 0, tm)
            x = _ffn(h2, ffn_w_gate[l // 2].astype(BF16), ffn_w_up[l // 2].astype(BF16),
                     ffn_w_down[l // 2].astype(BF16), x_mid, gf, n_p, tm,
                     _pick(ffn_w_gate.shape[-1], (1408, 512, 256, 128)))
        if not last:
            x_p, x_s, s_base = x, x, npt

        n_win = min(WINDOW, seq)
        tail = lambda rows, c0, w_: jnp.stack(
            [lax.slice(p_all, ((b + 1) * seq - rows, c0), ((b + 1) * seq, c0 + w_)) for b in range(batch)])
        k_p = tail(n_win, C_K, N_KV * HD).reshape(batch, n_win, N_KV, HD)
        v_p = tail(n_win, C_V, N_KV * HD).reshape(batch, n_win, N_KV, HD)
        unsplit = lambda a: jnp.concatenate(
            [a[..., C_RKV:C_RKV + 3 * D], a[..., C_LORA:C_LORA + LORA_W + LORA_A + LORA_G]], -1)
        last_p = jnp.concatenate([tail(1, C_RKV, 3 * D), tail(1, C_LORA, LORA_W + LORA_A + LORA_G)], -1)[:, 0]
        new_p.append((k_p, v_p, _pair_state_to_heads(s_pairs), last_p))
        k_s_win = jnp.concatenate([k_cache, k_new], 1)[:, -n_buf:].reshape(sb, n_buf, N_KV, HD)
        v_s_win = jnp.concatenate([v_cache, v_new], 1)[:, -n_buf:].reshape(sb, n_buf, N_KV, HD)
        new_s.append((k_s_win, v_s_win, s_fin.transpose(3, 0, 1, 2), unsplit(ps[:, -1])))

    y_p, y_s = x
    y_p = y_p.reshape(batch, seq, D)
    y_s = y_s.reshape(st, sb, D).transpose(1, 0, 2)
    stk = lambda sts, i: jnp.stack([s[i] for s in sts])
    return (y_p, y_s,
            stk(new_p, 0), stk(new_p, 1), stk(new_p, 2), stk(new_p, 3),
            stk(new_s, 0), stk(new_s, 1), stk(new_s, 2), stk(new_s, 3))
```

```python
import functools

import jax
import jax.numpy as jnp
from jax import lax
from jax.experimental import pallas as pl
from jax.experimental.pallas import tpu as pltpu

F32 = jnp.float32
BF16 = jnp.bfloat16
HIGHEST = lax.Precision.HIGHEST

LANES = 128
SUBLANES = 8
VMEM_LIMIT = 56 * 1024 * 1024

D = 1024
HD = 64
N_Q = 16
N_KV = 4
ROT = 16
ROPE_THETA = 500000.0
WINDOW = 128
RMS_EPS = 1e-5
LNX_EPS = 64e-5
N_PAIR = D // LANES
LORA_W, LORA_A, LORA_G = 64, 64, 160
LORA_PAD = 512
G_PAD = 256

C_RKV = 0
C_Q = 3072
C_GATE = 4096
C_K = 6144
C_V = 6400
C_LORA = 6656
N_COLS = 7168
TN_IN = 1024
Q_TILE = C_Q // TN_IN
KV_TILE = C_K // TN_IN

RWKV_CHUNK = 64


PAST_LEN = 16384


def _pick(n, cands):
    return next(c for c in cands if n % c == 0)


def _cparams(sem, **kw):
    return pltpu.CompilerParams(dimension_semantics=sem, vmem_limit_bytes=VMEM_LIMIT, **kw)


def _rms(x, g):
    return x * lax.rsqrt(jnp.mean(x * x, -1, keepdims=True) + RMS_EPS) * g


def _sigmoid(x):
    return 1.0 / (1.0 + jnp.exp(-x))


def _rope_chunk(a, c, sa, sb):
    return a * c + pltpu.roll(a, LANES - ROT // 2, 1) * sa + pltpu.roll(a, ROT // 2, 1) * sb


def _inproj_kernel(xp_ref, xs_ref, g_ref, w_ref, cp_ref, sap_ref, sbp_ref, cs_ref, sas_ref, sbs_ref,
                   o_ref, h_ref, *, n_prompt_tiles):
    i, j = pl.program_id(0), pl.program_id(1)
    is_prompt = i < n_prompt_tiles

    @pl.when(j == 0)
    def _():
        x = jnp.where(is_prompt, xp_ref[...], xs_ref[...])
        h_ref[...] = _rms(x, g_ref[...]).astype(BF16)

    acc = jnp.dot(h_ref[...], w_ref[...], preferred_element_type=F32)

    def roped(n_chunks):
        c = jnp.where(is_prompt, cp_ref[...], cs_ref[...])
        sa = jnp.where(is_prompt, sap_ref[...], sas_ref[...])
        sb = jnp.where(is_prompt, sbp_ref[...], sbs_ref[...])
        parts = [_rope_chunk(acc[:, k * LANES:(k + 1) * LANES], c, sa, sb) for k in range(n_chunks)]
        if n_chunks * LANES < TN_IN:
            parts.append(acc[:, n_chunks * LANES:])
        return jnp.concatenate(parts, axis=1)

    @pl.when(j == Q_TILE)
    def _():
        o_ref[...] = roped(TN_IN // LANES)

    @pl.when(j == KV_TILE)
    def _():
        o_ref[...] = roped(N_KV * HD // LANES)

    @pl.when((j != Q_TILE) & (j != KV_TILE))
    def _():
        o_ref[...] = acc


def _two_part_specs(block, npt, s_base, period=None):
    def p_map(i, *_):
        ip = jnp.minimum(i, npt - 1)
        return (ip % period if period else ip, 0)

    def s_map(i, *_):
        return (s_base + jnp.maximum(i - npt, 0), 0)

    return pl.BlockSpec(block, p_map), pl.BlockSpec(block, s_map)


def _inproj(x_p, x_s, s_base, npt, n, g, w, rope_p, rope_s, seq, tm):
    xp_spec, xs_spec = _two_part_specs((tm, D), npt, s_base)
    rp_spec, rs_spec = _two_part_specs((tm, LANES), npt, 0, period=seq // tm)
    return pl.pallas_call(
        functools.partial(_inproj_kernel, n_prompt_tiles=npt),
        out_shape=jax.ShapeDtypeStruct((n, N_COLS), F32),
        grid=(n // tm, N_COLS // TN_IN),
        in_specs=[xp_spec, xs_spec,
                  pl.BlockSpec((1, D), lambda i, j: (0, 0)),
                  pl.BlockSpec((D, TN_IN), lambda i, j: (0, j)),
                  rp_spec, rp_spec, rp_spec, rs_spec, rs_spec, rs_spec],
        out_specs=pl.BlockSpec((tm, TN_IN), lambda i, j: (i, j)),
        scratch_shapes=[pltpu.VMEM((tm, D), BF16)],
        compiler_params=_cparams(("parallel", "arbitrary")),
        name="inproj",
    )(x_p, x_s, g, w, *rope_p, *rope_s)


def _sink_softmax(s, mask, sink):
    s = jnp.where(mask, s * (HD ** -0.5), -jnp.inf)
    m = jnp.maximum(jnp.max(s, -1, keepdims=True), sink)
    p = jnp.exp(s - m)
    return p / (jnp.sum(p, -1, keepdims=True) + jnp.exp(sink - m))


def _dot_nt(a, b, **kw):
    return lax.dot_general(a, b, (((1,), (1,)), ((), ())), preferred_element_type=F32, **kw)


def _dot_tn(a, b, **kw):
    return lax.dot_general(a, b, (((0,), (0,)), ((), ())), preferred_element_type=F32, **kw)


def _head_operands(x, half):
    lo = lax.broadcasted_iota(jnp.int32, x.shape, 1) < HD
    sw = pltpu.roll(x, HD, 1)
    x_lo, x_hi = (x, sw) if half == 0 else (sw, x)
    return jnp.where(lo, x_lo, 0.0).astype(BF16), jnp.where(lo, 0.0, x_hi).astype(BF16)


def _group_queries(q, g):
    return jnp.concatenate([q[:, (2 * g) * LANES:(2 * g + 1) * LANES],
                            q[:, (2 * g + 1) * LANES:(2 * g + 2) * LANES]], axis=0)


def _group_sinks(sink_ref, g, top):
    return (jnp.where(top, sink_ref[4 * g], sink_ref[4 * g + 2]),
            jnp.where(top, sink_ref[4 * g + 1], sink_ref[4 * g + 3]))


def _attn_prompt_kernel(sink_ref, q_ref, kp_ref, kc_ref, vp_ref, vc_ref, o_ref):
    blk = pl.program_id(1)
    w = WINDOW
    q = q_ref[...].astype(BF16)
    k = jnp.concatenate([kp_ref[...], kc_ref[...]], axis=0)
    v = jnp.concatenate([vp_ref[...], vc_ref[...]], axis=0)
    qi = lax.broadcasted_iota(jnp.int32, (w, w), 0)
    kj = lax.broadcasted_iota(jnp.int32, (w, w), 1)
    band = jnp.concatenate([kj <= qi] * 2, axis=0)
    mask = band | (blk > 0)
    top = lax.broadcasted_iota(jnp.int32, (2 * w, 1), 0) < w
    outs = []
    for g in range(N_KV):
        ch, half = divmod(g, 2)
        q2 = _group_queries(q, g)
        o = 0.0
        for k_, v_, sink in zip(_head_operands(k[:, ch * LANES:(ch + 1) * LANES], half),
                                _head_operands(v[:, ch * LANES:(ch + 1) * LANES], half),
                                _group_sinks(sink_ref, g, top)):
            s = _dot_nt(q2, k_)
            p = _sink_softmax(jnp.where(band, s[:, w:], s[:, :w]), mask, sink)
            p_cat = jnp.concatenate([jnp.where(band, 0.0, p), jnp.where(band, p, 0.0)], axis=1)
            o = o + jnp.dot(p_cat.astype(BF16), v_, preferred_element_type=F32)
        outs += [o[:w], o[w:]]
    o_ref[...] = jnp.concatenate(outs, axis=1).astype(o_ref.dtype)


def _attn_prompt(p_all, sinks, batch, seq):
    nb = seq // WINDOW
    kcol, vcol = C_K // (N_KV * HD), C_V // (N_KV * HD)
    cur = lambda b, i: b * nb + i
    prev = lambda b, i: b * nb + jnp.maximum(i - 1, 0)
    return pl.pallas_call(
        _attn_prompt_kernel,
        out_shape=jax.ShapeDtypeStruct((batch * seq, D), BF16),
        grid=(batch, nb),
        in_specs=[
            pl.BlockSpec(memory_space=pltpu.SMEM),
            pl.BlockSpec((WINDOW, D), lambda b, i: (cur(b, i), C_Q // D)),
            pl.BlockSpec((WINDOW, N_KV * HD), lambda b, i: (prev(b, i), kcol)),
            pl.BlockSpec((WINDOW, N_KV * HD), lambda b, i: (cur(b, i), kcol)),
            pl.BlockSpec((WINDOW, N_KV * HD), lambda b, i: (prev(b, i), vcol)),
            pl.BlockSpec((WINDOW, N_KV * HD), lambda b, i: (cur(b, i), vcol)),
        ],
        out_specs=pl.BlockSpec((WINDOW, D), lambda b, i: (cur(b, i), 0)),
        compiler_params=_cparams(("parallel", "arbitrary")),
        name="attn_prompt",
    )(sinks, p_all, p_all, p_all, p_all, p_all)


ATT_S_BT = 8
T_PAD = 8


def _attn_sample_kernel(sink_ref, q_ref, kn_ref, vn_ref, kc_ref, vc_ref, o_ref):
    tn = q_ref.shape[1]
    nbuf = kc_ref.shape[1]
    rows = 2 * tn
    keys = nbuf + T_PAD
    r = lax.broadcasted_iota(jnp.int32, (rows, keys), 0)
    t = jnp.where(r >= tn, r - tn, r)
    kj = lax.broadcasted_iota(jnp.int32, (rows, keys), 1)
    mask = (kj > t + (nbuf - WINDOW)) & (kj <= t + nbuf)
    top = lax.broadcasted_iota(jnp.int32, (rows, 1), 0) < tn
    zpad = jnp.zeros((T_PAD - tn, N_KV * HD), F32)
    scores, sinks, values = [], [], []
    for b in range(ATT_S_BT):
        q = q_ref[b].astype(BF16)
        k = jnp.concatenate([kc_ref[b], kn_ref[b], zpad], axis=0)
        v = jnp.concatenate([vc_ref[b], vn_ref[b], zpad], axis=0)
        for g in range(N_KV):
            ch, half = divmod(g, 2)
            q2 = _group_queries(q, g)
            values += list(_head_operands(v[:, ch * LANES:(ch + 1) * LANES], half))
            sinks += list(_group_sinks(sink_ref, g, top))
            scores += [_dot_nt(q2, k_) for k_ in _head_operands(k[:, ch * LANES:(ch + 1) * LANES], half)]
    p = _sink_softmax(jnp.stack(scores), mask[None], jnp.stack(sinks)).astype(BF16)
    for b in range(ATT_S_BT):
        outs = []
        for g in range(N_KV):
            i = 2 * (b * N_KV + g)
            o = (jnp.dot(p[i], values[i], preferred_element_type=F32)
                 + jnp.dot(p[i + 1], values[i + 1], preferred_element_type=F32))
            outs += [o[:tn], o[tn:]]
        o_ref[b] = jnp.concatenate(outs, axis=1).astype(o_ref.dtype)


def _attn_sample(q, k_new, v_new, k_cache, v_cache, sinks):
    bsz, tn, _ = q.shape
    nbuf = k_cache.shape[1]
    kvw = N_KV * HD
    blk = lambda w_, r_: pl.BlockSpec((ATT_S_BT, r_, w_), lambda i: (i, 0, 0))
    return pl.pallas_call(
        _attn_sample_kernel,
        out_shape=jax.ShapeDtypeStruct((bsz, tn, D), BF16),
        grid=(bsz // ATT_S_BT,),
        in_specs=[pl.BlockSpec(memory_space=pltpu.SMEM),
                  blk(D, tn), blk(kvw, tn), blk(kvw, tn), blk(kvw, nbuf), blk(kvw, nbuf)],
        out_specs=blk(D, tn),
        compiler_params=_cparams(("parallel",)),
        name="attn_sample",
    )(sinks, q, k_new, v_new, k_cache, v_cache)


def _to_pm(x):
    return jnp.concatenate([x[:, p * LANES:(p + 1) * LANES] for p in range(N_PAIR)], axis=0)


def _from_pm(x):
    r = x.shape[0] // N_PAIR
    return jnp.concatenate([x[p * r:(p + 1) * r] for p in range(N_PAIR)], axis=1)


def _param_pm(v, r):
    return jnp.concatenate(
        [jnp.broadcast_to(v[:, p * LANES:(p + 1) * LANES], (r, LANES)) for p in range(N_PAIR)], axis=0)


def _head_sum(x, ones_bd):
    hi = x.astype(BF16)
    lo = (x - hi.astype(F32)).astype(BF16)
    return (jnp.dot(hi, ones_bd, preferred_element_type=F32)
            + jnp.dot(lo, ones_bd, preferred_element_type=F32))


def _ones_bd():
    r = lax.broadcasted_iota(jnp.int32, (LANES, LANES), 0) // HD
    c = lax.broadcasted_iota(jnp.int32, (LANES, LANES), 1) // HD
    return jnp.where(r == c, 1.0, 0.0).astype(BF16)


def _softplus(z):
    return jnp.maximum(z, 0.0) + jnp.log(1.0 + jnp.exp(-jnp.abs(z)))


def _rwkv_pre(p_rkv, p_lora, prev_rkv, prev_lora, prm):
    rows = p_rkv.shape[0]
    xs = p_rkv + (prev_rkv - p_rkv) * prm["mu_rkv"]
    xl = p_lora + (prev_lora - p_lora) * prm["mu_lora"]
    wd = xl[:, 0:LORA_W]
    ad = xl[:, LORA_W:LORA_W + LORA_A]
    gd = xl[:, LANES:LANES + G_PAD]
    mm = (((1,), (0,)), ((), ()))
    w_pre = prm["w0"] + _pdot(jnp.tanh(wd), prm["w_up"], mm, _PASSES["lora_w"])
    a_pre = prm["a0"] + _pdot(ad, prm["a_up"], mm, _PASSES["lora_a"])
    g = _pdot(_sigmoid(gd), prm["g_up"], mm, _PASSES["lora_g"])
    logw = -jnp.exp(-_softplus(-w_pre) - 0.5)
    a = _to_pm(_sigmoid(a_pre))
    r = _to_pm(xs[:, 0:D])
    k = _to_pm(xs[:, D:2 * D])
    v = _to_pm(xs[:, 2 * D:3 * D])
    kk = k * _param_pm(prm["k_k"], rows)
    nrm = jnp.sqrt(_head_sum(kk * kk, _ones_bd()))
    kk = kk / jnp.maximum(nrm, 1e-12)
    k = k * (1.0 + (a - 1.0) * _param_pm(prm["k_a"], rows))
    return r, k, v, -kk, kk * a, _to_pm(logw), _to_pm(g)


def _rwkv_post(y, r, k, v, g, prm):
    rows = y.shape[0] // N_PAIR
    ones_bd = _ones_bd()
    mean = _head_sum(y, ones_bd) * (1.0 / HD)
    yc = y - mean
    var = _head_sum(yc * yc, ones_bd) * (1.0 / HD)
    yn = yc * lax.rsqrt(var + LNX_EPS) * _param_pm(prm["lnx_g"], rows) + _param_pm(prm["lnx_b"], rows)
    bonus = _head_sum(r * k * _param_pm(prm["r_k"], rows), ones_bd) * v
    return (yn + bonus) * g


_PRM_NAMES = ("mu_rkv", "mu_lora", "w0", "w_up", "a0", "a_up", "g_up", "k_k", "k_a", "r_k", "lnx_g", "lnx_b")


def _prm_specs(prm, n_grid):
    zero = lambda *_: (0, 0)
    return [pl.BlockSpec(prm[n].shape, zero) for n in _PRM_NAMES]


def _stack2(x, lo):
    return jnp.concatenate([jnp.where(lo, x, 0.0), jnp.where(lo, 0.0, x)], axis=1)


def _split_bf16(x):
    hi = x.astype(BF16)
    return hi, (x - hi.astype(F32)).astype(BF16)


def _pdot(a, b, dims, passes):
    if passes == 6:
        return lax.dot_general(a, b, dims, preferred_element_type=F32, precision=HIGHEST)
    dot = lambda x, y: lax.dot_general(x, y, dims, preferred_element_type=F32)
    if passes == 1:
        return dot(a.astype(BF16), b.astype(BF16))
    a_hi, a_lo = _split_bf16(a)
    b_hi, b_lo = _split_bf16(b)
    return dot(a_hi, b_hi) + (dot(a_hi, b_lo) + dot(a_lo, b_hi))


_PASSES = dict(lora_w=1, lora_a=1, lora_g=1, gram=1, inv=1, rhs=1, u=1, y=1, state=1)


def _bdot(a, b, ca, cb, site):
    return _pdot(a, b, (((ca,), (cb,)), ((0,), (0,))), _PASSES[site])


def _bmm(a, b, site):
    return _bdot(a, b, 2, 1, site)


def _bmm_nt(a, b, site):
    return _bdot(a, b, 2, 2, site)


def _bmm_tn(a, b, site):
    return _bdot(a, b, 1, 1, site)


def _rwkv_chunk(s, r, k, v, a_, b_, logw):
    c = r.shape[1]
    t_idx = lax.broadcasted_iota(jnp.int32, logw.shape, 1)
    cum = logw
    d = 1
    while d < c:
        cum = cum + jnp.where(t_idx >= d, pltpu.roll(cum, d, 1), 0.0)
        d *= 2
    e_neg = jnp.exp(-cum)
    l_end = cum[:, c - 1:c, :]
    e_end = jnp.exp(l_end)
    assert c == HD, "the lane split of (C, 2C) time matrices reuses the head mask"
    lo = lax.broadcasted_iota(jnp.int32, (N_PAIR, c, LANES), 2) < HD
    at = a_ * jnp.exp(cum - logw)
    rt = r * jnp.exp(cum)
    yb, yk = _stack2(b_ * e_neg, lo), _stack2(k * e_neg, lo)
    vs = _stack2(v, lo)
    gmat = _bmm_nt(jnp.concatenate([at, rt], axis=1), jnp.concatenate([yb, yk], axis=1), "gram")
    n2 = 2 * c
    tr = lax.broadcasted_iota(jnp.int32, (N_PAIR, c, n2), 1)
    tc = lax.broadcasted_iota(jnp.int32, (N_PAIR, c, n2), 2) & (c - 1)
    strict, incl = tr > tc, tr >= tc
    a_ab = jnp.where(strict, gmat[:, :c, :n2], 0.0)
    a_ak = jnp.where(strict, gmat[:, :c, n2:], 0.0)
    a_rb = jnp.where(incl, gmat[:, c:, :n2], 0.0)
    a_rk = jnp.where(incl, gmat[:, c:, n2:], 0.0)
    tinv = jnp.where(tr == tc, 1.0, 0.0) + a_ab
    pw = _bmm(a_ab, _stack2(a_ab, lo), "inv")
    d = 4
    while d < c:
        res = _bmm(jnp.concatenate([tinv, pw], axis=1), _stack2(pw, lo), "inv")
        tinv, pw = tinv + res[:, :c], res[:, c:]
        d *= 2
    tinv = tinv + _bmm(tinv, _stack2(pw, lo), "inv")
    rhs = _bmm_nt(at, s, "rhs") + _bmm(a_ak, vs, "rhs")
    us = _stack2(_bmm(tinv, _stack2(rhs, lo), "u"), lo)
    uv = jnp.concatenate([us, vs], axis=1)
    y = _bmm_nt(rt, s, "y") + _bmm(jnp.concatenate([a_rb, a_rk], axis=2), uv, "y")
    s_new = s * e_end + _bmm_tn(uv, jnp.concatenate([yb * e_end, yk * e_end], axis=1), "state")
    return s_new, y


def _rwkv_prompt_kernel(prkv_ref, plora_ref, *rest):
    prm_refs = rest[:len(_PRM_NAMES)]
    o_ref, s_out_ref, s_ref, carry_rkv, carry_lora = rest[len(_PRM_NAMES):]
    ci = pl.program_id(1)
    c = prkv_ref.shape[0]

    @pl.when(ci == 0)
    def _():
        s_ref[...] = jnp.zeros_like(s_ref)
        carry_rkv[...] = jnp.zeros_like(carry_rkv)
        carry_lora[...] = jnp.zeros_like(carry_lora)

    prm = {n: ref[...] for n, ref in zip(_PRM_NAMES, prm_refs)}
    p_rkv, p_lora = prkv_ref[...], plora_ref[...]

    def shifted(x, carry_ref):
        first = lax.broadcasted_iota(jnp.int32, x.shape, 0) == 0
        return jnp.where(first, carry_ref[0:1, :], pltpu.roll(x, 1, 0))

    prev_rkv = shifted(p_rkv, carry_rkv)
    prev_lora = shifted(p_lora, carry_lora)
    carry_rkv[0:1, :] = p_rkv[c - 1:c, :]
    carry_lora[0:1, :] = p_lora[c - 1:c, :]

    r, k, v, a_, b_, logw, g = _rwkv_pre(p_rkv, p_lora, prev_rkv, prev_lora, prm)
    sh = (N_PAIR, c, LANES)
    s_new, y = _rwkv_chunk(s_ref[...], r.reshape(sh), k.reshape(sh), v.reshape(sh),
                           a_.reshape(sh), b_.reshape(sh), logw.reshape(sh))
    s_ref[...] = s_new
    out = _rwkv_post(y.reshape(N_PAIR * c, LANES), r, k, v, g, prm)
    o_ref[...] = _from_pm(out).astype(o_ref.dtype)

    @pl.when(ci == pl.num_programs(1) - 1)
    def _():
        s_out_ref[0] = s_new


def _rwkv_prompt(p_all, prm, batch, seq):
    c = RWKV_CHUNK
    nc = seq // c
    row = lambda b, i: b * nc + i
    return pl.pallas_call(
        _rwkv_prompt_kernel,
        out_shape=(jax.ShapeDtypeStruct((batch * seq, D), BF16),
                   jax.ShapeDtypeStruct((batch, N_PAIR, LANES, LANES), F32)),
        grid=(batch, nc),
        in_specs=[pl.BlockSpec((c, 3 * D), lambda b, i: (row(b, i), C_RKV // (3 * D))),
                  pl.BlockSpec((c, LORA_PAD), lambda b, i: (row(b, i), C_LORA // LORA_PAD))]
                 + _prm_specs(prm, 2),
        out_specs=(pl.BlockSpec((c, D), lambda b, i: (row(b, i), 0)),
                   pl.BlockSpec((1, N_PAIR, LANES, LANES), lambda b, i: (b, 0, 0, 0))),
        scratch_shapes=[pltpu.VMEM((N_PAIR, LANES, LANES), F32),
                        pltpu.VMEM((8, 3 * D), F32),
                        pltpu.VMEM((8, LORA_PAD), F32)],
        compiler_params=_cparams(("parallel", "arbitrary")),
        name="rwkv_prompt",
    )(p_all, p_all, *[prm[n] for n in _PRM_NAMES])


def _rwkv_sample_pre_kernel(prkv_ref, plora_ref, qrkv_ref, qlora_ref, srkv_ref, slora_ref, *rest):
    prm_refs = rest[:len(_PRM_NAMES)]
    outs = rest[len(_PRM_NAMES):]
    t = pl.program_id(0)
    prm = {n: ref[...] for n, ref in zip(_PRM_NAMES, prm_refs)}
    first = t == 0
    prev_rkv = jnp.where(first, srkv_ref[...], qrkv_ref[...])
    prev_lora = jnp.where(first, slora_ref[...], qlora_ref[...])
    r, k, v, a_, b_, logw, g = _rwkv_pre(prkv_ref[...], plora_ref[...], prev_rkv, prev_lora, prm)
    for ref, val in zip(outs, (r, jnp.exp(logw), k, v, a_, b_, g)):
        ref[...] = _from_pm(val)


def _rwkv_sample_pre(p_all, row0, shift_rkv, shift_lora, prm, bsz, tn):
    base = row0 // bsz
    cur = lambda t: base + t
    prv = lambda t: base + jnp.maximum(t - 1, 0)
    out = jax.ShapeDtypeStruct((tn * bsz, D), F32)
    return pl.pallas_call(
        _rwkv_sample_pre_kernel,
        out_shape=(out,) * 7,
        grid=(tn,),
        in_specs=[pl.BlockSpec((bsz, 3 * D), lambda t: (cur(t), C_RKV // (3 * D))),
                  pl.BlockSpec((bsz, LORA_PAD), lambda t: (cur(t), C_LORA // LORA_PAD)),
                  pl.BlockSpec((bsz, 3 * D), lambda t: (prv(t), C_RKV // (3 * D))),
                  pl.BlockSpec((bsz, LORA_PAD), lambda t: (prv(t), C_LORA // LORA_PAD)),
                  pl.BlockSpec((bsz, 3 * D), lambda t: (0, 0)),
                  pl.BlockSpec((bsz, LORA_PAD), lambda t: (0, 0))]
                 + _prm_specs(prm, 1),
        out_specs=tuple(pl.BlockSpec((bsz, D), lambda t: (t, 0)) for _ in range(7)),
        compiler_params=_cparams(("arbitrary",)),
        name="rwkv_sample_pre",
    )(p_all, p_all, p_all, p_all, shift_rkv, shift_lora, *[prm[n] for n in _PRM_NAMES])


def _rwkv_sample_scan_kernel(s0_ref, r_ref, w_ref, k_ref, v_ref, a_ref, b_ref, y_ref, so_ref, s_ref):
    bsz = s0_ref.shape[-1]
    tn = r_ref.shape[1] // bsz
    s_ref[...] = s0_ref[0]
    for t in range(tn):
        cols = slice(t * bsz, (t + 1) * bsz)
        r_t, w_t, k_t = r_ref[:, cols], w_ref[:, cols], k_ref[:, cols]
        a_t, b_t = a_ref[:, cols], b_ref[:, cols]

        def body(i, _):
            s_i = s_ref[i]
            sa = jnp.sum(s_i * a_t, axis=0, keepdims=True)
            v_i = v_ref[i, :, cols]
            s_i = s_i * w_t + sa * b_t + v_i * k_t
            s_ref[i] = s_i
            y_ref[i, :, cols] = jnp.sum(s_i * r_t, axis=0, keepdims=True)
            return 0

        lax.fori_loop(0, HD, body, 0)
    so_ref[0] = s_ref[...]


def _rwkv_sample_scan(s0, r, w, k, v, a_, b_):
    nh, _, _, bsz = s0.shape
    tb = r.shape[1]
    vec = pl.BlockSpec((HD, tb), lambda h: (h, 0))
    vec3 = pl.BlockSpec((HD, 1, tb), lambda h: (h, 0, 0))
    st = pl.BlockSpec((1, HD, HD, bsz), lambda h: (h, 0, 0, 0))
    return pl.pallas_call(
        _rwkv_sample_scan_kernel,
        out_shape=(jax.ShapeDtypeStruct((nh * HD, 1, tb), F32), jax.ShapeDtypeStruct(s0.shape, F32)),
        grid=(nh,),
        in_specs=[st, vec, vec, vec, vec3, vec, vec],
        out_specs=(vec3, st),
        scratch_shapes=[pltpu.VMEM((HD, HD, bsz), F32)],
        compiler_params=_cparams(("parallel",)),
        name="rwkv_sample_scan",
    )(s0, r, w, k, v, a_, b_)


def _rwkv_sample_post_kernel(y_ref, r_ref, k_ref, v_ref, g_ref, *rest):
    prm_refs = rest[:len(_PRM_NAMES)]
    o_ref = rest[len(_PRM_NAMES)]
    prm = {n: ref[...] for n, ref in zip(_PRM_NAMES, prm_refs)}
    out = _rwkv_post(_to_pm(y_ref[...]), _to_pm(r_ref[...]), _to_pm(k_ref[...]),
                     _to_pm(v_ref[...]), _to_pm(g_ref[...]), prm)
    o_ref[...] = _from_pm(out).astype(o_ref.dtype)


def _rwkv_sample_post(y, r, k, v, g, prm, bsz):
    n = y.shape[0]
    blk = pl.BlockSpec((bsz, D), lambda t: (t, 0))
    return pl.pallas_call(
        _rwkv_sample_post_kernel,
        out_shape=jax.ShapeDtypeStruct((n, D), BF16),
        grid=(n // bsz,),
        in_specs=[blk] * 5 + _prm_specs(prm, 1),
        out_specs=blk,
        compiler_params=_cparams(("parallel",)),
        name="rwkv_sample_post",
    )(y, r, k, v, g, *[prm[n] for n in _PRM_NAMES])


def _outproj_kernel(attp_ref, atts_ref, rwp_ref, rws_ref, ga_ref, gr_ref, xp_ref, xs_ref, w_ref, g2_ref, *rest,
                    n_experts, n_prompt_tiles):
    is_prompt = pl.program_id(0) < n_prompt_tiles
    att = jnp.where(is_prompt, attp_ref[...], atts_ref[...]).astype(F32)
    rw = jnp.where(is_prompt, rwp_ref[...], rws_ref[...]).astype(F32)
    x = jnp.where(is_prompt, xp_ref[...], xs_ref[...])
    m = _sigmoid(ga_ref[...]) * att + _sigmoid(gr_ref[...]) * rw
    xn = x + jnp.dot(m.astype(BF16), w_ref[...], preferred_element_type=F32)
    h2 = _rms(xn, g2_ref[...])
    if n_experts:
        router_ref, xo_ref, h2_ref, gate_ref = rest
        logits = jnp.dot(h2, router_ref[...], preferred_element_type=F32, precision=HIGHEST)
        lane = lax.broadcasted_iota(jnp.int32, logits.shape, 1).astype(F32)
        lg = jnp.where(lane < n_experts, logits, -jnp.inf)
        v1 = jnp.max(lg, -1, keepdims=True)
        i1 = jnp.min(jnp.where(lg == v1, lane, float(LANES)), -1, keepdims=True)
        lg2 = jnp.where(lane == i1, -jnp.inf, lg)
        v2 = jnp.max(lg2, -1, keepdims=True)
        i2 = jnp.min(jnp.where(lg2 == v2, lane, float(LANES)), -1, keepdims=True)
        e2 = jnp.exp(v2 - v1)
        den = 1.0 + e2
        gate_ref[...] = (jnp.where(lane == 0.0, i1, 0.0) + jnp.where(lane == 1.0, i2, 0.0)
                         + jnp.where(lane == 2.0, 1.0 / den, 0.0) + jnp.where(lane == 3.0, e2 / den, 0.0))
        h2_ref[...] = h2
    else:
        xo_ref, h2_ref = rest
        h2_ref[...] = h2.astype(BF16)
    xo_ref[...] = xn


def _outproj(att_p, att_s, rw_p, rw_s, p_all, x_p, x_s, s_base, w_out, g2, router, n_experts, tm):
    n = att_p.shape[0] + att_s.shape[0]
    npt = att_p.shape[0] // tm
    row = lambda w_: pl.BlockSpec((tm, w_), lambda i: (i, 0))
    p_spec, s_spec = _two_part_specs((tm, D), npt, 0)
    xp_spec, xs_spec = _two_part_specs((tm, D), npt, s_base)
    in_specs = [p_spec, s_spec, p_spec, s_spec,
                pl.BlockSpec((tm, D), lambda i: (i, C_GATE // D)),
                pl.BlockSpec((tm, D), lambda i: (i, C_GATE // D + 1)),
                xp_spec, xs_spec,
                pl.BlockSpec((D, D), lambda i: (0, 0)),
                pl.BlockSpec((1, D), lambda i: (0, 0))]
    args = [att_p, att_s, rw_p, rw_s, p_all, p_all, x_p, x_s, w_out, g2]
    out_shape = [jax.ShapeDtypeStruct((n, D), F32), jax.ShapeDtypeStruct((n, D), F32 if n_experts else BF16)]
    out_specs = [row(D), row(D)]
    if n_experts:
        in_specs.append(pl.BlockSpec((D, LANES), lambda i: (0, 0)))
        args.append(router)
        out_shape.append(jax.ShapeDtypeStruct((n, LANES), F32))
        out_specs.append(row(LANES))
    return pl.pallas_call(
        functools.partial(_outproj_kernel, n_experts=n_experts, n_prompt_tiles=npt),
        out_shape=tuple(out_shape),
        grid=(n // tm,),
        in_specs=in_specs,
        out_specs=tuple(out_specs),
        compiler_params=_cparams(("parallel",)),
        name="outproj",
    )(*args)


def _swiglu_part(h, wg, wu, wd):
    a = jnp.dot(h, wg, preferred_element_type=F32)
    b = jnp.dot(h, wu, preferred_element_type=F32)
    t = (a * _sigmoid(a) * b).astype(BF16)
    return jnp.dot(t, wd, preferred_element_type=F32)


def _final_store(y, gf_ref, out_refs, n_prompt_tiles):
    if gf_ref is None:
        (o_ref,) = out_refs
        o_ref[...] = y
        return
    op_ref, os_ref = out_refs
    y = _rms(y, gf_ref[...])
    is_prompt = pl.program_id(0) < n_prompt_tiles

    @pl.when(is_prompt)
    def _():
        op_ref[...] = y

    @pl.when(jnp.logical_not(is_prompt))
    def _():
        os_ref[...] = y


def _final_out(n, n_p, tm, final):
    if not final:
        return jax.ShapeDtypeStruct((n, D), F32), pl.BlockSpec((tm, D), lambda i, *_: (i, 0))
    return ((jax.ShapeDtypeStruct((n_p, D), F32), jax.ShapeDtypeStruct((n - n_p, D), F32)),
            _two_part_specs((tm, D), n_p // tm, 0))


def _ffn_kernel(h_ref, wg_ref, wu_ref, wd_ref, x_ref, *rest, final, n_prompt_tiles):
    rest = list(rest)
    gf_ref = rest.pop(0) if final else None
    acc_ref = rest.pop()
    f = pl.program_id(1)

    @pl.when(f == 0)
    def _():
        acc_ref[...] = jnp.zeros_like(acc_ref)

    acc_ref[...] += _swiglu_part(h_ref[...], wg_ref[...], wu_ref[...], wd_ref[...])

    @pl.when(f == pl.num_programs(1) - 1)
    def _():
        _final_store(x_ref[...] + acc_ref[...], gf_ref, rest, n_prompt_tiles)


def _ffn(h, wg, wu, wd, x, gf, n_p, tm, tf):
    n = x.shape[0]
    fdim = wg.shape[1]
    final = gf is not None
    in_specs = [pl.BlockSpec((tm, D), lambda i, f: (i, 0)),
                pl.BlockSpec((D, tf), lambda i, f: (0, f)),
                pl.BlockSpec((D, tf), lambda i, f: (0, f)),
                pl.BlockSpec((tf, D), lambda i, f: (f, 0)),
                pl.BlockSpec((tm, D), lambda i, f: (i, 0))]
    args = [h, wg, wu, wd, x]
    if final:
        in_specs.append(pl.BlockSpec((1, D), lambda i, f: (0, 0)))
        args.append(gf)
    out_shape, out_specs = _final_out(n, n_p, tm, final)
    return pl.pallas_call(
        functools.partial(_ffn_kernel, final=final, n_prompt_tiles=n_p // tm),
        out_shape=out_shape,
        grid=(n // tm, fdim // tf),
        in_specs=in_specs,
        out_specs=out_specs,
        scratch_shapes=[pltpu.VMEM((tm, D), F32)],
        compiler_params=_cparams(("arbitrary" if final else "parallel", "arbitrary")),
        name="ffn",
    )(*args)


MOE_TM = 512


def _moe_plan(route, n_experts):
    n = route.shape[0]
    e_flat = route[:, :2].astype(jnp.int32).T.reshape(-1)
    n_asg = e_flat.shape[0]
    order = jnp.argsort(e_flat, stable=True).astype(jnp.int32)
    counts = jnp.sum((e_flat[:, None] == jnp.arange(n_experts)[None, :]).astype(jnp.int32), axis=0)
    first = jnp.cumsum(counts) - counts
    padded = (counts + MOE_TM - 1) // MOE_TM * MOE_TM
    ends = jnp.cumsum(padded)
    offs = ends - padded
    n_tiles = -(-n_asg // MOE_TM) + n_experts
    start = jnp.arange(n_tiles, dtype=jnp.int32) * MOE_TM
    n_active = ends[-1] // MOE_TM
    tile_e = jnp.sum((start[:, None] >= ends[None, :]).astype(jnp.int32), axis=1)
    last_e = jnp.sum(((n_active - 1) * MOE_TM >= ends).astype(jnp.int32))
    tile_e = jnp.minimum(tile_e, last_e)
    n_valid = jnp.clip(counts[tile_e] - (start - offs[tile_e]), 0, MOE_TM)
    n_valid = jnp.where(start < ends[-1], n_valid, 0)
    r_in_tile = jnp.arange(MOE_TM, dtype=jnp.int32)[None, :]
    src = (first[tile_e] + start - offs[tile_e])[:, None] + r_in_tile
    real = r_in_tile < n_valid[:, None]
    dst = jnp.where(real, order[jnp.where(real, src, 0)], 0).reshape(-1)
    tok = jnp.where(dst >= n, dst - n, dst)
    return (tile_e.astype(jnp.int32), n_valid.astype(jnp.int32), n_active.reshape(1).astype(jnp.int32),
            tok.astype(jnp.int32), dst.astype(jnp.int32))


def _moe_kernel(te_ref, nv_ref, na_ref, tok_ref, dst_ref, h_hbm, wg_ref, wu_ref, wd_ref, o_hbm,
                xbuf, xb, acc, gsem, ssem):
    del te_ref
    j, f = pl.program_id(0), pl.program_id(1)
    nf = pl.num_programs(1)
    n_active = na_ref[0]
    active = j < n_active
    slot = j % 2

    def gather_row(tile, slot_, r):
        tok = tok_ref[tile * MOE_TM + r]
        return pltpu.make_async_copy(h_hbm.at[pl.ds(tok, 1)], xbuf.at[slot_, pl.ds(r, 1)], gsem.at[slot_])

    def gather_start(tile, slot_):
        def body(r, _):
            gather_row(tile, slot_, r).start()
            return 0
        lax.fori_loop(0, MOE_TM, body, 0, unroll=SUBLANES)

    def gather_wait(slot_):
        pltpu.make_async_copy(h_hbm.at[pl.ds(0, MOE_TM)], xbuf.at[slot_], gsem.at[slot_]).wait()

    def scatter_row(tile, r):
        row = dst_ref[tile * MOE_TM + r]
        pltpu.make_async_copy(acc.at[pl.ds(r, 1)], o_hbm.at[pl.ds(row, 1)], ssem.at[0]).start()

    def scatter_start(tile):
        n_rows = nv_ref[tile]
        n_grp = n_rows // SUBLANES

        def group(gi, _):
            for u in range(SUBLANES):
                scatter_row(tile, gi * SUBLANES + u)
            return 0

        def single(r, _):
            scatter_row(tile, r)
            return 0

        lax.fori_loop(0, n_grp, group, 0)
        lax.fori_loop(n_grp * SUBLANES, n_rows, single, 0)

    def scatter_wait(tile):
        n_rows = nv_ref[tile]
        n_grp = pl.multiple_of(n_rows // SUBLANES * SUBLANES, SUBLANES)

        @pl.when(n_grp > 0)
        def _():
            pltpu.make_async_copy(acc.at[pl.ds(0, n_grp)], o_hbm.at[pl.ds(0, n_grp)], ssem.at[0]).wait()

        def body(r, _):
            pltpu.make_async_copy(acc.at[pl.ds(0, 1)], o_hbm.at[pl.ds(0, 1)], ssem.at[0]).wait()
            return 0

        lax.fori_loop(0, n_rows - n_grp, body, 0)

    @pl.when(active & (f == 0))
    def _():
        @pl.when(j == 0)
        def _():
            gather_start(0, 0)

        gather_wait(slot)

        @pl.when(j + 1 < n_active)
        def _():
            gather_start(j + 1, 1 - slot)

        xb[...] = xbuf[slot].astype(BF16)

    @pl.when(active)
    def _():
        part = _swiglu_part(xb[...], wg_ref[0], wu_ref[0], wd_ref[0])

        @pl.when(f == 0)
        def _():
            @pl.when(j > 0)
            def _():
                scatter_wait(j - 1)

            acc[...] = part

        @pl.when(f > 0)
        def _():
            acc[...] += part

        @pl.when(f == nf - 1)
        def _():
            scatter_start(j)

            @pl.when(j == n_active - 1)
            def _():
                scatter_wait(j)


def _moe_experts(h, plan, wg, wu, wd, tf):
    tile_e, n_valid, n_active, tok, dst = plan
    n = h.shape[0]
    fdim = wg.shape[2]
    n_tiles = tile_e.shape[0]
    nf = fdim // tf

    def wmap(j, f, te, nv, na, tok_, dst_):
        return te[j], jnp.where(j < na[0], f, nf - 1)

    grid_spec = pltpu.PrefetchScalarGridSpec(
        num_scalar_prefetch=5,
        grid=(n_tiles, nf),
        in_specs=[pl.BlockSpec(memory_space=pl.ANY),
                  pl.BlockSpec((1, D, tf), lambda *a: (wmap(*a)[0], 0, wmap(*a)[1])),
                  pl.BlockSpec((1, D, tf), lambda *a: (wmap(*a)[0], 0, wmap(*a)[1])),
                  pl.BlockSpec((1, tf, D), lambda *a: (wmap(*a)[0], wmap(*a)[1], 0))],
        out_specs=pl.BlockSpec(memory_space=pl.ANY),
        scratch_shapes=[pltpu.VMEM((2, MOE_TM, D), F32),
                        pltpu.VMEM((MOE_TM, D), BF16),
                        pltpu.VMEM((MOE_TM, D), F32),
                        pltpu.SemaphoreType.DMA((2,)),
                        pltpu.SemaphoreType.DMA((1,))],
    )
    return pl.pallas_call(
        _moe_kernel,
        out_shape=jax.ShapeDtypeStruct((2 * n, D), F32),
        grid_spec=grid_spec,
        compiler_params=_cparams(("arbitrary", "arbitrary"), disable_bounds_checks=True),
        name="moe_experts",
    )(tile_e, n_valid, n_active, tok, dst, h, wg, wu, wd)


def _moe_combine_kernel(x_ref, o1_ref, o2_ref, route_ref, *rest, final, n_prompt_tiles):
    rest = list(rest)
    gf_ref = rest.pop(0) if final else None
    route = route_ref[...]
    lane = lax.broadcasted_iota(jnp.int32, route.shape, 1)
    w1 = jnp.sum(jnp.where(lane == 2, route, 0.0), -1, keepdims=True)
    w2 = jnp.sum(jnp.where(lane == 3, route, 0.0), -1, keepdims=True)
    _final_store(x_ref[...] + (w1 * o1_ref[...] + w2 * o2_ref[...]), gf_ref, rest, n_prompt_tiles)


def _moe_combine(x, o, route, gf, n_p, tm):
    n = x.shape[0]
    nb = n // tm
    final = gf is not None
    in_specs = [pl.BlockSpec((tm, D), lambda i: (i, 0)),
                pl.BlockSpec((tm, D), lambda i: (i, 0)),
                pl.BlockSpec((tm, D), lambda i: (i + nb, 0)),
                pl.BlockSpec((tm, LANES), lambda i: (i, 0))]
    args = [x, o, o, route]
    if final:
        in_specs.append(pl.BlockSpec((1, D), lambda i: (0, 0)))
        args.append(gf)
    out_shape, out_specs = _final_out(n, n_p, tm, final)
    return pl.pallas_call(
        functools.partial(_moe_combine_kernel, final=final, n_prompt_tiles=n_p // tm),
        out_shape=out_shape,
        grid=(nb,),
        in_specs=in_specs,
        out_specs=out_specs,
        compiler_params=_cparams(("arbitrary",)),
        name="moe_combine",
    )(*args)


def _split_shift_cols(a):
    pad = jnp.zeros(a.shape[:-1] + (LORA_PAD - (LORA_W + LORA_A + LORA_G),), a.dtype)
    lead = a[..., 3 * D:3 * D + LORA_W + LORA_A]
    gd = a[..., 3 * D + LORA_W + LORA_A:]
    return a[..., :3 * D], jnp.concatenate([lead, gd, pad], -1)


def _relayout_w_in(w):
    q, k, v = w[:, 0:D], w[:, D:D + 256], w[:, D + 256:D + 512]
    pr = w[:, D + 512:D + 512 + 3360]
    gates = w[:, D + 512 + 3360:]
    rkv, lora = _split_shift_cols(pr)
    return jnp.concatenate([rkv, q, gates, k, v, lora], axis=1).astype(BF16)


def _rope_tables(pos):
    inv = ROPE_THETA ** (-jnp.arange(0, ROT, 2, dtype=F32) / ROT)
    ang = pos.astype(F32)[:, None] * inv[None, :]
    cos, sin = jnp.cos(ang), jnp.sin(ang)
    n = pos.shape[0]
    half = ROT // 2
    one = jnp.ones((n, HD - ROT), F32)
    zero = jnp.zeros((n, HD - half), F32)
    c = jnp.concatenate([cos, cos, one], 1)
    sa = jnp.concatenate([-sin, zero], 1)
    sb = jnp.concatenate([jnp.zeros((n, half), F32), sin, jnp.zeros((n, HD - ROT), F32)], 1)
    tile = lambda a: jnp.concatenate([a, a], 1)
    return tile(c), tile(sa), tile(sb)


def _pair_state_to_heads(s):
    even = s[:, :, :HD, :HD]
    odd = s[:, :, HD:, HD:]
    b = s.shape[0]
    return jnp.stack([even, odd], axis=2).reshape(b, 2 * N_PAIR, HD, HD)


def kernel(x_prompt, x_sample, cache_k_win, cache_v_win, state_wkv, state_shift, norm_mix_g, w_in, w_out, attn_sinks, shift_mu, decay_w0, decay_up, iclr_a0, iclr_up, gate_up, key_kk, key_ka, bonus_rk, lnx_g, lnx_b, norm_ffn_g, ffn_w_gate, ffn_w_up, ffn_w_down, moe_router, moe_w_gate, moe_w_up, moe_w_down, norm_final_g):
    batch, seq, _ = x_prompt.shape
    sb, st, _ = x_sample.shape
    depth = w_in.shape[0]
    n_p, n_s = batch * seq, sb * st
    n = n_p + n_s
    n_buf = cache_k_win.shape[2]
    if depth == 0:
        raise ValueError("depth must be positive")
    tm = next(c for c in (512, 256, 128) if n_p % c == 0 and n_s % c == 0 and seq % c == 0)
    npt = n_p // tm
    tm_in = 2 * tm if n_p % (2 * tm) == 0 and seq % (2 * tm) == 0 else tm
    n_in = -(-n // tm_in) * tm_in
    pad_rows = lambda a: jnp.pad(a, ((0, n_in - n_p - a.shape[0]), (0, 0)))

    x_p, x_s, s_base = x_prompt.reshape(n_p, D), pad_rows(x_sample.transpose(1, 0, 2).reshape(n_s, D)), 0
    rope_p = _rope_tables(jnp.arange(seq))
    rope_s = tuple(pad_rows(t) for t in _rope_tables(jnp.repeat(PAST_LEN + jnp.arange(st), sb)))
    row = lambda a: a.reshape(1, -1)

    new_p, new_s = [], []
    for l in range(depth):
        mu_rkv, mu_lora = _split_shift_cols(row(shift_mu[l]))
        g_up = jnp.concatenate([gate_up[l], jnp.zeros((G_PAD - LORA_G, D), F32)], 0)
        prm = dict(mu_rkv=mu_rkv, mu_lora=mu_lora, w0=row(decay_w0[l]), w_up=decay_up[l],
                   a0=row(iclr_a0[l]), a_up=iclr_up[l], g_up=g_up, k_k=row(key_kk[l]),
                   k_a=row(key_ka[l]), r_k=row(bonus_rk[l]), lnx_g=row(lnx_g[l]), lnx_b=row(lnx_b[l]))
        p_all = _inproj(x_p, x_s, n_p // tm_in if l else 0, n_p // tm_in, n_in,
                        row(norm_mix_g[l]), _relayout_w_in(w_in[l]), rope_p, rope_s, seq, tm_in)

        att_p = _attn_prompt(p_all, attn_sinks[l], batch, seq)
        ps = p_all[n_p:n].reshape(st, sb, N_COLS).transpose(1, 0, 2)
        k_new, v_new = ps[..., C_K:C_K + 256], ps[..., C_V:C_V + 256]
        k_cache = cache_k_win[l].reshape(sb, n_buf, 256)
        v_cache = cache_v_win[l].reshape(sb, n_buf, 256)
        att_s = _attn_sample(ps[..., C_Q:C_Q + D], k_new, v_new, k_cache, v_cache, attn_sinks[l])
        att_s = pad_rows(att_s.transpose(1, 0, 2).reshape(n_s, D))

        rw_p, s_pairs = _rwkv_prompt(p_all, prm, batch, seq)
        sh_rkv, sh_lora = _split_shift_cols(state_shift[l])
        r_s, w_s, k_s, v_s, a_s, b_s, g_s = _rwkv_sample_pre(p_all, n_p, sh_rkv, sh_lora, prm, sb, st)
        s0 = state_wkv[l].transpose(1, 2, 3, 0)
        y_t, s_fin = _rwkv_sample_scan(s0, r_s.T, w_s.T, k_s.T, v_s.T[:, None, :], a_s.T, b_s.T)
        rw_s = pad_rows(_rwkv_sample_post(y_t[:, 0, :].T, r_s, k_s, v_s, g_s, prm, sb))

        is_moe = l % 2 == 1
        last = l == depth - 1
        gf = row(norm_final_g) if last else None
        if is_moe:
            ne = moe_router.shape[-1]
            router = jnp.concatenate([moe_router[l // 2], jnp.zeros((D, LANES - ne), F32)], 1)
            x_mid, h2, route = _outproj(att_p, att_s, rw_p, rw_s, p_all, x_p, x_s, s_base, w_out[l].astype(BF16),
                                        row(norm_ffn_g[l]), router, ne, tm)
            o_exp = _moe_experts(h2, _moe_plan(route, ne), moe_w_gate[l // 2].astype(BF16),
                                 moe_w_up[l // 2].astype(BF16), moe_w_down[l // 2].astype(BF16),
                                 _pick(moe_w_gate.shape[-1], (896, 512, 256, 128)))
            x = _moe_combine(x_mid, o_exp, route, gf, n_p, tm)
        else:
            x_mid, h2 = _outproj(att_p, att_s, rw_p, rw_s, p_all, x_p, x_s, s_base, w_out[l].astype(BF16),
                                 row(norm_ffn_g[l]), None, 0, tm)
            x = _ffn(h2, ffn_w_gate[l // 2].astype(BF16), ffn_w_up[l // 2].astype(BF16),
                     ffn_w_down[l // 2].astype(BF16), x_mid, gf, n_p, tm,
                     _pick(ffn_w_gate.shape[-1], (1408, 512, 256, 128)))
        if not last:
            x_p, x_s, s_base = x, x, npt

        n_win = min(WINDOW, seq)
        tail = lambda rows, c0, w_: jnp.stack(
            [lax.slice(p_all, ((b + 1) * seq - rows, c0), ((b + 1) * seq, c0 + w_)) for b in range(batch)])
        k_p = tail(n_win, C_K, N_KV * HD).reshape(batch, n_win, N_KV, HD)
        v_p = tail(n_win, C_V, N_KV * HD).reshape(batch, n_win, N_KV, HD)
        unsplit = lambda a: jnp.concatenate(
            [a[..., C_RKV:C_RKV + 3 * D], a[..., C_LORA:C_LORA + LORA_W + LORA_A + LORA_G]], -1)
        last_p = jnp.concatenate([tail(1, C_RKV, 3 * D), tail(1, C_LORA, LORA_W + LORA_A + LORA_G)], -1)[:, 0]
        new_p.append((k_p, v_p, _pair_state_to_heads(s_pairs), last_p))
        k_s_win = jnp.concatenate([k_cache, k_new], 1)[:, -n_buf:].reshape(sb, n_buf, N_KV, HD)
        v_s_win = jnp.concatenate([v_cache, v_new], 1)[:, -n_buf:].reshape(sb, n_buf, N_KV, HD)
        new_s.append((k_s_win, v_s_win, s_fin.transpose(3, 0, 1, 2), unsplit(ps[:, -1])))

    y_p, y_s = x
    y_p = y_p.reshape(batch, seq, D)
    y_s = y_s[:n_s].reshape(st, sb, D).transpose(1, 0, 2)
    stk = lambda sts, i: jnp.stack([s[i] for s in sts])
    return (y_p, y_s,
            stk(new_p, 0), stk(new_p, 1), stk(new_p, 2), stk(new_p, 3),
            stk(new_s, 0), stk(new_s, 1), stk(new_s, 2), stk(new_s, 3))
```

```python
import functools

import jax
import jax.numpy as jnp
from jax import lax
from jax.experimental import pallas as pl
from jax.experimental.pallas import tpu as pltpu

F32 = jnp.float32
BF16 = jnp.bfloat16
HIGHEST = lax.Precision.HIGHEST

LANES = 128
SUBLANES = 8
VMEM_LIMIT = 56 * 1024 * 1024

D = 1024
HD = 64
N_Q = 16
N_KV = 4
ROT = 16
ROPE_THETA = 500000.0
WINDOW = 128
RMS_EPS = 1e-5
LNX_EPS = 64e-5
N_PAIR = D // LANES
LORA_W, LORA_A, LORA_G = 64, 64, 160
LORA_PAD = 512
G_PAD = 256

C_RKV = 0
C_Q = 3072
C_GATE = 4096
C_K = 6144
C_V = 6400
C_LORA = 6656
N_COLS = 7168
TN_IN = 1024
Q_TILE = C_Q // TN_IN
KV_TILE = C_K // TN_IN

RWKV_CHUNK = 64


PAST_LEN = 16384


def _pick(n, cands):
    return next(c for c in cands if n % c == 0)


def _cparams(sem, **kw):
    return pltpu.CompilerParams(dimension_semantics=sem, vmem_limit_bytes=VMEM_LIMIT, **kw)


def _rms(x, g):
    return x * lax.rsqrt(jnp.mean(x * x, -1, keepdims=True) + RMS_EPS) * g


def _sigmoid(x):
    return 1.0 / (1.0 + jnp.exp(-x))


def _rope_chunk(a, c, sa, sb):
    return a * c + pltpu.roll(a, LANES - ROT // 2, 1) * sa + pltpu.roll(a, ROT // 2, 1) * sb


def _inproj_kernel(xp_ref, xs_ref, g_ref, w_ref, cp_ref, sap_ref, sbp_ref, cs_ref, sas_ref, sbs_ref,
                   o_ref, h_ref, *, n_prompt_tiles):
    i, j = pl.program_id(0), pl.program_id(1)
    is_prompt = i < n_prompt_tiles

    @pl.when(j == 0)
    def _():
        x = jnp.where(is_prompt, xp_ref[...], xs_ref[...])
        h_ref[...] = _rms(x, g_ref[...]).astype(BF16)

    acc = jnp.dot(h_ref[...], w_ref[...], preferred_element_type=F32)

    def roped(n_chunks):
        c = jnp.where(is_prompt, cp_ref[...], cs_ref[...])
        sa = jnp.where(is_prompt, sap_ref[...], sas_ref[...])
        sb = jnp.where(is_prompt, sbp_ref[...], sbs_ref[...])
        parts = [_rope_chunk(acc[:, k * LANES:(k + 1) * LANES], c, sa, sb) for k in range(n_chunks)]
        if n_chunks * LANES < TN_IN:
            parts.append(acc[:, n_chunks * LANES:])
        return jnp.concatenate(parts, axis=1)

    @pl.when(j == Q_TILE)
    def _():
        o_ref[...] = roped(TN_IN // LANES)

    @pl.when(j == KV_TILE)
    def _():
        o_ref[...] = roped(N_KV * HD // LANES)

    @pl.when((j != Q_TILE) & (j != KV_TILE))
    def _():
        o_ref[...] = acc


def _two_part_specs(block, npt, s_base, period=None):
    def p_map(i, *_):
        ip = jnp.minimum(i, npt - 1)
        return (ip % period if period else ip, 0)

    def s_map(i, *_):
        return (s_base + jnp.maximum(i - npt, 0), 0)

    return pl.BlockSpec(block, p_map), pl.BlockSpec(block, s_map)


def _inproj(x_p, x_s, s_base, npt, n, g, w, rope_p, rope_s, seq, tm):
    xp_spec, xs_spec = _two_part_specs((tm, D), npt, s_base)
    rp_spec, rs_spec = _two_part_specs((tm, LANES), npt, 0, period=seq // tm)
    return pl.pallas_call(
        functools.partial(_inproj_kernel, n_prompt_tiles=npt),
        out_shape=jax.ShapeDtypeStruct((n, N_COLS), F32),
        grid=(n // tm, N_COLS // TN_IN),
        in_specs=[xp_spec, xs_spec,
                  pl.BlockSpec((1, D), lambda i, j: (0, 0)),
                  pl.BlockSpec((D, TN_IN), lambda i, j: (0, j)),
                  rp_spec, rp_spec, rp_spec, rs_spec, rs_spec, rs_spec],
        out_specs=pl.BlockSpec((tm, TN_IN), lambda i, j: (i, j)),
        scratch_shapes=[pltpu.VMEM((tm, D), BF16)],
        compiler_params=_cparams(("parallel", "arbitrary")),
        name="inproj",
    )(x_p, x_s, g, w, *rope_p, *rope_s)


def _sink_softmax(s, mask, sink):
    s = jnp.where(mask, s * (HD ** -0.5), -jnp.inf)
    m = jnp.maximum(jnp.max(s, -1, keepdims=True), sink)
    p = jnp.exp(s - m)
    return p / (jnp.sum(p, -1, keepdims=True) + jnp.exp(sink - m))


def _dot_nt(a, b, **kw):
    return lax.dot_general(a, b, (((1,), (1,)), ((), ())), preferred_element_type=F32, **kw)


def _dot_tn(a, b, **kw):
    return lax.dot_general(a, b, (((0,), (0,)), ((), ())), preferred_element_type=F32, **kw)


def _head_operands(x, half):
    lo = lax.broadcasted_iota(jnp.int32, x.shape, 1) < HD
    sw = pltpu.roll(x, HD, 1)
    x_lo, x_hi = (x, sw) if half == 0 else (sw, x)
    return jnp.where(lo, x_lo, 0.0).astype(BF16), jnp.where(lo, 0.0, x_hi).astype(BF16)


def _group_queries(q, g):
    return jnp.concatenate([q[:, (2 * g) * LANES:(2 * g + 1) * LANES],
                            q[:, (2 * g + 1) * LANES:(2 * g + 2) * LANES]], axis=0)


def _group_sinks(sink_ref, g, top):
    return (jnp.where(top, sink_ref[4 * g], sink_ref[4 * g + 2]),
            jnp.where(top, sink_ref[4 * g + 1], sink_ref[4 * g + 3]))


def _attn_prompt_kernel(sink_ref, q_ref, kp_ref, kc_ref, vp_ref, vc_ref, o_ref):
    blk = pl.program_id(1)
    w = WINDOW
    q = q_ref[...].astype(BF16)
    k = jnp.concatenate([kp_ref[...], kc_ref[...]], axis=0)
    v = jnp.concatenate([vp_ref[...], vc_ref[...]], axis=0)
    qi = lax.broadcasted_iota(jnp.int32, (w, w), 0)
    kj = lax.broadcasted_iota(jnp.int32, (w, w), 1)
    band = jnp.concatenate([kj <= qi] * 2, axis=0)
    mask = band | (blk > 0)
    top = lax.broadcasted_iota(jnp.int32, (2 * w, 1), 0) < w
    outs = []
    for g in range(N_KV):
        ch, half = divmod(g, 2)
        q2 = _group_queries(q, g)
        o = 0.0
        for k_, v_, sink in zip(_head_operands(k[:, ch * LANES:(ch + 1) * LANES], half),
                                _head_operands(v[:, ch * LANES:(ch + 1) * LANES], half),
                                _group_sinks(sink_ref, g, top)):
            s = _dot_nt(q2, k_)
            p = _sink_softmax(jnp.where(band, s[:, w:], s[:, :w]), mask, sink)
            p_cat = jnp.concatenate([jnp.where(band, 0.0, p), jnp.where(band, p, 0.0)], axis=1)
            o = o + jnp.dot(p_cat.astype(BF16), v_, preferred_element_type=F32)
        outs += [o[:w], o[w:]]
    o_ref[...] = jnp.concatenate(outs, axis=1).astype(o_ref.dtype)


def _attn_prompt(p_all, sinks, batch, seq):
    nb = seq // WINDOW
    kcol, vcol = C_K // (N_KV * HD), C_V // (N_KV * HD)
    cur = lambda b, i: b * nb + i
    prev = lambda b, i: b * nb + jnp.maximum(i - 1, 0)
    return pl.pallas_call(
        _attn_prompt_kernel,
        out_shape=jax.ShapeDtypeStruct((batch * seq, D), BF16),
        grid=(batch, nb),
        in_specs=[
            pl.BlockSpec(memory_space=pltpu.SMEM),
            pl.BlockSpec((WINDOW, D), lambda b, i: (cur(b, i), C_Q // D)),
            pl.BlockSpec((WINDOW, N_KV * HD), lambda b, i: (prev(b, i), kcol)),
            pl.BlockSpec((WINDOW, N_KV * HD), lambda b, i: (cur(b, i), kcol)),
            pl.BlockSpec((WINDOW, N_KV * HD), lambda b, i: (prev(b, i), vcol)),
            pl.BlockSpec((WINDOW, N_KV * HD), lambda b, i: (cur(b, i), vcol)),
        ],
        out_specs=pl.BlockSpec((WINDOW, D), lambda b, i: (cur(b, i), 0)),
        compiler_params=_cparams(("parallel", "arbitrary")),
        name="attn_prompt",
    )(sinks, p_all, p_all, p_all, p_all, p_all)


ATT_S_BT = 8
T_PAD = 8


def _attn_sample_kernel(sink_ref, q_ref, kn_ref, vn_ref, kc_ref, vc_ref, o_ref):
    tn = q_ref.shape[1]
    nbuf = kc_ref.shape[1]
    rows = 2 * tn
    keys = nbuf + T_PAD
    r = lax.broadcasted_iota(jnp.int32, (rows, keys), 0)
    t = jnp.where(r >= tn, r - tn, r)
    kj = lax.broadcasted_iota(jnp.int32, (rows, keys), 1)
    mask = (kj > t + (nbuf - WINDOW)) & (kj <= t + nbuf)
    top = lax.broadcasted_iota(jnp.int32, (rows, 1), 0) < tn
    zpad = jnp.zeros((T_PAD - tn, N_KV * HD), F32)
    scores, sinks, values = [], [], []
    for b in range(ATT_S_BT):
        q = q_ref[b].astype(BF16)
        k = jnp.concatenate([kc_ref[b], kn_ref[b], zpad], axis=0)
        v = jnp.concatenate([vc_ref[b], vn_ref[b], zpad], axis=0)
        for g in range(N_KV):
            ch, half = divmod(g, 2)
            q2 = _group_queries(q, g)
            values += list(_head_operands(v[:, ch * LANES:(ch + 1) * LANES], half))
            sinks += list(_group_sinks(sink_ref, g, top))
            scores += [_dot_nt(q2, k_) for k_ in _head_operands(k[:, ch * LANES:(ch + 1) * LANES], half)]
    p = _sink_softmax(jnp.stack(scores), mask[None], jnp.stack(sinks)).astype(BF16)
    for b in range(ATT_S_BT):
        outs = []
        for g in range(N_KV):
            i = 2 * (b * N_KV + g)
            o = (jnp.dot(p[i], values[i], preferred_element_type=F32)
                 + jnp.dot(p[i + 1], values[i + 1], preferred_element_type=F32))
            outs += [o[:tn], o[tn:]]
        o_ref[b] = jnp.concatenate(outs, axis=1).astype(o_ref.dtype)


def _attn_sample(q, k_new, v_new, k_cache, v_cache, sinks):
    bsz, tn, _ = q.shape
    nbuf = k_cache.shape[1]
    kvw = N_KV * HD
    blk = lambda w_, r_: pl.BlockSpec((ATT_S_BT, r_, w_), lambda i: (i, 0, 0))
    return pl.pallas_call(
        _attn_sample_kernel,
        out_shape=jax.ShapeDtypeStruct((bsz, tn, D), BF16),
        grid=(bsz // ATT_S_BT,),
        in_specs=[pl.BlockSpec(memory_space=pltpu.SMEM),
                  blk(D, tn), blk(kvw, tn), blk(kvw, tn), blk(kvw, nbuf), blk(kvw, nbuf)],
        out_specs=blk(D, tn),
        compiler_params=_cparams(("parallel",)),
        name="attn_sample",
    )(sinks, q, k_new, v_new, k_cache, v_cache)


def _to_pm(x):
    return jnp.concatenate([x[:, p * LANES:(p + 1) * LANES] for p in range(N_PAIR)], axis=0)


def _from_pm(x):
    r = x.shape[0] // N_PAIR
    return jnp.concatenate([x[p * r:(p + 1) * r] for p in range(N_PAIR)], axis=1)


def _param_pm(v, r):
    if v.shape == (N_PAIR * r, LANES):
        return v
    return jnp.concatenate(
        [jnp.broadcast_to(v[:, p * LANES:(p + 1) * LANES], (r, LANES)) for p in range(N_PAIR)], axis=0)


def _head_sum(x, ones_bd):
    hi = x.astype(BF16)
    out = jnp.dot(hi, ones_bd, preferred_element_type=F32)
    if _PASSES["head_sum"] == 2:
        lo = (x - hi.astype(F32)).astype(BF16)
        out = out + jnp.dot(lo, ones_bd, preferred_element_type=F32)
    return out


def _ones_bd():
    r = lax.broadcasted_iota(jnp.int32, (LANES, LANES), 0) // HD
    c = lax.broadcasted_iota(jnp.int32, (LANES, LANES), 1) // HD
    return jnp.where(r == c, 1.0, 0.0).astype(BF16)


def _softplus(z):
    return jnp.maximum(z, 0.0) + jnp.log(1.0 + jnp.exp(-jnp.abs(z)))


def _rwkv_pre(p_rkv, p_lora, prev_rkv, prev_lora, prm):
    rows = p_rkv.shape[0]
    xs = p_rkv + (prev_rkv - p_rkv) * prm["mu_rkv"]
    xl = p_lora + (prev_lora - p_lora) * prm["mu_lora"]
    wd = xl[:, 0:LORA_W]
    ad = xl[:, LORA_W:LORA_W + LORA_A]
    gd = xl[:, LANES:LANES + G_PAD]
    mm = (((1,), (0,)), ((), ()))
    w_pre = prm["w0"] + _pdot(jnp.tanh(wd), prm["w_up"], mm, _PASSES["lora_w"])
    a_pre = prm["a0"] + _pdot(ad, prm["a_up"], mm, _PASSES["lora_a"])
    g = _pdot(_sigmoid(gd), prm["g_up"], mm, _PASSES["lora_g"])
    logw = -jnp.exp(-_softplus(-w_pre) - 0.5)
    a = _to_pm(_sigmoid(a_pre))
    r = _to_pm(xs[:, 0:D])
    k = _to_pm(xs[:, D:2 * D])
    v = _to_pm(xs[:, 2 * D:3 * D])
    kk = k * _param_pm(prm["k_k"], rows)
    nrm = jnp.sqrt(_head_sum(kk * kk, _ones_bd()))
    kk = kk / jnp.maximum(nrm, 1e-12)
    k = k * (1.0 + (a - 1.0) * _param_pm(prm["k_a"], rows))
    return r, k, v, -kk, kk * a, _to_pm(logw), _to_pm(g)


def _rwkv_post(y, r, k, v, g, prm):
    rows = y.shape[0] // N_PAIR
    ones_bd = _ones_bd()
    mean = _head_sum(y, ones_bd) * (1.0 / HD)
    yc = y - mean
    var = _head_sum(yc * yc, ones_bd) * (1.0 / HD)
    yn = yc * lax.rsqrt(var + LNX_EPS) * _param_pm(prm["lnx_g"], rows) + _param_pm(prm["lnx_b"], rows)
    bonus = _head_sum(r * k * _param_pm(prm["r_k"], rows), ones_bd) * v
    return (yn + bonus) * g


_PRM_NAMES = ("mu_rkv", "mu_lora", "w0", "w_up", "a0", "a_up", "g_up", "k_k", "k_a", "r_k", "lnx_g", "lnx_b")


def _prm_specs(prm, n_grid):
    zero = lambda *_: (0, 0)
    return [pl.BlockSpec(prm[n].shape, zero) for n in _PRM_NAMES]


def _stack2(x, lo):
    return jnp.concatenate([jnp.where(lo, x, 0.0), jnp.where(lo, 0.0, x)], axis=1)


def _split_bf16(x):
    hi = x.astype(BF16)
    return hi, (x - hi.astype(F32)).astype(BF16)


def _pdot(a, b, dims, passes):
    if passes == 6:
        return lax.dot_general(a, b, dims, preferred_element_type=F32, precision=HIGHEST)
    dot = lambda x, y: lax.dot_general(x, y, dims, preferred_element_type=F32)
    if passes == 1:
        return dot(a.astype(BF16), b.astype(BF16))
    a_hi, a_lo = _split_bf16(a)
    b_hi, b_lo = _split_bf16(b)
    return dot(a_hi, b_hi) + (dot(a_hi, b_lo) + dot(a_lo, b_hi))


_PASSES = dict(lora_w=1, lora_a=1, lora_g=1, gram=1, inv=1, rhs=1, u=1, y=1, state=1, head_sum=1)


def _bdot(a, b, ca, cb, site):
    return _pdot(a, b, (((ca,), (cb,)), ((0,), (0,))), _PASSES[site])


def _bmm(a, b, site):
    return _bdot(a, b, 2, 1, site)


def _bmm_nt(a, b, site):
    return _bdot(a, b, 2, 2, site)


def _bmm_tn(a, b, site):
    return _bdot(a, b, 1, 1, site)


def _rwkv_chunk(s, r, k, v, a_, b_, logw):
    c = r.shape[1]
    ti = lax.broadcasted_iota(jnp.int32, (N_PAIR, c, c), 1)
    si = lax.broadcasted_iota(jnp.int32, (N_PAIR, c, c), 2)
    tri = jnp.where(si <= ti, 1.0, 0.0).astype(BF16)
    w_hi, w_lo = _split_bf16(logw)
    cdot = lambda x: lax.dot_general(tri, x, (((2,), (1,)), ((0,), (0,))), preferred_element_type=F32)
    cum = cdot(w_hi) + cdot(w_lo)
    e_neg = jnp.exp(-cum)
    l_end = cum[:, c - 1:c, :]
    e_end = jnp.exp(l_end)
    assert c == HD, "the lane split of (C, 2C) time matrices reuses the head mask"
    lo = lax.broadcasted_iota(jnp.int32, (N_PAIR, c, LANES), 2) < HD
    at = a_ * jnp.exp(cum - logw)
    rt = r * jnp.exp(cum)
    yb, yk = _stack2(b_ * e_neg, lo), _stack2(k * e_neg, lo)
    vs = _stack2(v, lo)
    gmat = _bmm_nt(jnp.concatenate([at, rt], axis=1), jnp.concatenate([yb, yk], axis=1), "gram")
    n2 = 2 * c
    tr = lax.broadcasted_iota(jnp.int32, (N_PAIR, c, n2), 1)
    tc = lax.broadcasted_iota(jnp.int32, (N_PAIR, c, n2), 2) & (c - 1)
    strict, incl = tr > tc, tr >= tc
    a_ab = jnp.where(strict, gmat[:, :c, :n2], 0.0)
    a_ak = jnp.where(strict, gmat[:, :c, n2:], 0.0)
    a_rb = jnp.where(incl, gmat[:, c:, :n2], 0.0)
    a_rk = jnp.where(incl, gmat[:, c:, n2:], 0.0)
    tinv = jnp.where(tr == tc, 1.0, 0.0) + a_ab
    pw = _bmm(a_ab, _stack2(a_ab, lo), "inv")
    d = 4
    while d < c:
        res = _bmm(jnp.concatenate([tinv, pw], axis=1), _stack2(pw, lo), "inv")
        tinv, pw = tinv + res[:, :c], res[:, c:]
        d *= 2
    tinv = tinv + _bmm(tinv, _stack2(pw, lo), "inv")
    rhs = _bmm_nt(at, s, "rhs") + _bmm(a_ak, vs, "rhs")
    us = _stack2(_bmm(tinv, _stack2(rhs, lo), "u"), lo)
    uv = jnp.concatenate([us, vs], axis=1)
    y = _bmm_nt(rt, s, "y") + _bmm(jnp.concatenate([a_rb, a_rk], axis=2), uv, "y")
    s_new = s * e_end + _bmm_tn(uv, jnp.concatenate([yb * e_end, yk * e_end], axis=1), "state")
    return s_new, y


def _rwkv_prompt_kernel(prkv_ref, plora_ref, *rest):
    prm_refs = rest[:len(_PRM_NAMES)]
    o_ref, s_out_ref, s_ref, carry_rkv, carry_lora = rest[len(_PRM_NAMES):]
    ci = pl.program_id(1)
    c = prkv_ref.shape[0]

    @pl.when(ci == 0)
    def _():
        s_ref[...] = jnp.zeros_like(s_ref)
        carry_rkv[...] = jnp.zeros_like(carry_rkv)
        carry_lora[...] = jnp.zeros_like(carry_lora)

    prm = {n: ref[...] for n, ref in zip(_PRM_NAMES, prm_refs)}
    p_rkv, p_lora = prkv_ref[...], plora_ref[...]

    def shifted(x, carry_ref):
        first = lax.broadcasted_iota(jnp.int32, x.shape, 0) == 0
        return jnp.where(first, carry_ref[0:1, :], pltpu.roll(x, 1, 0))

    prev_rkv = shifted(p_rkv, carry_rkv)
    prev_lora = shifted(p_lora, carry_lora)
    carry_rkv[0:1, :] = p_rkv[c - 1:c, :]
    carry_lora[0:1, :] = p_lora[c - 1:c, :]

    r, k, v, a_, b_, logw, g = _rwkv_pre(p_rkv, p_lora, prev_rkv, prev_lora, prm)
    sh = (N_PAIR, c, LANES)
    s_new, y = _rwkv_chunk(s_ref[...], r.reshape(sh), k.reshape(sh), v.reshape(sh),
                           a_.reshape(sh), b_.reshape(sh), logw.reshape(sh))
    s_ref[...] = s_new
    out = _rwkv_post(y.reshape(N_PAIR * c, LANES), r, k, v, g, prm)
    o_ref[...] = _from_pm(out).astype(o_ref.dtype)

    @pl.when(ci == pl.num_programs(1) - 1)
    def _():
        s_out_ref[0] = s_new


def _rwkv_prompt(p_all, prm, batch, seq):
    c = RWKV_CHUNK
    nc = seq // c
    row = lambda b, i: b * nc + i
    return pl.pallas_call(
        _rwkv_prompt_kernel,
        out_shape=(jax.ShapeDtypeStruct((batch * seq, D), BF16),
                   jax.ShapeDtypeStruct((batch, N_PAIR, LANES, LANES), F32)),
        grid=(batch, nc),
        in_specs=[pl.BlockSpec((c, 3 * D), lambda b, i: (row(b, i), C_RKV // (3 * D))),
                  pl.BlockSpec((c, LORA_PAD), lambda b, i: (row(b, i), C_LORA // LORA_PAD))]
                 + _prm_specs(prm, 2),
        out_specs=(pl.BlockSpec((c, D), lambda b, i: (row(b, i), 0)),
                   pl.BlockSpec((1, N_PAIR, LANES, LANES), lambda b, i: (b, 0, 0, 0))),
        scratch_shapes=[pltpu.VMEM((N_PAIR, LANES, LANES), F32),
                        pltpu.VMEM((8, 3 * D), F32),
                        pltpu.VMEM((8, LORA_PAD), F32)],
        compiler_params=_cparams(("parallel", "arbitrary")),
        name="rwkv_prompt",
    )(p_all, p_all, *[prm[n] for n in _PRM_NAMES])


def _rwkv_sample_pre_kernel(prkv_ref, plora_ref, qrkv_ref, qlora_ref, srkv_ref, slora_ref, *rest):
    prm_refs = rest[:len(_PRM_NAMES)]
    outs = rest[len(_PRM_NAMES):]
    t = pl.program_id(0)
    prm = {n: ref[...] for n, ref in zip(_PRM_NAMES, prm_refs)}
    first = t == 0
    prev_rkv = jnp.where(first, srkv_ref[...], qrkv_ref[...])
    prev_lora = jnp.where(first, slora_ref[...], qlora_ref[...])
    r, k, v, a_, b_, logw, g = _rwkv_pre(prkv_ref[...], plora_ref[...], prev_rkv, prev_lora, prm)
    for ref, val in zip(outs, (r, jnp.exp(logw), k, v, a_, b_, g)):
        ref[...] = _from_pm(val)


def _rwkv_sample_pre(p_all, row0, shift_rkv, shift_lora, prm, bsz, tn):
    base = row0 // bsz
    cur = lambda t: base + t
    prv = lambda t: base + jnp.maximum(t - 1, 0)
    out = jax.ShapeDtypeStruct((tn * bsz, D), F32)
    return pl.pallas_call(
        _rwkv_sample_pre_kernel,
        out_shape=(out,) * 7,
        grid=(tn,),
        in_specs=[pl.BlockSpec((bsz, 3 * D), lambda t: (cur(t), C_RKV // (3 * D))),
                  pl.BlockSpec((bsz, LORA_PAD), lambda t: (cur(t), C_LORA // LORA_PAD)),
                  pl.BlockSpec((bsz, 3 * D), lambda t: (prv(t), C_RKV // (3 * D))),
                  pl.BlockSpec((bsz, LORA_PAD), lambda t: (prv(t), C_LORA // LORA_PAD)),
                  pl.BlockSpec((bsz, 3 * D), lambda t: (0, 0)),
                  pl.BlockSpec((bsz, LORA_PAD), lambda t: (0, 0))]
                 + _prm_specs(prm, 1),
        out_specs=tuple(pl.BlockSpec((bsz, D), lambda t: (t, 0)) for _ in range(7)),
        compiler_params=_cparams(("arbitrary",)),
        name="rwkv_sample_pre",
    )(p_all, p_all, p_all, p_all, shift_rkv, shift_lora, *[prm[n] for n in _PRM_NAMES])


def _rwkv_sample_scan_kernel(s0_ref, r_ref, w_ref, k_ref, v_ref, a_ref, b_ref, y_ref, so_ref, s_ref):
    bsz = s0_ref.shape[-1]
    tn = r_ref.shape[1] // bsz
    s_ref[...] = s0_ref[0]
    for t in range(tn):
        cols = slice(t * bsz, (t + 1) * bsz)
        r_t, w_t, k_t = r_ref[:, cols], w_ref[:, cols], k_ref[:, cols]
        a_t, b_t = a_ref[:, cols], b_ref[:, cols]

        def body(i, _):
            s_i = s_ref[i]
            sa = jnp.sum(s_i * a_t, axis=0, keepdims=True)
            v_i = v_ref[i, :, cols]
            s_i = s_i * w_t + sa * b_t + v_i * k_t
            s_ref[i] = s_i
            y_ref[i, :, cols] = jnp.sum(s_i * r_t, axis=0, keepdims=True)
            return 0

        lax.fori_loop(0, HD, body, 0)
    so_ref[0] = s_ref[...]


def _rwkv_sample_scan(s0, r, w, k, v, a_, b_):
    nh, _, _, bsz = s0.shape
    tb = r.shape[1]
    vec = pl.BlockSpec((HD, tb), lambda h: (h, 0))
    vec3 = pl.BlockSpec((HD, 1, tb), lambda h: (h, 0, 0))
    st = pl.BlockSpec((1, HD, HD, bsz), lambda h: (h, 0, 0, 0))
    return pl.pallas_call(
        _rwkv_sample_scan_kernel,
        out_shape=(jax.ShapeDtypeStruct((nh * HD, 1, tb), F32), jax.ShapeDtypeStruct(s0.shape, F32)),
        grid=(nh,),
        in_specs=[st, vec, vec, vec, vec3, vec, vec],
        out_specs=(vec3, st),
        scratch_shapes=[pltpu.VMEM((HD, HD, bsz), F32)],
        compiler_params=_cparams(("parallel",)),
        name="rwkv_sample_scan",
    )(s0, r, w, k, v, a_, b_)


def _rwkv_sample_post_kernel(y_ref, r_ref, k_ref, v_ref, g_ref, *rest):
    prm_refs = rest[:len(_PRM_NAMES)]
    o_ref = rest[len(_PRM_NAMES)]
    prm = {n: ref[...] for n, ref in zip(_PRM_NAMES, prm_refs)}
    out = _rwkv_post(_to_pm(y_ref[...]), _to_pm(r_ref[...]), _to_pm(k_ref[...]),
                     _to_pm(v_ref[...]), _to_pm(g_ref[...]), prm)
    o_ref[...] = _from_pm(out).astype(o_ref.dtype)


def _rwkv_sample_post(y, r, k, v, g, prm, bsz):
    n = y.shape[0]
    blk = pl.BlockSpec((bsz, D), lambda t: (t, 0))
    return pl.pallas_call(
        _rwkv_sample_post_kernel,
        out_shape=jax.ShapeDtypeStruct((n, D), BF16),
        grid=(n // bsz,),
        in_specs=[blk] * 5 + _prm_specs(prm, 1),
        out_specs=blk,
        compiler_params=_cparams(("parallel",)),
        name="rwkv_sample_post",
    )(y, r, k, v, g, *[prm[n] for n in _PRM_NAMES])


def _outproj_kernel(attp_ref, atts_ref, rwp_ref, rws_ref, ga_ref, gr_ref, xp_ref, xs_ref, w_ref, g2_ref, *rest,
                    n_experts, n_prompt_tiles):
    is_prompt = pl.program_id(0) < n_prompt_tiles
    att = jnp.where(is_prompt, attp_ref[...], atts_ref[...]).astype(F32)
    rw = jnp.where(is_prompt, rwp_ref[...], rws_ref[...]).astype(F32)
    x = jnp.where(is_prompt, xp_ref[...], xs_ref[...])
    m = _sigmoid(ga_ref[...]) * att + _sigmoid(gr_ref[...]) * rw
    xn = x + jnp.dot(m.astype(BF16), w_ref[...], preferred_element_type=F32)
    h2 = _rms(xn, g2_ref[...])
    if n_experts:
        router_ref, xo_ref, h2_ref, gate_ref = rest
        logits = jnp.dot(h2, router_ref[...], preferred_element_type=F32, precision=HIGHEST)
        lane = lax.broadcasted_iota(jnp.int32, logits.shape, 1).astype(F32)
        lg = jnp.where(lane < n_experts, logits, -jnp.inf)
        v1 = jnp.max(lg, -1, keepdims=True)
        i1 = jnp.min(jnp.where(lg == v1, lane, float(LANES)), -1, keepdims=True)
        lg2 = jnp.where(lane == i1, -jnp.inf, lg)
        v2 = jnp.max(lg2, -1, keepdims=True)
        i2 = jnp.min(jnp.where(lg2 == v2, lane, float(LANES)), -1, keepdims=True)
        e2 = jnp.exp(v2 - v1)
        den = 1.0 + e2
        gate_ref[...] = (jnp.where(lane == 0.0, i1, 0.0) + jnp.where(lane == 1.0, i2, 0.0)
                         + jnp.where(lane == 2.0, 1.0 / den, 0.0) + jnp.where(lane == 3.0, e2 / den, 0.0))
        h2_ref[...] = h2
    else:
        xo_ref, h2_ref = rest
        h2_ref[...] = h2.astype(BF16)
    xo_ref[...] = xn


def _outproj(att_p, att_s, rw_p, rw_s, p_all, x_p, x_s, s_base, w_out, g2, router, n_experts, tm):
    n = att_p.shape[0] + att_s.shape[0]
    npt = att_p.shape[0] // tm
    row = lambda w_: pl.BlockSpec((tm, w_), lambda i: (i, 0))
    p_spec, s_spec = _two_part_specs((tm, D), npt, 0)
    xp_spec, xs_spec = _two_part_specs((tm, D), npt, s_base)
    in_specs = [p_spec, s_spec, p_spec, s_spec,
                pl.BlockSpec((tm, D), lambda i: (i, C_GATE // D)),
                pl.BlockSpec((tm, D), lambda i: (i, C_GATE // D + 1)),
                xp_spec, xs_spec,
                pl.BlockSpec((D, D), lambda i: (0, 0)),
                pl.BlockSpec((1, D), lambda i: (0, 0))]
    args = [att_p, att_s, rw_p, rw_s, p_all, p_all, x_p, x_s, w_out, g2]
    out_shape = [jax.ShapeDtypeStruct((n, D), F32), jax.ShapeDtypeStruct((n, D), F32 if n_experts else BF16)]
    out_specs = [row(D), row(D)]
    if n_experts:
        in_specs.append(pl.BlockSpec((D, LANES), lambda i: (0, 0)))
        args.append(router)
        out_shape.append(jax.ShapeDtypeStruct((n, LANES), F32))
        out_specs.append(row(LANES))
    return pl.pallas_call(
        functools.partial(_outproj_kernel, n_experts=n_experts, n_prompt_tiles=npt),
        out_shape=tuple(out_shape),
        grid=(n // tm,),
        in_specs=in_specs,
        out_specs=tuple(out_specs),
        compiler_params=_cparams(("parallel",)),
        name="outproj",
    )(*args)


def _swiglu_part(h, wg, wu, wd):
    a = jnp.dot(h, wg, preferred_element_type=F32)
    b = jnp.dot(h, wu, preferred_element_type=F32)
    t = (a * _sigmoid(a) * b).astype(BF16)
    return jnp.dot(t, wd, preferred_element_type=F32)


def _final_store(y, gf_ref, out_refs, n_prompt_tiles):
    if gf_ref is None:
        (o_ref,) = out_refs
        o_ref[...] = y
        return
    op_ref, os_ref = out_refs
    y = _rms(y, gf_ref[...])
    is_prompt = pl.program_id(0) < n_prompt_tiles

    @pl.when(is_prompt)
    def _():
        op_ref[...] = y

    @pl.when(jnp.logical_not(is_prompt))
    def _():
        os_ref[...] = y


def _final_out(n, n_p, tm, final):
    if not final:
        return jax.ShapeDtypeStruct((n, D), F32), pl.BlockSpec((tm, D), lambda i, *_: (i, 0))
    return ((jax.ShapeDtypeStruct((n_p, D), F32), jax.ShapeDtypeStruct((n - n_p, D), F32)),
            _two_part_specs((tm, D), n_p // tm, 0))


def _ffn_kernel(h_ref, wg_ref, wu_ref, wd_ref, x_ref, *rest, final, n_prompt_tiles):
    rest = list(rest)
    gf_ref = rest.pop(0) if final else None
    acc_ref = rest.pop()
    f = pl.program_id(1)

    @pl.when(f == 0)
    def _():
        acc_ref[...] = jnp.zeros_like(acc_ref)

    acc_ref[...] += _swiglu_part(h_ref[...], wg_ref[...], wu_ref[...], wd_ref[...])

    @pl.when(f == pl.num_programs(1) - 1)
    def _():
        _final_store(x_ref[...] + acc_ref[...], gf_ref, rest, n_prompt_tiles)


def _ffn(h, wg, wu, wd, x, gf, n_p, tm, tf):
    n = x.shape[0]
    fdim = wg.shape[1]
    final = gf is not None
    in_specs = [pl.BlockSpec((tm, D), lambda i, f: (i, 0)),
                pl.BlockSpec((D, tf), lambda i, f: (0, f)),
                pl.BlockSpec((D, tf), lambda i, f: (0, f)),
                pl.BlockSpec((tf, D), lambda i, f: (f, 0)),
                pl.BlockSpec((tm, D), lambda i, f: (i, 0))]
    args = [h, wg, wu, wd, x]
    if final:
        in_specs.append(pl.BlockSpec((1, D), lambda i, f: (0, 0)))
        args.append(gf)
    out_shape, out_specs = _final_out(n, n_p, tm, final)
    return pl.pallas_call(
        functools.partial(_ffn_kernel, final=final, n_prompt_tiles=n_p // tm),
        out_shape=out_shape,
        grid=(n // tm, fdim // tf),
        in_specs=in_specs,
        out_specs=out_specs,
        scratch_shapes=[pltpu.VMEM((tm, D), F32)],
        compiler_params=_cparams(("arbitrary" if final else "parallel", "arbitrary")),
        name="ffn",
    )(*args)


MOE_TM = 512


def _moe_plan(route, n_experts):
    n = route.shape[0]
    e_flat = route[:, :2].astype(jnp.int32).T.reshape(-1)
    n_asg = e_flat.shape[0]
    order = jnp.argsort(e_flat, stable=True).astype(jnp.int32)
    counts = jnp.sum((e_flat[:, None] == jnp.arange(n_experts)[None, :]).astype(jnp.int32), axis=0)
    first = jnp.cumsum(counts) - counts
    padded = (counts + MOE_TM - 1) // MOE_TM * MOE_TM
    ends = jnp.cumsum(padded)
    offs = ends - padded
    n_tiles = -(-n_asg // MOE_TM) + n_experts
    start = jnp.arange(n_tiles, dtype=jnp.int32) * MOE_TM
    n_active = ends[-1] // MOE_TM
    tile_e = jnp.sum((start[:, None] >= ends[None, :]).astype(jnp.int32), axis=1)
    last_e = jnp.sum(((n_active - 1) * MOE_TM >= ends).astype(jnp.int32))
    tile_e = jnp.minimum(tile_e, last_e)
    n_valid = jnp.clip(counts[tile_e] - (start - offs[tile_e]), 0, MOE_TM)
    n_valid = jnp.where(start < ends[-1], n_valid, 0)
    r_in_tile = jnp.arange(MOE_TM, dtype=jnp.int32)[None, :]
    src = (first[tile_e] + start - offs[tile_e])[:, None] + r_in_tile
    real = r_in_tile < n_valid[:, None]
    dst = jnp.where(real, order[jnp.where(real, src, 0)], 0).reshape(-1)
    tok = jnp.where(dst >= n, dst - n, dst)
    return (tile_e.astype(jnp.int32), n_valid.astype(jnp.int32), n_active.reshape(1).astype(jnp.int32),
            tok.astype(jnp.int32), dst.astype(jnp.int32))


def _moe_kernel(te_ref, nv_ref, na_ref, tok_ref, dst_ref, h_hbm, wg_ref, wu_ref, wd_ref, o_hbm,
                xbuf, xb, acc, gsem, ssem, *, nf):
    del te_ref
    j, f = pl.program_id(0), pl.program_id(1)
    n_tiles = pl.num_programs(0)
    n_active = na_ref[0]
    active = j < n_active
    slot = j % 2
    other = 1 - slot
    dump0 = o_hbm.shape[0] - MOE_TM
    rows_per_step = MOE_TM // nf

    def gather_row(tile, slot_, r):
        tok = tok_ref[tile * MOE_TM + r]
        return pltpu.make_async_copy(h_hbm.at[pl.ds(tok, 1)], xbuf.at[slot_, pl.ds(r, 1)], gsem.at[slot_])

    def gather_wait(slot_):
        pltpu.make_async_copy(h_hbm.at[pl.ds(0, MOE_TM)], xbuf.at[slot_], gsem.at[slot_]).wait()

    def scatter_row(tile, slot_, r, real):
        row = jnp.where(real & (r < nv_ref[tile]), dst_ref[tile * MOE_TM + r], dump0 + r)
        return pltpu.make_async_copy(acc.at[slot_, pl.ds(r, 1)], o_hbm.at[pl.ds(row, 1)], ssem.at[slot_])

    def scatter_wait(slot_):
        pltpu.make_async_copy(acc.at[slot_], o_hbm.at[pl.ds(0, MOE_TM)], ssem.at[slot_]).wait()

    @pl.when(active & (f == 0))
    def _():
        @pl.when(j == 0)
        def _():
            def body(r, _):
                gather_row(0, 0, r).start()
                return 0
            lax.fori_loop(0, MOE_TM, body, 0, unroll=SUBLANES)
            acc[1] = jnp.zeros((MOE_TM, D), F32)

        gather_wait(slot)
        xb[...] = xbuf[slot].astype(BF16)

        @pl.when(j > 0)
        def _():
            scatter_wait(slot)

    @pl.when(active)
    def _():
        part = _swiglu_part(xb[...], wg_ref[0], wu_ref[0], wd_ref[0])
        nxt = jnp.minimum(j + 1, n_tiles - 1)
        prev = jnp.maximum(j - 1, 0)
        for u in range(rows_per_step):
            r = f * rows_per_step + u
            gather_row(nxt, other, r).start()
            scatter_row(prev, other, r, j > 0).start()

        @pl.when(f == 0)
        def _():
            acc[slot] = part

        @pl.when(f > 0)
        def _():
            acc[slot] += part

        @pl.when((f == nf - 1) & (j == n_active - 1))
        def _():
            gather_wait(other)
            scatter_wait(other)

            def body(r, _):
                scatter_row(j, slot, r, True).start()
                return 0
            lax.fori_loop(0, MOE_TM, body, 0, unroll=SUBLANES)
            scatter_wait(slot)


def _moe_experts(h, plan, wg, wu, wd, tf):
    tile_e, n_valid, n_active, tok, dst = plan
    n = h.shape[0]
    fdim = wg.shape[2]
    n_tiles = tile_e.shape[0]
    nf = fdim // tf

    def wmap(j, f, te, nv, na, tok_, dst_):
        return te[j], jnp.where(j < na[0], f, nf - 1)

    grid_spec = pltpu.PrefetchScalarGridSpec(
        num_scalar_prefetch=5,
        grid=(n_tiles, nf),
        in_specs=[pl.BlockSpec(memory_space=pl.ANY),
                  pl.BlockSpec((1, D, tf), lambda *a: (wmap(*a)[0], 0, wmap(*a)[1])),
                  pl.BlockSpec((1, D, tf), lambda *a: (wmap(*a)[0], 0, wmap(*a)[1])),
                  pl.BlockSpec((1, tf, D), lambda *a: (wmap(*a)[0], wmap(*a)[1], 0))],
        out_specs=pl.BlockSpec(memory_space=pl.ANY),
        scratch_shapes=[pltpu.VMEM((2, MOE_TM, D), F32),
                        pltpu.VMEM((MOE_TM, D), BF16),
                        pltpu.VMEM((2, MOE_TM, D), F32),
                        pltpu.SemaphoreType.DMA((2,)),
                        pltpu.SemaphoreType.DMA((2,))],
    )
    return pl.pallas_call(
        functools.partial(_moe_kernel, nf=nf),
        out_shape=jax.ShapeDtypeStruct((2 * n + MOE_TM, D), F32),
        grid_spec=grid_spec,
        compiler_params=_cparams(("arbitrary", "arbitrary"), disable_bounds_checks=True),
        name="moe_experts",
    )(tile_e, n_valid, n_active, tok, dst, h, wg, wu, wd)


def _moe_combine_kernel(x_ref, o1_ref, o2_ref, route_ref, *rest, final, n_prompt_tiles):
    rest = list(rest)
    gf_ref = rest.pop(0) if final else None
    route = route_ref[...]
    lane = lax.broadcasted_iota(jnp.int32, route.shape, 1)
    w1 = jnp.sum(jnp.where(lane == 2, route, 0.0), -1, keepdims=True)
    w2 = jnp.sum(jnp.where(lane == 3, route, 0.0), -1, keepdims=True)
    _final_store(x_ref[...] + (w1 * o1_ref[...] + w2 * o2_ref[...]), gf_ref, rest, n_prompt_tiles)


def _moe_combine(x, o, route, gf, n_p, tm):
    n = x.shape[0]
    nb = n // tm
    final = gf is not None
    in_specs = [pl.BlockSpec((tm, D), lambda i: (i, 0)),
                pl.BlockSpec((tm, D), lambda i: (i, 0)),
                pl.BlockSpec((tm, D), lambda i: (i + nb, 0)),
                pl.BlockSpec((tm, LANES), lambda i: (i, 0))]
    args = [x, o, o, route]
    if final:
        in_specs.append(pl.BlockSpec((1, D), lambda i: (0, 0)))
        args.append(gf)
    out_shape, out_specs = _final_out(n, n_p, tm, final)
    return pl.pallas_call(
        functools.partial(_moe_combine_kernel, final=final, n_prompt_tiles=n_p // tm),
        out_shape=out_shape,
        grid=(nb,),
        in_specs=in_specs,
        out_specs=out_specs,
        compiler_params=_cparams(("arbitrary",)),
        name="moe_combine",
    )(*args)


def _split_shift_cols(a):
    pad = jnp.zeros(a.shape[:-1] + (LORA_PAD - (LORA_W + LORA_A + LORA_G),), a.dtype)
    lead = a[..., 3 * D:3 * D + LORA_W + LORA_A]
    gd = a[..., 3 * D + LORA_W + LORA_A:]
    return a[..., :3 * D], jnp.concatenate([lead, gd, pad], -1)


def _relayout_w_in(w):
    q, k, v = w[:, 0:D], w[:, D:D + 256], w[:, D + 256:D + 512]
    pr = w[:, D + 512:D + 512 + 3360]
    gates = w[:, D + 512 + 3360:]
    rkv, lora = _split_shift_cols(pr)
    return jnp.concatenate([rkv, q, gates, k, v, lora], axis=1).astype(BF16)


def _rope_tables(pos):
    inv = ROPE_THETA ** (-jnp.arange(0, ROT, 2, dtype=F32) / ROT)
    ang = pos.astype(F32)[:, None] * inv[None, :]
    cos, sin = jnp.cos(ang), jnp.sin(ang)
    n = pos.shape[0]
    half = ROT // 2
    one = jnp.ones((n, HD - ROT), F32)
    zero = jnp.zeros((n, HD - half), F32)
    c = jnp.concatenate([cos, cos, one], 1)
    sa = jnp.concatenate([-sin, zero], 1)
    sb = jnp.concatenate([jnp.zeros((n, half), F32), sin, jnp.zeros((n, HD - ROT), F32)], 1)
    tile = lambda a: jnp.concatenate([a, a], 1)
    return tile(c), tile(sa), tile(sb)


def _pair_state_to_heads(s):
    even = s[:, :, :HD, :HD]
    odd = s[:, :, HD:, HD:]
    b = s.shape[0]
    return jnp.stack([even, odd], axis=2).reshape(b, 2 * N_PAIR, HD, HD)


def kernel(x_prompt, x_sample, cache_k_win, cache_v_win, state_wkv, state_shift, norm_mix_g, w_in, w_out, attn_sinks, shift_mu, decay_w0, decay_up, iclr_a0, iclr_up, gate_up, key_kk, key_ka, bonus_rk, lnx_g, lnx_b, norm_ffn_g, ffn_w_gate, ffn_w_up, ffn_w_down, moe_router, moe_w_gate, moe_w_up, moe_w_down, norm_final_g):
    batch, seq, _ = x_prompt.shape
    sb, st, _ = x_sample.shape
    depth = w_in.shape[0]
    n_p, n_s = batch * seq, sb * st
    n = n_p + n_s
    n_buf = cache_k_win.shape[2]
    if depth == 0:
        raise ValueError("depth must be positive")
    tm = next(c for c in (512, 256, 128) if n_p % c == 0 and n_s % c == 0 and seq % c == 0)
    npt = n_p // tm
    tm_in = 2 * tm if n_p % (2 * tm) == 0 and seq % (2 * tm) == 0 else tm
    n_in = -(-n // tm_in) * tm_in
    pad_rows = lambda a: jnp.pad(a, ((0, n_in - n_p - a.shape[0]), (0, 0)))

    x_p, x_s, s_base = x_prompt.reshape(n_p, D), pad_rows(x_sample.transpose(1, 0, 2).reshape(n_s, D)), 0
    rope_p = _rope_tables(jnp.arange(seq))
    rope_s = tuple(pad_rows(t) for t in _rope_tables(jnp.repeat(PAST_LEN + jnp.arange(st), sb)))
    row = lambda a: a.reshape(1, -1)

    new_p, new_s = [], []
    for l in range(depth):
        mu_rkv, mu_lora = _split_shift_cols(row(shift_mu[l]))
        g_up = jnp.concatenate([gate_up[l], jnp.zeros((G_PAD - LORA_G, D), F32)], 0)
        prm = dict(mu_rkv=mu_rkv, mu_lora=mu_lora, w0=row(decay_w0[l]), w_up=decay_up[l],
                   a0=row(iclr_a0[l]), a_up=iclr_up[l], g_up=g_up, k_k=row(key_kk[l]),
                   k_a=row(key_ka[l]), r_k=row(bonus_rk[l]), lnx_g=row(lnx_g[l]), lnx_b=row(lnx_b[l]))
        expand = lambda rows: {**prm, **{k_: _param_pm(prm[k_], rows) for k_ in ("k_k", "k_a", "r_k", "lnx_g", "lnx_b")}}
        prm_p, prm_s = expand(RWKV_CHUNK), expand(sb)
        p_all = _inproj(x_p, x_s, n_p // tm_in if l else 0, n_p // tm_in, n_in,
                        row(norm_mix_g[l]), _relayout_w_in(w_in[l]), rope_p, rope_s, seq, tm_in)

        att_p = _attn_prompt(p_all, attn_sinks[l], batch, seq)
        ps = p_all[n_p:n].reshape(st, sb, N_COLS).transpose(1, 0, 2)
        k_new, v_new = ps[..., C_K:C_K + 256], ps[..., C_V:C_V + 256]
        k_cache = cache_k_win[l].reshape(sb, n_buf, 256)
        v_cache = cache_v_win[l].reshape(sb, n_buf, 256)
        att_s = _attn_sample(ps[..., C_Q:C_Q + D], k_new, v_new, k_cache, v_cache, attn_sinks[l])
        att_s = pad_rows(att_s.transpose(1, 0, 2).reshape(n_s, D))

        rw_p, s_pairs = _rwkv_prompt(p_all, prm_p, batch, seq)
        sh_rkv, sh_lora = _split_shift_cols(state_shift[l])
        r_s, w_s, k_s, v_s, a_s, b_s, g_s = _rwkv_sample_pre(p_all, n_p, sh_rkv, sh_lora, prm_s, sb, st)
        s0 = state_wkv[l].transpose(1, 2, 3, 0)
        y_t, s_fin = _rwkv_sample_scan(s0, r_s.T, w_s.T, k_s.T, v_s.T[:, None, :], a_s.T, b_s.T)
        rw_s = pad_rows(_rwkv_sample_post(y_t[:, 0, :].T, r_s, k_s, v_s, g_s, prm_s, sb))

        is_moe = l % 2 == 1
        last = l == depth - 1
        gf = row(norm_final_g) if last else None
        if is_moe:
            ne = moe_router.shape[-1]
            router = jnp.concatenate([moe_router[l // 2], jnp.zeros((D, LANES - ne), F32)], 1)
            x_mid, h2, route = _outproj(att_p, att_s, rw_p, rw_s, p_all, x_p, x_s, s_base, w_out[l].astype(BF16),
                                        row(norm_ffn_g[l]), router, ne, tm)
            o_exp = _moe_experts(h2, _moe_plan(route, ne), moe_w_gate[l // 2].astype(BF16),
                                 moe_w_up[l // 2].astype(BF16), moe_w_down[l // 2].astype(BF16),
                                 _pick(moe_w_gate.shape[-1], (896, 512, 256, 128)))
            x = _moe_combine(x_mid, o_exp, route, gf, n_p, tm)
        else:
            x_mid, h2 = _outproj(att_p, att_s, rw_p, rw_s, p_all, x_p, x_s, s_base, w_out[l].astype(BF16),
                                 row(norm_ffn_g[l]), None, 0, tm)
            x = _ffn(h2, ffn_w_gate[l // 2].astype(BF16), ffn_w_up[l // 2].astype(BF16),
                     ffn_w_down[l // 2].astype(BF16), x_mid, gf, n_p, tm,
                     _pick(ffn_w_gate.shape[-1], (1408, 512, 256, 128)))
        if not last:
            x_p, x_s, s_base = x, x, npt

        n_win = min(WINDOW, seq)
        tail = lambda rows, c0, w_: jnp.stack(
            [lax.slice(p_all, ((b + 1) * seq - rows, c0), ((b + 1) * seq, c0 + w_)) for b in range(batch)])
        k_p = tail(n_win, C_K, N_KV * HD).reshape(batch, n_win, N_KV, HD)
        v_p = tail(n_win, C_V, N_KV * HD).reshape(batch, n_win, N_KV, HD)
        unsplit = lambda a: jnp.concatenate(
            [a[..., C_RKV:C_RKV + 3 * D], a[..., C_LORA:C_LORA + LORA_W + LORA_A + LORA_G]], -1)
        last_p = jnp.concatenate([tail(1, C_RKV, 3 * D), tail(1, C_LORA, LORA_W + LORA_A + LORA_G)], -1)[:, 0]
        new_p.append((k_p, v_p, _pair_state_to_heads(s_pairs), last_p))
        k_s_win = jnp.concatenate([k_cache, k_new], 1)[:, -n_buf:].reshape(sb, n_buf, N_KV, HD)
        v_s_win = jnp.concatenate([v_cache, v_new], 1)[:, -n_buf:].reshape(sb, n_buf, N_KV, HD)
        new_s.append((k_s_win, v_s_win, s_fin.transpose(3, 0, 1, 2), unsplit(ps[:, -1])))

    y_p, y_s = x
    y_p = y_p.reshape(batch, seq, D)
    y_s = y_s[:n_s].reshape(st, sb, D).transpose(1, 0, 2)
    stk = lambda sts, i: jnp.stack([s[i] for s in sts])
    return (y_p, y_s,
            stk(new_p, 0), stk(new_p, 1), stk(new_p, 2), stk(new_p, 3),
            stk(new_s, 0), stk(new_s, 1), stk(new_s, 2), stk(new_s, 3))
```

```python
import functools

import jax
import jax.numpy as jnp
from jax import lax
from jax.experimental import pallas as pl
from jax.experimental.pallas import tpu as pltpu

F32 = jnp.float32
BF16 = jnp.bfloat16
HIGHEST = lax.Precision.HIGHEST

LANES = 128
SUBLANES = 8
VMEM_LIMIT = 56 * 1024 * 1024

D = 1024
HD = 64
N_Q = 16
N_KV = 4
ROT = 16
ROPE_THETA = 500000.0
WINDOW = 128
RMS_EPS = 1e-5
LNX_EPS = 64e-5
N_PAIR = D // LANES
LORA_W, LORA_A, LORA_G = 64, 64, 160
LORA_PAD = 512
G_PAD = 256

C_RKV = 0
C_Q = 3072
C_GATE = 4096
C_K = 6144
C_V = 6400
C_LORA = 6656
N_COLS = 7168
TN_IN = 1024
Q_TILE = C_Q // TN_IN
KV_TILE = C_K // TN_IN

RWKV_CHUNK = 64


PAST_LEN = 16384


def _pick(n, cands):
    return next(c for c in cands if n % c == 0)


def _cparams(sem, **kw):
    return pltpu.CompilerParams(dimension_semantics=sem, vmem_limit_bytes=VMEM_LIMIT, **kw)


def _rms(x, g):
    return x * lax.rsqrt(jnp.mean(x * x, -1, keepdims=True) + RMS_EPS) * g


def _sigmoid(x):
    return 1.0 / (1.0 + jnp.exp(-x))


def _rope_chunk(a, c, sa, sb):
    return a * c + pltpu.roll(a, LANES - ROT // 2, 1) * sa + pltpu.roll(a, ROT // 2, 1) * sb


def _inproj_kernel(xp_ref, xs_ref, g_ref, w_ref, cp_ref, sap_ref, sbp_ref, cs_ref, sas_ref, sbs_ref,
                   o_ref, h_ref, *, n_prompt_tiles):
    i, j = pl.program_id(0), pl.program_id(1)
    is_prompt = i < n_prompt_tiles

    @pl.when(j == 0)
    def _():
        x = jnp.where(is_prompt, xp_ref[...], xs_ref[...])
        h_ref[...] = _rms(x, g_ref[...]).astype(BF16)

    acc = jnp.dot(h_ref[...], w_ref[...], preferred_element_type=F32)

    def roped(n_chunks):
        c = jnp.where(is_prompt, cp_ref[...], cs_ref[...])
        sa = jnp.where(is_prompt, sap_ref[...], sas_ref[...])
        sb = jnp.where(is_prompt, sbp_ref[...], sbs_ref[...])
        parts = [_rope_chunk(acc[:, k * LANES:(k + 1) * LANES], c, sa, sb) for k in range(n_chunks)]
        if n_chunks * LANES < TN_IN:
            parts.append(acc[:, n_chunks * LANES:])
        return jnp.concatenate(parts, axis=1)

    @pl.when(j == Q_TILE)
    def _():
        o_ref[...] = roped(TN_IN // LANES)

    @pl.when(j == KV_TILE)
    def _():
        o_ref[...] = roped(N_KV * HD // LANES)

    @pl.when((j != Q_TILE) & (j != KV_TILE))
    def _():
        o_ref[...] = acc


def _two_part_specs(block, npt, s_base, period=None):
    def p_map(i, *_):
        ip = jnp.minimum(i, npt - 1)
        return (ip % period if period else ip, 0)

    def s_map(i, *_):
        return (s_base + jnp.maximum(i - npt, 0), 0)

    return pl.BlockSpec(block, p_map), pl.BlockSpec(block, s_map)


def _inproj(x_p, x_s, s_base, npt, n, g, w, rope_p, rope_s, seq, tm):
    xp_spec, xs_spec = _two_part_specs((tm, D), npt, s_base)
    rp_spec, rs_spec = _two_part_specs((tm, LANES), npt, 0, period=seq // tm)
    return pl.pallas_call(
        functools.partial(_inproj_kernel, n_prompt_tiles=npt),
        out_shape=jax.ShapeDtypeStruct((n, N_COLS), F32),
        grid=(n // tm, N_COLS // TN_IN),
        in_specs=[xp_spec, xs_spec,
                  pl.BlockSpec((1, D), lambda i, j: (0, 0)),
                  pl.BlockSpec((D, TN_IN), lambda i, j: (0, j)),
                  rp_spec, rp_spec, rp_spec, rs_spec, rs_spec, rs_spec],
        out_specs=pl.BlockSpec((tm, TN_IN), lambda i, j: (i, j)),
        scratch_shapes=[pltpu.VMEM((tm, D), BF16)],
        compiler_params=_cparams(("parallel", "arbitrary")),
        name="inproj",
    )(x_p, x_s, g, w, *rope_p, *rope_s)


def _sink_softmax(s, mask, sink):
    s = jnp.where(mask, s * (HD ** -0.5), -jnp.inf)
    m = jnp.maximum(jnp.max(s, -1, keepdims=True), sink)
    p = jnp.exp(s - m)
    return p / (jnp.sum(p, -1, keepdims=True) + jnp.exp(sink - m))


def _dot_nt(a, b, **kw):
    return lax.dot_general(a, b, (((1,), (1,)), ((), ())), preferred_element_type=F32, **kw)


def _dot_tn(a, b, **kw):
    return lax.dot_general(a, b, (((0,), (0,)), ((), ())), preferred_element_type=F32, **kw)


def _head_operands(x, half):
    lo = lax.broadcasted_iota(jnp.int32, x.shape, 1) < HD
    sw = pltpu.roll(x, HD, 1)
    x_lo, x_hi = (x, sw) if half == 0 else (sw, x)
    return jnp.where(lo, x_lo, 0.0).astype(BF16), jnp.where(lo, 0.0, x_hi).astype(BF16)


def _group_queries(q, g):
    return jnp.concatenate([q[:, (2 * g) * LANES:(2 * g + 1) * LANES],
                            q[:, (2 * g + 1) * LANES:(2 * g + 2) * LANES]], axis=0)


def _group_sinks(sink_ref, g, top):
    return (jnp.where(top, sink_ref[4 * g], sink_ref[4 * g + 2]),
            jnp.where(top, sink_ref[4 * g + 1], sink_ref[4 * g + 3]))


def _attn_prompt_kernel(sink_ref, q_ref, kp_ref, kc_ref, vp_ref, vc_ref, o_ref):
    blk = pl.program_id(1)
    w = WINDOW
    q = q_ref[...].astype(BF16)
    k = jnp.concatenate([kp_ref[...], kc_ref[...]], axis=0)
    v = jnp.concatenate([vp_ref[...], vc_ref[...]], axis=0)
    qi = lax.broadcasted_iota(jnp.int32, (w, w), 0)
    kj = lax.broadcasted_iota(jnp.int32, (w, w), 1)
    band = jnp.concatenate([kj <= qi] * 2, axis=0)
    mask = band | (blk > 0)
    top = lax.broadcasted_iota(jnp.int32, (2 * w, 1), 0) < w
    outs = []
    for g in range(N_KV):
        ch, half = divmod(g, 2)
        q2 = _group_queries(q, g)
        o = 0.0
        for k_, v_, sink in zip(_head_operands(k[:, ch * LANES:(ch + 1) * LANES], half),
                                _head_operands(v[:, ch * LANES:(ch + 1) * LANES], half),
                                _group_sinks(sink_ref, g, top)):
            s = _dot_nt(q2, k_)
            p = _sink_softmax(jnp.where(band, s[:, w:], s[:, :w]), mask, sink)
            p_cat = jnp.concatenate([jnp.where(band, 0.0, p), jnp.where(band, p, 0.0)], axis=1)
            o = o + jnp.dot(p_cat.astype(BF16), v_, preferred_element_type=F32)
        outs += [o[:w], o[w:]]
    o_ref[...] = jnp.concatenate(outs, axis=1).astype(o_ref.dtype)


def _attn_prompt(p_all, sinks, batch, seq):
    nb = seq // WINDOW
    kcol, vcol = C_K // (N_KV * HD), C_V // (N_KV * HD)
    cur = lambda b, i: b * nb + i
    prev = lambda b, i: b * nb + jnp.maximum(i - 1, 0)
    return pl.pallas_call(
        _attn_prompt_kernel,
        out_shape=jax.ShapeDtypeStruct((batch * seq, D), BF16),
        grid=(batch, nb),
        in_specs=[
            pl.BlockSpec(memory_space=pltpu.SMEM),
            pl.BlockSpec((WINDOW, D), lambda b, i: (cur(b, i), C_Q // D)),
            pl.BlockSpec((WINDOW, N_KV * HD), lambda b, i: (prev(b, i), kcol)),
            pl.BlockSpec((WINDOW, N_KV * HD), lambda b, i: (cur(b, i), kcol)),
            pl.BlockSpec((WINDOW, N_KV * HD), lambda b, i: (prev(b, i), vcol)),
            pl.BlockSpec((WINDOW, N_KV * HD), lambda b, i: (cur(b, i), vcol)),
        ],
        out_specs=pl.BlockSpec((WINDOW, D), lambda b, i: (cur(b, i), 0)),
        compiler_params=_cparams(("parallel", "arbitrary")),
        name="attn_prompt",
    )(sinks, p_all, p_all, p_all, p_all, p_all)


ATT_S_BT = 8
T_PAD = 8


def _attn_sample_kernel(sink_ref, q_ref, kn_ref, vn_ref, kc_ref, vc_ref, o_ref):
    tn = q_ref.shape[1]
    nbuf = kc_ref.shape[1]
    rows = 2 * tn
    keys = nbuf + T_PAD
    r = lax.broadcasted_iota(jnp.int32, (rows, keys), 0)
    t = jnp.where(r >= tn, r - tn, r)
    kj = lax.broadcasted_iota(jnp.int32, (rows, keys), 1)
    mask = (kj > t + (nbuf - WINDOW)) & (kj <= t + nbuf)
    top = lax.broadcasted_iota(jnp.int32, (rows, 1), 0) < tn
    zpad = jnp.zeros((T_PAD - tn, N_KV * HD), F32)
    scores, sinks, values = [], [], []
    for b in range(ATT_S_BT):
        q = q_ref[b].astype(BF16)
        k = jnp.concatenate([kc_ref[b], kn_ref[b], zpad], axis=0)
        v = jnp.concatenate([vc_ref[b], vn_ref[b], zpad], axis=0)
        for g in range(N_KV):
            ch, half = divmod(g, 2)
            q2 = _group_queries(q, g)
            values += list(_head_operands(v[:, ch * LANES:(ch + 1) * LANES], half))
            sinks += list(_group_sinks(sink_ref, g, top))
            scores += [_dot_nt(q2, k_) for k_ in _head_operands(k[:, ch * LANES:(ch + 1) * LANES], half)]
    p = _sink_softmax(jnp.stack(scores), mask[None], jnp.stack(sinks)).astype(BF16)
    for b in range(ATT_S_BT):
        outs = []
        for g in range(N_KV):
            i = 2 * (b * N_KV + g)
            o = (jnp.dot(p[i], values[i], preferred_element_type=F32)
                 + jnp.dot(p[i + 1], values[i + 1], preferred_element_type=F32))
            outs += [o[:tn], o[tn:]]
        o_ref[b] = jnp.concatenate(outs, axis=1).astype(o_ref.dtype)


def _attn_sample(q, k_new, v_new, k_cache, v_cache, sinks):
    bsz, tn, _ = q.shape
    nbuf = k_cache.shape[1]
    kvw = N_KV * HD
    blk = lambda w_, r_: pl.BlockSpec((ATT_S_BT, r_, w_), lambda i: (i, 0, 0))
    return pl.pallas_call(
        _attn_sample_kernel,
        out_shape=jax.ShapeDtypeStruct((bsz, tn, D), BF16),
        grid=(bsz // ATT_S_BT,),
        in_specs=[pl.BlockSpec(memory_space=pltpu.SMEM),
                  blk(D, tn), blk(kvw, tn), blk(kvw, tn), blk(kvw, nbuf), blk(kvw, nbuf)],
        out_specs=blk(D, tn),
        compiler_params=_cparams(("parallel",)),
        name="attn_sample",
    )(sinks, q, k_new, v_new, k_cache, v_cache)


def _to_pm(x):
    return jnp.concatenate([x[:, p * LANES:(p + 1) * LANES] for p in range(N_PAIR)], axis=0)


def _from_pm(x):
    r = x.shape[0] // N_PAIR
    return jnp.concatenate([x[p * r:(p + 1) * r] for p in range(N_PAIR)], axis=1)


def _param_pm(v, r):
    if v.shape == (N_PAIR * r, LANES):
        return v
    return jnp.concatenate(
        [jnp.broadcast_to(v[:, p * LANES:(p + 1) * LANES], (r, LANES)) for p in range(N_PAIR)], axis=0)


def _head_sum(x, ones_bd):
    hi = x.astype(BF16)
    out = jnp.dot(hi, ones_bd, preferred_element_type=F32)
    if _PASSES["head_sum"] == 2:
        lo = (x - hi.astype(F32)).astype(BF16)
        out = out + jnp.dot(lo, ones_bd, preferred_element_type=F32)
    return out


def _ones_bd():
    r = lax.broadcasted_iota(jnp.int32, (LANES, LANES), 0) // HD
    c = lax.broadcasted_iota(jnp.int32, (LANES, LANES), 1) // HD
    return jnp.where(r == c, 1.0, 0.0).astype(BF16)


def _softplus(z):
    return jnp.maximum(z, 0.0) + jnp.log(1.0 + jnp.exp(-jnp.abs(z)))


def _rwkv_pre(p_rkv, p_lora, prev_rkv, prev_lora, prm):
    rows = p_rkv.shape[0]
    xs = p_rkv + (prev_rkv - p_rkv) * prm["mu_rkv"]
    xl = p_lora + (prev_lora - p_lora) * prm["mu_lora"]
    wd = xl[:, 0:LORA_W]
    ad = xl[:, LORA_W:LORA_W + LORA_A]
    gd = xl[:, LANES:LANES + G_PAD]
    mm = (((1,), (0,)), ((), ()))
    w_pre = prm["w0"] + _pdot(jnp.tanh(wd), prm["w_up"], mm, _PASSES["lora_w"])
    a_pre = prm["a0"] + _pdot(ad, prm["a_up"], mm, _PASSES["lora_a"])
    g = _pdot(_sigmoid(gd), prm["g_up"], mm, _PASSES["lora_g"])
    logw = -jnp.exp(-_softplus(-w_pre) - 0.5)
    a = _to_pm(_sigmoid(a_pre))
    r = _to_pm(xs[:, 0:D])
    k = _to_pm(xs[:, D:2 * D])
    v = _to_pm(xs[:, 2 * D:3 * D])
    kk = k * _param_pm(prm["k_k"], rows)
    nrm = jnp.sqrt(_head_sum(kk * kk, _ones_bd()))
    kk = kk / jnp.maximum(nrm, 1e-12)
    k = k * (1.0 + (a - 1.0) * _param_pm(prm["k_a"], rows))
    return r, k, v, -kk, kk * a, _to_pm(logw), _to_pm(g)


def _rwkv_post(y, r, k, v, g, prm):
    rows = y.shape[0] // N_PAIR
    ones_bd = _ones_bd()
    mean = _head_sum(y, ones_bd) * (1.0 / HD)
    yc = y - mean
    var = _head_sum(yc * yc, ones_bd) * (1.0 / HD)
    yn = yc * lax.rsqrt(var + LNX_EPS) * _param_pm(prm["lnx_g"], rows) + _param_pm(prm["lnx_b"], rows)
    bonus = _head_sum(r * k * _param_pm(prm["r_k"], rows), ones_bd) * v
    return (yn + bonus) * g


_PRM_NAMES = ("mu_rkv", "mu_lora", "w0", "w_up", "a0", "a_up", "g_up", "k_k", "k_a", "r_k", "lnx_g", "lnx_b")


def _prm_specs(prm, n_grid):
    zero = lambda *_: (0, 0)
    return [pl.BlockSpec(prm[n].shape, zero) for n in _PRM_NAMES]


def _stack2(x, lo):
    return jnp.concatenate([jnp.where(lo, x, 0.0), jnp.where(lo, 0.0, x)], axis=1)


def _split_bf16(x):
    hi = x.astype(BF16)
    return hi, (x - hi.astype(F32)).astype(BF16)


def _pdot(a, b, dims, passes):
    if passes == 6:
        return lax.dot_general(a, b, dims, preferred_element_type=F32, precision=HIGHEST)
    dot = lambda x, y: lax.dot_general(x, y, dims, preferred_element_type=F32)
    if passes == 1:
        return dot(a.astype(BF16), b.astype(BF16))
    a_hi, a_lo = _split_bf16(a)
    b_hi, b_lo = _split_bf16(b)
    return dot(a_hi, b_hi) + (dot(a_hi, b_lo) + dot(a_lo, b_hi))


_PASSES = dict(lora_w=1, lora_a=1, lora_g=1, gram=1, inv=1, rhs=1, u=1, y=1, state=1, head_sum=1)


def _bdot(a, b, ca, cb, site):
    return _pdot(a, b, (((ca,), (cb,)), ((0,), (0,))), _PASSES[site])


def _bmm(a, b, site):
    return _bdot(a, b, 2, 1, site)


def _bmm_nt(a, b, site):
    return _bdot(a, b, 2, 2, site)


def _bmm_tn(a, b, site):
    return _bdot(a, b, 1, 1, site)


def _rwkv_chunk(s, r, k, v, a_, b_, logw):
    c = r.shape[1]
    ti = lax.broadcasted_iota(jnp.int32, (N_PAIR, c, c), 1)
    si = lax.broadcasted_iota(jnp.int32, (N_PAIR, c, c), 2)
    tri = jnp.where(si <= ti, 1.0, 0.0).astype(BF16)
    w_hi, w_lo = _split_bf16(logw)
    cdot = lambda x: lax.dot_general(tri, x, (((2,), (1,)), ((0,), (0,))), preferred_element_type=F32)
    cum = cdot(w_hi) + cdot(w_lo)
    e_neg = jnp.exp(-cum)
    l_end = cum[:, c - 1:c, :]
    e_end = jnp.exp(l_end)
    assert c == HD, "the lane split of (C, 2C) time matrices reuses the head mask"
    lo = lax.broadcasted_iota(jnp.int32, (N_PAIR, c, LANES), 2) < HD
    at = a_ * jnp.exp(cum - logw)
    rt = r * jnp.exp(cum)
    yb, yk = _stack2(b_ * e_neg, lo), _stack2(k * e_neg, lo)
    vs = _stack2(v, lo)
    gmat = _bmm_nt(jnp.concatenate([at, rt], axis=1), jnp.concatenate([yb, yk], axis=1), "gram")
    n2 = 2 * c
    tr = lax.broadcasted_iota(jnp.int32, (N_PAIR, c, n2), 1)
    tc = lax.broadcasted_iota(jnp.int32, (N_PAIR, c, n2), 2) & (c - 1)
    strict, incl = tr > tc, tr >= tc
    a_ab = jnp.where(strict, gmat[:, :c, :n2], 0.0)
    a_ak = jnp.where(strict, gmat[:, :c, n2:], 0.0)
    a_rb = jnp.where(incl, gmat[:, c:, :n2], 0.0)
    a_rk = jnp.where(incl, gmat[:, c:, n2:], 0.0)
    tinv = jnp.where(tr == tc, 1.0, 0.0) + a_ab
    pw = _bmm(a_ab, _stack2(a_ab, lo), "inv")
    d = 4
    while d < c:
        res = _bmm(jnp.concatenate([tinv, pw], axis=1), _stack2(pw, lo), "inv")
        tinv, pw = tinv + res[:, :c], res[:, c:]
        d *= 2
    tinv = tinv + _bmm(tinv, _stack2(pw, lo), "inv")
    rhs = _bmm_nt(at, s, "rhs") + _bmm(a_ak, vs, "rhs")
    us = _stack2(_bmm(tinv, _stack2(rhs, lo), "u"), lo)
    uv = jnp.concatenate([us, vs], axis=1)
    y = _bmm_nt(rt, s, "y") + _bmm(jnp.concatenate([a_rb, a_rk], axis=2), uv, "y")
    s_new = s * e_end + _bmm_tn(uv, jnp.concatenate([yb * e_end, yk * e_end], axis=1), "state")
    return s_new, y


def _rwkv_prompt_kernel(prkv_ref, plora_ref, *rest):
    prm_refs = rest[:len(_PRM_NAMES)]
    o_ref, s_out_ref, s_ref, carry_rkv, carry_lora = rest[len(_PRM_NAMES):]
    ci = pl.program_id(1)
    c = prkv_ref.shape[0]

    @pl.when(ci == 0)
    def _():
        s_ref[...] = jnp.zeros_like(s_ref)
        carry_rkv[...] = jnp.zeros_like(carry_rkv)
        carry_lora[...] = jnp.zeros_like(carry_lora)

    prm = {n: ref[...] for n, ref in zip(_PRM_NAMES, prm_refs)}
    p_rkv, p_lora = prkv_ref[...], plora_ref[...]

    def shifted(x, carry_ref):
        first = lax.broadcasted_iota(jnp.int32, x.shape, 0) == 0
        return jnp.where(first, carry_ref[0:1, :], pltpu.roll(x, 1, 0))

    prev_rkv = shifted(p_rkv, carry_rkv)
    prev_lora = shifted(p_lora, carry_lora)
    carry_rkv[0:1, :] = p_rkv[c - 1:c, :]
    carry_lora[0:1, :] = p_lora[c - 1:c, :]

    r, k, v, a_, b_, logw, g = _rwkv_pre(p_rkv, p_lora, prev_rkv, prev_lora, prm)
    sh = (N_PAIR, c, LANES)
    s_new, y = _rwkv_chunk(s_ref[...], r.reshape(sh), k.reshape(sh), v.reshape(sh),
                           a_.reshape(sh), b_.reshape(sh), logw.reshape(sh))
    s_ref[...] = s_new
    out = _rwkv_post(y.reshape(N_PAIR * c, LANES), r, k, v, g, prm)
    o_ref[...] = _from_pm(out).astype(o_ref.dtype)

    @pl.when(ci == pl.num_programs(1) - 1)
    def _():
        s_out_ref[0] = s_new


def _rwkv_prompt(p_all, prm, batch, seq):
    c = RWKV_CHUNK
    nc = seq // c
    row = lambda b, i: b * nc + i
    return pl.pallas_call(
        _rwkv_prompt_kernel,
        out_shape=(jax.ShapeDtypeStruct((batch * seq, D), BF16),
                   jax.ShapeDtypeStruct((batch, N_PAIR, LANES, LANES), F32)),
        grid=(batch, nc),
        in_specs=[pl.BlockSpec((c, 3 * D), lambda b, i: (row(b, i), C_RKV // (3 * D))),
                  pl.BlockSpec((c, LORA_PAD), lambda b, i: (row(b, i), C_LORA // LORA_PAD))]
                 + _prm_specs(prm, 2),
        out_specs=(pl.BlockSpec((c, D), lambda b, i: (row(b, i), 0)),
                   pl.BlockSpec((1, N_PAIR, LANES, LANES), lambda b, i: (b, 0, 0, 0))),
        scratch_shapes=[pltpu.VMEM((N_PAIR, LANES, LANES), F32),
                        pltpu.VMEM((8, 3 * D), F32),
                        pltpu.VMEM((8, LORA_PAD), F32)],
        compiler_params=_cparams(("parallel", "arbitrary")),
        name="rwkv_prompt",
    )(p_all, p_all, *[prm[n] for n in _PRM_NAMES])


def _rwkv_sample_pre_kernel(prkv_ref, plora_ref, qrkv_ref, qlora_ref, srkv_ref, slora_ref, *rest):
    prm_refs = rest[:len(_PRM_NAMES)]
    outs = rest[len(_PRM_NAMES):]
    t = pl.program_id(0)
    prm = {n: ref[...] for n, ref in zip(_PRM_NAMES, prm_refs)}
    first = t == 0
    prev_rkv = jnp.where(first, srkv_ref[...], qrkv_ref[...])
    prev_lora = jnp.where(first, slora_ref[...], qlora_ref[...])
    r, k, v, a_, b_, logw, g = _rwkv_pre(prkv_ref[...], plora_ref[...], prev_rkv, prev_lora, prm)
    for ref, val in zip(outs, (r, jnp.exp(logw), k, v, a_, b_, g)):
        ref[...] = _from_pm(val)


def _rwkv_sample_pre(p_all, row0, shift_rkv, shift_lora, prm, bsz, tn):
    base = row0 // bsz
    cur = lambda t: base + t
    prv = lambda t: base + jnp.maximum(t - 1, 0)
    out = jax.ShapeDtypeStruct((tn * bsz, D), F32)
    return pl.pallas_call(
        _rwkv_sample_pre_kernel,
        out_shape=(out,) * 7,
        grid=(tn,),
        in_specs=[pl.BlockSpec((bsz, 3 * D), lambda t: (cur(t), C_RKV // (3 * D))),
                  pl.BlockSpec((bsz, LORA_PAD), lambda t: (cur(t), C_LORA // LORA_PAD)),
                  pl.BlockSpec((bsz, 3 * D), lambda t: (prv(t), C_RKV // (3 * D))),
                  pl.BlockSpec((bsz, LORA_PAD), lambda t: (prv(t), C_LORA // LORA_PAD)),
                  pl.BlockSpec((bsz, 3 * D), lambda t: (0, 0)),
                  pl.BlockSpec((bsz, LORA_PAD), lambda t: (0, 0))]
                 + _prm_specs(prm, 1),
        out_specs=tuple(pl.BlockSpec((bsz, D), lambda t: (t, 0)) for _ in range(7)),
        compiler_params=_cparams(("arbitrary",)),
        name="rwkv_sample_pre",
    )(p_all, p_all, p_all, p_all, shift_rkv, shift_lora, *[prm[n] for n in _PRM_NAMES])


def _rwkv_sample_scan_kernel(s0_ref, r_ref, w_ref, k_ref, v_ref, a_ref, b_ref, y_ref, so_ref, s_ref):
    bsz = s0_ref.shape[-1]
    tn = r_ref.shape[1] // bsz
    s_ref[...] = s0_ref[0]
    for t in range(tn):
        cols = slice(t * bsz, (t + 1) * bsz)
        r_t, w_t, k_t = r_ref[:, cols], w_ref[:, cols], k_ref[:, cols]
        a_t, b_t = a_ref[:, cols], b_ref[:, cols]

        def body(i, _):
            s_i = s_ref[i]
            sa = jnp.sum(s_i * a_t, axis=0, keepdims=True)
            v_i = v_ref[i, :, cols]
            s_i = s_i * w_t + sa * b_t + v_i * k_t
            s_ref[i] = s_i
            y_ref[i, :, cols] = jnp.sum(s_i * r_t, axis=0, keepdims=True)
            return 0

        lax.fori_loop(0, HD, body, 0)
    so_ref[0] = s_ref[...]


def _rwkv_sample_scan(s0, r, w, k, v, a_, b_):
    nh, _, _, bsz = s0.shape
    tb = r.shape[1]
    vec = pl.BlockSpec((HD, tb), lambda h: (h, 0))
    vec3 = pl.BlockSpec((HD, 1, tb), lambda h: (h, 0, 0))
    st = pl.BlockSpec((1, HD, HD, bsz), lambda h: (h, 0, 0, 0))
    return pl.pallas_call(
        _rwkv_sample_scan_kernel,
        out_shape=(jax.ShapeDtypeStruct((nh * HD, 1, tb), F32), jax.ShapeDtypeStruct(s0.shape, F32)),
        grid=(nh,),
        in_specs=[st, vec, vec, vec, vec3, vec, vec],
        out_specs=(vec3, st),
        scratch_shapes=[pltpu.VMEM((HD, HD, bsz), F32)],
        compiler_params=_cparams(("parallel",)),
        name="rwkv_sample_scan",
    )(s0, r, w, k, v, a_, b_)


def _rwkv_sample_post_kernel(y_ref, r_ref, k_ref, v_ref, g_ref, *rest):
    prm_refs = rest[:len(_PRM_NAMES)]
    o_ref = rest[len(_PRM_NAMES)]
    prm = {n: ref[...] for n, ref in zip(_PRM_NAMES, prm_refs)}
    out = _rwkv_post(_to_pm(y_ref[...]), _to_pm(r_ref[...]), _to_pm(k_ref[...]),
                     _to_pm(v_ref[...]), _to_pm(g_ref[...]), prm)
    o_ref[...] = _from_pm(out).astype(o_ref.dtype)


def _rwkv_sample_post(y, r, k, v, g, prm, bsz):
    n = y.shape[0]
    blk = pl.BlockSpec((bsz, D), lambda t: (t, 0))
    return pl.pallas_call(
        _rwkv_sample_post_kernel,
        out_shape=jax.ShapeDtypeStruct((n, D), BF16),
        grid=(n // bsz,),
        in_specs=[blk] * 5 + _prm_specs(prm, 1),
        out_specs=blk,
        compiler_params=_cparams(("parallel",)),
        name="rwkv_sample_post",
    )(y, r, k, v, g, *[prm[n] for n in _PRM_NAMES])


N_CHUNK = D // LANES


def _store_token_major(ref, x, lead=()):
    rows = x.shape[0]
    for c in range(N_CHUNK):
        ref[lead + (pl.ds(c, rows, stride=N_CHUNK), slice(None))] = x[:, c * LANES:(c + 1) * LANES]


def _load_token_major(ref, rows, lead=()):
    return jnp.concatenate(
        [ref[lead + (pl.ds(c, rows, stride=N_CHUNK), slice(None))] for c in range(N_CHUNK)], axis=1)


def _outproj_kernel(attp_ref, atts_ref, rwp_ref, rws_ref, ga_ref, gr_ref, xp_ref, xs_ref, w_ref, g2_ref, *rest,
                    n_experts, n_prompt_tiles):
    is_prompt = pl.program_id(0) < n_prompt_tiles
    att = jnp.where(is_prompt, attp_ref[...], atts_ref[...]).astype(F32)
    rw = jnp.where(is_prompt, rwp_ref[...], rws_ref[...]).astype(F32)
    x = jnp.where(is_prompt, xp_ref[...], xs_ref[...])
    m = _sigmoid(ga_ref[...]) * att + _sigmoid(gr_ref[...]) * rw
    xn = x + jnp.dot(m.astype(BF16), w_ref[...], preferred_element_type=F32)
    h2 = _rms(xn, g2_ref[...])
    if n_experts:
        router_ref, xo_ref, h2_ref, gate_ref = rest
        logits = jnp.dot(h2, router_ref[...], preferred_element_type=F32, precision=HIGHEST)
        lane = lax.broadcasted_iota(jnp.int32, logits.shape, 1).astype(F32)
        lg = jnp.where(lane < n_experts, logits, -jnp.inf)
        v1 = jnp.max(lg, -1, keepdims=True)
        i1 = jnp.min(jnp.where(lg == v1, lane, float(LANES)), -1, keepdims=True)
        lg2 = jnp.where(lane == i1, -jnp.inf, lg)
        v2 = jnp.max(lg2, -1, keepdims=True)
        i2 = jnp.min(jnp.where(lg2 == v2, lane, float(LANES)), -1, keepdims=True)
        e2 = jnp.exp(v2 - v1)
        den = 1.0 + e2
        gate_ref[...] = (jnp.where(lane == 0.0, i1, 0.0) + jnp.where(lane == 1.0, i2, 0.0)
                         + jnp.where(lane == 2.0, 1.0 / den, 0.0) + jnp.where(lane == 3.0, e2 / den, 0.0))
        _store_token_major(h2_ref, h2)
    else:
        xo_ref, h2_ref = rest
        h2_ref[...] = h2.astype(BF16)
    xo_ref[...] = xn


def _outproj(att_p, att_s, rw_p, rw_s, p_all, x_p, x_s, s_base, w_out, g2, router, n_experts, tm):
    n = att_p.shape[0] + att_s.shape[0]
    npt = att_p.shape[0] // tm
    row = lambda w_: pl.BlockSpec((tm, w_), lambda i: (i, 0))
    p_spec, s_spec = _two_part_specs((tm, D), npt, 0)
    xp_spec, xs_spec = _two_part_specs((tm, D), npt, s_base)
    in_specs = [p_spec, s_spec, p_spec, s_spec,
                pl.BlockSpec((tm, D), lambda i: (i, C_GATE // D)),
                pl.BlockSpec((tm, D), lambda i: (i, C_GATE // D + 1)),
                xp_spec, xs_spec,
                pl.BlockSpec((D, D), lambda i: (0, 0)),
                pl.BlockSpec((1, D), lambda i: (0, 0))]
    args = [att_p, att_s, rw_p, rw_s, p_all, p_all, x_p, x_s, w_out, g2]
    if n_experts:
        h2_shape = jax.ShapeDtypeStruct((n * N_CHUNK, LANES), F32)
        h2_spec = pl.BlockSpec((tm * N_CHUNK, LANES), lambda i: (i, 0))
    else:
        h2_shape, h2_spec = jax.ShapeDtypeStruct((n, D), BF16), row(D)
    out_shape = [jax.ShapeDtypeStruct((n, D), F32), h2_shape]
    out_specs = [row(D), h2_spec]
    if n_experts:
        in_specs.append(pl.BlockSpec((D, LANES), lambda i: (0, 0)))
        args.append(router)
        out_shape.append(jax.ShapeDtypeStruct((n, LANES), F32))
        out_specs.append(row(LANES))
    return pl.pallas_call(
        functools.partial(_outproj_kernel, n_experts=n_experts, n_prompt_tiles=npt),
        out_shape=tuple(out_shape),
        grid=(n // tm,),
        in_specs=in_specs,
        out_specs=tuple(out_specs),
        compiler_params=_cparams(("parallel",)),
        name="outproj",
    )(*args)


def _swiglu_part(h, wg, wu, wd):
    a = jnp.dot(h, wg, preferred_element_type=F32)
    b = jnp.dot(h, wu, preferred_element_type=F32)
    t = (a * _sigmoid(a) * b).astype(BF16)
    return jnp.dot(t, wd, preferred_element_type=F32)


def _final_store(y, gf_ref, out_refs, n_prompt_tiles):
    if gf_ref is None:
        (o_ref,) = out_refs
        o_ref[...] = y
        return
    op_ref, os_ref = out_refs
    y = _rms(y, gf_ref[...])
    is_prompt = pl.program_id(0) < n_prompt_tiles

    @pl.when(is_prompt)
    def _():
        op_ref[...] = y

    @pl.when(jnp.logical_not(is_prompt))
    def _():
        os_ref[...] = y


def _final_out(n, n_p, tm, final):
    if not final:
        return jax.ShapeDtypeStruct((n, D), F32), pl.BlockSpec((tm, D), lambda i, *_: (i, 0))
    return ((jax.ShapeDtypeStruct((n_p, D), F32), jax.ShapeDtypeStruct((n - n_p, D), F32)),
            _two_part_specs((tm, D), n_p // tm, 0))


def _ffn_kernel(h_ref, wg_ref, wu_ref, wd_ref, x_ref, *rest, final, n_prompt_tiles):
    rest = list(rest)
    gf_ref = rest.pop(0) if final else None
    acc_ref = rest.pop()
    f = pl.program_id(1)

    @pl.when(f == 0)
    def _():
        acc_ref[...] = jnp.zeros_like(acc_ref)

    acc_ref[...] += _swiglu_part(h_ref[...], wg_ref[...], wu_ref[...], wd_ref[...])

    @pl.when(f == pl.num_programs(1) - 1)
    def _():
        _final_store(x_ref[...] + acc_ref[...], gf_ref, rest, n_prompt_tiles)


def _ffn(h, wg, wu, wd, x, gf, n_p, tm, tf):
    n = x.shape[0]
    fdim = wg.shape[1]
    final = gf is not None
    in_specs = [pl.BlockSpec((tm, D), lambda i, f: (i, 0)),
                pl.BlockSpec((D, tf), lambda i, f: (0, f)),
                pl.BlockSpec((D, tf), lambda i, f: (0, f)),
                pl.BlockSpec((tf, D), lambda i, f: (f, 0)),
                pl.BlockSpec((tm, D), lambda i, f: (i, 0))]
    args = [h, wg, wu, wd, x]
    if final:
        in_specs.append(pl.BlockSpec((1, D), lambda i, f: (0, 0)))
        args.append(gf)
    out_shape, out_specs = _final_out(n, n_p, tm, final)
    return pl.pallas_call(
        functools.partial(_ffn_kernel, final=final, n_prompt_tiles=n_p // tm),
        out_shape=out_shape,
        grid=(n // tm, fdim // tf),
        in_specs=in_specs,
        out_specs=out_specs,
        scratch_shapes=[pltpu.VMEM((tm, D), F32)],
        compiler_params=_cparams(("arbitrary" if final else "parallel", "arbitrary")),
        name="ffn",
    )(*args)


MOE_TM = 512


def _moe_plan(route, n_experts):
    n = route.shape[0]
    e_flat = route[:, :2].astype(jnp.int32).T.reshape(-1)
    n_asg = e_flat.shape[0]
    order = jnp.argsort(e_flat, stable=True).astype(jnp.int32)
    counts = jnp.sum((e_flat[:, None] == jnp.arange(n_experts)[None, :]).astype(jnp.int32), axis=0)
    first = jnp.cumsum(counts) - counts
    padded = (counts + MOE_TM - 1) // MOE_TM * MOE_TM
    ends = jnp.cumsum(padded)
    offs = ends - padded
    n_tiles = -(-n_asg // MOE_TM) + n_experts
    start = jnp.arange(n_tiles, dtype=jnp.int32) * MOE_TM
    n_active = ends[-1] // MOE_TM
    tile_e = jnp.sum((start[:, None] >= ends[None, :]).astype(jnp.int32), axis=1)
    last_e = jnp.sum(((n_active - 1) * MOE_TM >= ends).astype(jnp.int32))
    tile_e = jnp.minimum(tile_e, last_e)
    n_valid = jnp.clip(counts[tile_e] - (start - offs[tile_e]), 0, MOE_TM)
    n_valid = jnp.where(start < ends[-1], n_valid, 0)
    r_in_tile = jnp.arange(MOE_TM, dtype=jnp.int32)[None, :]
    src = (first[tile_e] + start - offs[tile_e])[:, None] + r_in_tile
    real = r_in_tile < n_valid[:, None]
    dst = jnp.where(real, order[jnp.where(real, src, 0)], 0).reshape(-1)
    tok = jnp.where(dst >= n, dst - n, dst)
    return (tile_e.astype(jnp.int32), n_valid.astype(jnp.int32), n_active.reshape(1).astype(jnp.int32),
            tok.astype(jnp.int32), dst.astype(jnp.int32))


def _moe_kernel(te_ref, nv_ref, na_ref, tok_ref, dst_ref, h_hbm, wg_ref, wu_ref, wd_ref, o_hbm,
                xbuf, xb, acc, stage, gsem, ssem, *, nf):
    del te_ref
    j, f = pl.program_id(0), pl.program_id(1)
    n_tiles = pl.num_programs(0)
    n_active = na_ref[0]
    active = j < n_active
    slot = j % 2
    other = 1 - slot
    tile_rows = MOE_TM * N_CHUNK
    dump0 = o_hbm.shape[0] // N_CHUNK - MOE_TM
    rows_per_step = MOE_TM // nf
    token = lambda t: pl.ds(pl.multiple_of(t * N_CHUNK, N_CHUNK), N_CHUNK)

    def gather_row(tile, slot_, r):
        tok = tok_ref[tile * MOE_TM + r]
        return pltpu.make_async_copy(h_hbm.at[token(tok)], xbuf.at[slot_, token(r)], gsem.at[slot_])

    def gather_wait(slot_):
        pltpu.make_async_copy(h_hbm.at[pl.ds(0, tile_rows)], xbuf.at[slot_], gsem.at[slot_]).wait()

    def scatter_row(tile, slot_, r, real):
        row = jnp.where(real & (r < nv_ref[tile]), dst_ref[tile * MOE_TM + r], dump0 + r)
        return pltpu.make_async_copy(stage.at[slot_, token(r)], o_hbm.at[token(row)], ssem.at[slot_])

    def scatter_wait(slot_):
        pltpu.make_async_copy(stage.at[slot_], o_hbm.at[pl.ds(0, tile_rows)], ssem.at[slot_]).wait()

    @pl.when(active & (f == 0))
    def _():
        @pl.when(j == 0)
        def _():
            def body(r, _):
                gather_row(0, 0, r).start()
                return 0
            lax.fori_loop(0, MOE_TM, body, 0, unroll=SUBLANES)
            stage[1] = jnp.zeros((tile_rows, LANES), F32)

        gather_wait(slot)
        xb[...] = _load_token_major(xbuf, MOE_TM, (slot,)).astype(BF16)

        @pl.when(j > 0)
        def _():
            scatter_wait(slot)

    @pl.when(active)
    def _():
        part = _swiglu_part(xb[...], wg_ref[0], wu_ref[0], wd_ref[0])
        nxt = jnp.minimum(j + 1, n_tiles - 1)
        prev = jnp.maximum(j - 1, 0)
        for u in range(rows_per_step):
            r = f * rows_per_step + u
            gather_row(nxt, other, r).start()
            scatter_row(prev, other, r, j > 0).start()

        if nf > 1:
            @pl.when(f == 0)
            def _():
                acc[...] = part

            @pl.when((f > 0) & (f < nf - 1))
            def _():
                acc[...] += part

        @pl.when(f == nf - 1)
        def _():
            _store_token_major(stage, acc[...] + part if nf > 1 else part, (slot,))

        @pl.when((f == nf - 1) & (j == n_active - 1))
        def _():
            gather_wait(other)
            scatter_wait(other)

            def body(r, _):
                scatter_row(j, slot, r, True).start()
                return 0
            lax.fori_loop(0, MOE_TM, body, 0, unroll=SUBLANES)
            scatter_wait(slot)


def _moe_experts(h, plan, wg, wu, wd, tf):
    tile_e, n_valid, n_active, tok, dst = plan
    n = h.shape[0] // N_CHUNK
    fdim = wg.shape[2]
    n_tiles = tile_e.shape[0]
    nf = fdim // tf

    def wmap(j, f, te, nv, na, tok_, dst_):
        return te[j], jnp.where(j < na[0], f, nf - 1)

    grid_spec = pltpu.PrefetchScalarGridSpec(
        num_scalar_prefetch=5,
        grid=(n_tiles, nf),
        in_specs=[pl.BlockSpec(memory_space=pl.ANY),
                  pl.BlockSpec((1, D, tf), lambda *a: (wmap(*a)[0], 0, wmap(*a)[1])),
                  pl.BlockSpec((1, D, tf), lambda *a: (wmap(*a)[0], 0, wmap(*a)[1])),
                  pl.BlockSpec((1, tf, D), lambda *a: (wmap(*a)[0], wmap(*a)[1], 0))],
        out_specs=pl.BlockSpec(memory_space=pl.ANY),
        scratch_shapes=[pltpu.VMEM((2, MOE_TM * N_CHUNK, LANES), F32),
                        pltpu.VMEM((MOE_TM, D), BF16),
                        pltpu.VMEM((MOE_TM, D), F32),
                        pltpu.VMEM((2, MOE_TM * N_CHUNK, LANES), F32),
                        pltpu.SemaphoreType.DMA((2,)),
                        pltpu.SemaphoreType.DMA((2,))],
    )
    return pl.pallas_call(
        functools.partial(_moe_kernel, nf=nf),
        out_shape=jax.ShapeDtypeStruct(((2 * n + MOE_TM) * N_CHUNK, LANES), F32),
        grid_spec=grid_spec,
        compiler_params=_cparams(("arbitrary", "arbitrary"), disable_bounds_checks=True),
        name="moe_experts",
    )(tile_e, n_valid, n_active, tok, dst, h, wg, wu, wd)


def _moe_combine_kernel(x_ref, o1_ref, o2_ref, route_ref, *rest, final, n_prompt_tiles):
    rest = list(rest)
    gf_ref = rest.pop(0) if final else None
    route = route_ref[...]
    lane = lax.broadcasted_iota(jnp.int32, route.shape, 1)
    w1 = jnp.sum(jnp.where(lane == 2, route, 0.0), -1, keepdims=True)
    w2 = jnp.sum(jnp.where(lane == 3, route, 0.0), -1, keepdims=True)
    rows = x_ref.shape[0]
    o1, o2 = _load_token_major(o1_ref, rows), _load_token_major(o2_ref, rows)
    _final_store(x_ref[...] + (w1 * o1 + w2 * o2), gf_ref, rest, n_prompt_tiles)


def _moe_combine(x, o, route, gf, n_p, tm):
    n = x.shape[0]
    nb = n // tm
    final = gf is not None
    in_specs = [pl.BlockSpec((tm, D), lambda i: (i, 0)),
                pl.BlockSpec((tm * N_CHUNK, LANES), lambda i: (i, 0)),
                pl.BlockSpec((tm * N_CHUNK, LANES), lambda i: (i + nb, 0)),
                pl.BlockSpec((tm, LANES), lambda i: (i, 0))]
    args = [x, o, o, route]
    if final:
        in_specs.append(pl.BlockSpec((1, D), lambda i: (0, 0)))
        args.append(gf)
    out_shape, out_specs = _final_out(n, n_p, tm, final)
    return pl.pallas_call(
        functools.partial(_moe_combine_kernel, final=final, n_prompt_tiles=n_p // tm),
        out_shape=out_shape,
        grid=(nb,),
        in_specs=in_specs,
        out_specs=out_specs,
        compiler_params=_cparams(("arbitrary",)),
        name="moe_combine",
    )(*args)


def _split_shift_cols(a):
    pad = jnp.zeros(a.shape[:-1] + (LORA_PAD - (LORA_W + LORA_A + LORA_G),), a.dtype)
    lead = a[..., 3 * D:3 * D + LORA_W + LORA_A]
    gd = a[..., 3 * D + LORA_W + LORA_A:]
    return a[..., :3 * D], jnp.concatenate([lead, gd, pad], -1)


def _relayout_w_in(w):
    q, k, v = w[:, 0:D], w[:, D:D + 256], w[:, D + 256:D + 512]
    pr = w[:, D + 512:D + 512 + 3360]
    gates = w[:, D + 512 + 3360:]
    rkv, lora = _split_shift_cols(pr)
    return jnp.concatenate([rkv, q, gates, k, v, lora], axis=1).astype(BF16)


def _rope_tables(pos):
    inv = ROPE_THETA ** (-jnp.arange(0, ROT, 2, dtype=F32) / ROT)
    ang = pos.astype(F32)[:, None] * inv[None, :]
    cos, sin = jnp.cos(ang), jnp.sin(ang)
    n = pos.shape[0]
    half = ROT // 2
    one = jnp.ones((n, HD - ROT), F32)
    zero = jnp.zeros((n, HD - half), F32)
    c = jnp.concatenate([cos, cos, one], 1)
    sa = jnp.concatenate([-sin, zero], 1)
    sb = jnp.concatenate([jnp.zeros((n, half), F32), sin, jnp.zeros((n, HD - ROT), F32)], 1)
    tile = lambda a: jnp.concatenate([a, a], 1)
    return tile(c), tile(sa), tile(sb)


def _pair_state_to_heads(s):
    even = s[:, :, :HD, :HD]
    odd = s[:, :, HD:, HD:]
    b = s.shape[0]
    return jnp.stack([even, odd], axis=2).reshape(b, 2 * N_PAIR, HD, HD)


def kernel(x_prompt, x_sample, cache_k_win, cache_v_win, state_wkv, state_shift, norm_mix_g, w_in, w_out, attn_sinks, shift_mu, decay_w0, decay_up, iclr_a0, iclr_up, gate_up, key_kk, key_ka, bonus_rk, lnx_g, lnx_b, norm_ffn_g, ffn_w_gate, ffn_w_up, ffn_w_down, moe_router, moe_w_gate, moe_w_up, moe_w_down, norm_final_g):
    batch, seq, _ = x_prompt.shape
    sb, st, _ = x_sample.shape
    depth = w_in.shape[0]
    n_p, n_s = batch * seq, sb * st
    n = n_p + n_s
    n_buf = cache_k_win.shape[2]
    if depth == 0:
        raise ValueError("depth must be positive")
    tm = next(c for c in (512, 256, 128) if n_p % c == 0 and n_s % c == 0 and seq % c == 0)
    npt = n_p // tm
    tm_in = 2 * tm if n_p % (2 * tm) == 0 and seq % (2 * tm) == 0 else tm
    n_in = -(-n // tm_in) * tm_in
    pad_rows = lambda a: jnp.pad(a, ((0, n_in - n_p - a.shape[0]), (0, 0)))

    x_p, x_s, s_base = x_prompt.reshape(n_p, D), pad_rows(x_sample.transpose(1, 0, 2).reshape(n_s, D)), 0
    rope_p = _rope_tables(jnp.arange(seq))
    rope_s = tuple(pad_rows(t) for t in _rope_tables(jnp.repeat(PAST_LEN + jnp.arange(st), sb)))
    row = lambda a: a.reshape(1, -1)

    new_p, new_s = [], []
    for l in range(depth):
        mu_rkv, mu_lora = _split_shift_cols(row(shift_mu[l]))
        g_up = jnp.concatenate([gate_up[l], jnp.zeros((G_PAD - LORA_G, D), F32)], 0)
        prm = dict(mu_rkv=mu_rkv, mu_lora=mu_lora, w0=row(decay_w0[l]), w_up=decay_up[l],
                   a0=row(iclr_a0[l]), a_up=iclr_up[l], g_up=g_up, k_k=row(key_kk[l]),
                   k_a=row(key_ka[l]), r_k=row(bonus_rk[l]), lnx_g=row(lnx_g[l]), lnx_b=row(lnx_b[l]))
        expand = lambda rows: {**prm, **{k_: _param_pm(prm[k_], rows) for k_ in ("k_k", "k_a", "r_k", "lnx_g", "lnx_b")}}
        prm_p, prm_s = expand(RWKV_CHUNK), expand(sb)
        p_all = _inproj(x_p, x_s, n_p // tm_in if l else 0, n_p // tm_in, n_in,
                        row(norm_mix_g[l]), _relayout_w_in(w_in[l]), rope_p, rope_s, seq, tm_in)

        att_p = _attn_prompt(p_all, attn_sinks[l], batch, seq)
        ps = p_all[n_p:n].reshape(st, sb, N_COLS).transpose(1, 0, 2)
        k_new, v_new = ps[..., C_K:C_K + 256], ps[..., C_V:C_V + 256]
        k_cache = cache_k_win[l].reshape(sb, n_buf, 256)
        v_cache = cache_v_win[l].reshape(sb, n_buf, 256)
        att_s = _attn_sample(ps[..., C_Q:C_Q + D], k_new, v_new, k_cache, v_cache, attn_sinks[l])
        att_s = pad_rows(att_s.transpose(1, 0, 2).reshape(n_s, D))

        rw_p, s_pairs = _rwkv_prompt(p_all, prm_p, batch, seq)
        sh_rkv, sh_lora = _split_shift_cols(state_shift[l])
        r_s, w_s, k_s, v_s, a_s, b_s, g_s = _rwkv_sample_pre(p_all, n_p, sh_rkv, sh_lora, prm_s, sb, st)
        s0 = state_wkv[l].transpose(1, 2, 3, 0)
        y_t, s_fin = _rwkv_sample_scan(s0, r_s.T, w_s.T, k_s.T, v_s.T[:, None, :], a_s.T, b_s.T)
        rw_s = pad_rows(_rwkv_sample_post(y_t[:, 0, :].T, r_s, k_s, v_s, g_s, prm_s, sb))

        is_moe = l % 2 == 1
        last = l == depth - 1
        gf = row(norm_final_g) if last else None
        if is_moe:
            ne = moe_router.shape[-1]
            router = jnp.concatenate([moe_router[l // 2], jnp.zeros((D, LANES - ne), F32)], 1)
            x_mid, h2, route = _outproj(att_p, att_s, rw_p, rw_s, p_all, x_p, x_s, s_base, w_out[l].astype(BF16),
                                        row(norm_ffn_g[l]), router, ne, tm)
            o_exp = _moe_experts(h2, _moe_plan(route, ne), moe_w_gate[l // 2].astype(BF16),
                                 moe_w_up[l // 2].astype(BF16), moe_w_down[l // 2].astype(BF16),
                                 _pick(moe_w_gate.shape[-1], (896, 512, 256, 128)))
            x = _moe_combine(x_mid, o_exp, route, gf, n_p, tm)
        else:
            x_mid, h2 = _outproj(att_p, att_s, rw_p, rw_s, p_all, x_p, x_s, s_base, w_out[l].astype(BF16),
                                 row(norm_ffn_g[l]), None, 0, tm)
            x = _ffn(h2, ffn_w_gate[l // 2].astype(BF16), ffn_w_up[l // 2].astype(BF16),
                     ffn_w_down[l // 2].astype(BF16), x_mid, gf, n_p, tm,
                     _pick(ffn_w_gate.shape[-1], (1408, 512, 256, 128)))
        if not last:
            x_p, x_s, s_base = x, x, npt

        n_win = min(WINDOW, seq)
        tail = lambda rows, c0, w_: jnp.stack(
            [lax.slice(p_all, ((b + 1) * seq - rows, c0), ((b + 1) * seq, c0 + w_)) for b in range(batch)])
        k_p = tail(n_win, C_K, N_KV * HD).reshape(batch, n_win, N_KV, HD)
        v_p = tail(n_win, C_V, N_KV * HD).reshape(batch, n_win, N_KV, HD)
        unsplit = lambda a: jnp.concatenate(
            [a[..., C_RKV:C_RKV + 3 * D], a[..., C_LORA:C_LORA + LORA_W + LORA_A + LORA_G]], -1)
        last_p = jnp.concatenate([tail(1, C_RKV, 3 * D), tail(1, C_LORA, LORA_W + LORA_A + LORA_G)], -1)[:, 0]
        new_p.append((k_p, v_p, _pair_state_to_heads(s_pairs), last_p))
        k_s_win = jnp.concatenate([k_cache, k_new], 1)[:, -n_buf:].reshape(sb, n_buf, N_KV, HD)
        v_s_win = jnp.concatenate([v_cache, v_new], 1)[:, -n_buf:].reshape(sb, n_buf, N_KV, HD)
        new_s.append((k_s_win, v_s_win, s_fin.transpose(3, 0, 1, 2), unsplit(ps[:, -1])))

    y_p, y_s = x
    y_p = y_p.reshape(batch, seq, D)
    y_s = y_s[:n_s].reshape(st, sb, D).transpose(1, 0, 2)
    stk = lambda sts, i: jnp.stack([s[i] for s in sts])
    return (y_p, y_s,
            stk(new_p, 0), stk(new_p, 1), stk(new_p, 2), stk(new_p, 3),
            stk(new_s, 0), stk(new_s, 1), stk(new_s, 2), stk(new_s, 3))
```

```python
import functools

import jax
import jax.numpy as jnp
from jax import lax
from jax.experimental import pallas as pl
from jax.experimental.pallas import tpu as pltpu

F32 = jnp.float32
BF16 = jnp.bfloat16
HIGHEST = lax.Precision.HIGHEST

LANES = 128
SUBLANES = 8
VMEM_LIMIT = 56 * 1024 * 1024

D = 1024
HD = 64
N_Q = 16
N_KV = 4
ROT = 16
ROPE_THETA = 500000.0
WINDOW = 128
RMS_EPS = 1e-5
LNX_EPS = 64e-5
N_PAIR = D // LANES
LORA_W, LORA_A, LORA_G = 64, 64, 160
LORA_PAD = 512
G_PAD = 256

C_RKV = 0
C_Q = 3072
C_GATE = 4096
C_K = 6144
C_V = 6400
C_LORA = 6656
N_COLS = 7168
TN_IN = 1024
Q_TILE = C_Q // TN_IN
KV_TILE = C_K // TN_IN

RWKV_CHUNK = 64


PAST_LEN = 16384


def _pick(n, cands):
    return next(c for c in cands if n % c == 0)


def _cparams(sem, **kw):
    return pltpu.CompilerParams(dimension_semantics=sem, vmem_limit_bytes=VMEM_LIMIT, **kw)


def _rms(x, g):
    return x * lax.rsqrt(jnp.mean(x * x, -1, keepdims=True) + RMS_EPS) * g


def _sigmoid(x):
    return 1.0 / (1.0 + jnp.exp(-x))


def _rope_chunk(a, c, sa, sb):
    return a * c + pltpu.roll(a, LANES - ROT // 2, 1) * sa + pltpu.roll(a, ROT // 2, 1) * sb


def _inproj_kernel(xp_ref, xs_ref, g_ref, w_ref, cp_ref, sap_ref, sbp_ref, cs_ref, sas_ref, sbs_ref,
                   o_ref, h_ref, *, n_prompt_tiles):
    i, j = pl.program_id(0), pl.program_id(1)
    is_prompt = i < n_prompt_tiles

    @pl.when(j == 0)
    def _():
        x = jnp.where(is_prompt, xp_ref[...], xs_ref[...])
        h_ref[...] = _rms(x, g_ref[...]).astype(BF16)

    acc = jnp.dot(h_ref[...], w_ref[...], preferred_element_type=F32)

    def roped(n_chunks):
        c = jnp.where(is_prompt, cp_ref[...], cs_ref[...])
        sa = jnp.where(is_prompt, sap_ref[...], sas_ref[...])
        sb = jnp.where(is_prompt, sbp_ref[...], sbs_ref[...])
        parts = [_rope_chunk(acc[:, k * LANES:(k + 1) * LANES], c, sa, sb) for k in range(n_chunks)]
        if n_chunks * LANES < TN_IN:
            parts.append(acc[:, n_chunks * LANES:])
        return jnp.concatenate(parts, axis=1)

    @pl.when(j == Q_TILE)
    def _():
        o_ref[...] = roped(TN_IN // LANES)

    @pl.when(j == KV_TILE)
    def _():
        o_ref[...] = roped(N_KV * HD // LANES)

    @pl.when((j != Q_TILE) & (j != KV_TILE))
    def _():
        o_ref[...] = acc


def _two_part_specs(block, npt, s_base, period=None):
    def p_map(i, *_):
        ip = jnp.minimum(i, npt - 1)
        return (ip % period if period else ip, 0)

    def s_map(i, *_):
        return (s_base + jnp.maximum(i - npt, 0), 0)

    return pl.BlockSpec(block, p_map), pl.BlockSpec(block, s_map)


def _inproj(x_p, x_s, s_base, npt, n, g, w, rope_p, rope_s, seq, tm):
    xp_spec, xs_spec = _two_part_specs((tm, D), npt, s_base)
    rp_spec, rs_spec = _two_part_specs((tm, LANES), npt, 0, period=seq // tm)
    return pl.pallas_call(
        functools.partial(_inproj_kernel, n_prompt_tiles=npt),
        out_shape=jax.ShapeDtypeStruct((n, N_COLS), F32),
        grid=(n // tm, N_COLS // TN_IN),
        in_specs=[xp_spec, xs_spec,
                  pl.BlockSpec((1, D), lambda i, j: (0, 0)),
                  pl.BlockSpec((D, TN_IN), lambda i, j: (0, j)),
                  rp_spec, rp_spec, rp_spec, rs_spec, rs_spec, rs_spec],
        out_specs=pl.BlockSpec((tm, TN_IN), lambda i, j: (i, j)),
        scratch_shapes=[pltpu.VMEM((tm, D), BF16)],
        compiler_params=_cparams(("parallel", "arbitrary")),
        name="inproj",
    )(x_p, x_s, g, w, *rope_p, *rope_s)


def _sink_softmax(s, mask, sink):
    s = jnp.where(mask, s * (HD ** -0.5), -jnp.inf)
    m = jnp.maximum(jnp.max(s, -1, keepdims=True), sink)
    p = jnp.exp(s - m)
    return p / (jnp.sum(p, -1, keepdims=True) + jnp.exp(sink - m))


def _dot_nt(a, b, **kw):
    return lax.dot_general(a, b, (((1,), (1,)), ((), ())), preferred_element_type=F32, **kw)


def _dot_tn(a, b, **kw):
    return lax.dot_general(a, b, (((0,), (0,)), ((), ())), preferred_element_type=F32, **kw)


def _head_operands(x, half):
    lo = lax.broadcasted_iota(jnp.int32, x.shape, 1) < HD
    sw = pltpu.roll(x, HD, 1)
    x_lo, x_hi = (x, sw) if half == 0 else (sw, x)
    return jnp.where(lo, x_lo, 0.0).astype(BF16), jnp.where(lo, 0.0, x_hi).astype(BF16)


def _group_queries(q, g):
    return jnp.concatenate([q[:, (2 * g) * LANES:(2 * g + 1) * LANES],
                            q[:, (2 * g + 1) * LANES:(2 * g + 2) * LANES]], axis=0)


def _group_sinks(sink_ref, g, top):
    return (jnp.where(top, sink_ref[4 * g], sink_ref[4 * g + 2]),
            jnp.where(top, sink_ref[4 * g + 1], sink_ref[4 * g + 3]))


def _attn_prompt_kernel(sink_ref, q_ref, kp_ref, kc_ref, vp_ref, vc_ref, o_ref):
    blk = pl.program_id(1)
    w = WINDOW
    q = q_ref[...].astype(BF16)
    k = jnp.concatenate([kp_ref[...], kc_ref[...]], axis=0)
    v = jnp.concatenate([vp_ref[...], vc_ref[...]], axis=0)
    qi = lax.broadcasted_iota(jnp.int32, (w, w), 0)
    kj = lax.broadcasted_iota(jnp.int32, (w, w), 1)
    band = jnp.concatenate([kj <= qi] * 2, axis=0)
    mask = band | (blk > 0)
    top = lax.broadcasted_iota(jnp.int32, (2 * w, 1), 0) < w
    outs = []
    for g in range(N_KV):
        ch, half = divmod(g, 2)
        q2 = _group_queries(q, g)
        o = 0.0
        for k_, v_, sink in zip(_head_operands(k[:, ch * LANES:(ch + 1) * LANES], half),
                                _head_operands(v[:, ch * LANES:(ch + 1) * LANES], half),
                                _group_sinks(sink_ref, g, top)):
            s = _dot_nt(q2, k_)
            p = _sink_softmax(jnp.where(band, s[:, w:], s[:, :w]), mask, sink)
            p_cat = jnp.concatenate([jnp.where(band, 0.0, p), jnp.where(band, p, 0.0)], axis=1)
            o = o + jnp.dot(p_cat.astype(BF16), v_, preferred_element_type=F32)
        outs += [o[:w], o[w:]]
    o_ref[...] = jnp.concatenate(outs, axis=1).astype(o_ref.dtype)


def _attn_prompt(p_all, sinks, batch, seq):
    nb = seq // WINDOW
    kcol, vcol = C_K // (N_KV * HD), C_V // (N_KV * HD)
    cur = lambda b, i: b * nb + i
    prev = lambda b, i: b * nb + jnp.maximum(i - 1, 0)
    return pl.pallas_call(
        _attn_prompt_kernel,
        out_shape=jax.ShapeDtypeStruct((batch * seq, D), BF16),
        grid=(batch, nb),
        in_specs=[
            pl.BlockSpec(memory_space=pltpu.SMEM),
            pl.BlockSpec((WINDOW, D), lambda b, i: (cur(b, i), C_Q // D)),
            pl.BlockSpec((WINDOW, N_KV * HD), lambda b, i: (prev(b, i), kcol)),
            pl.BlockSpec((WINDOW, N_KV * HD), lambda b, i: (cur(b, i), kcol)),
            pl.BlockSpec((WINDOW, N_KV * HD), lambda b, i: (prev(b, i), vcol)),
            pl.BlockSpec((WINDOW, N_KV * HD), lambda b, i: (cur(b, i), vcol)),
        ],
        out_specs=pl.BlockSpec((WINDOW, D), lambda b, i: (cur(b, i), 0)),
        compiler_params=_cparams(("parallel", "arbitrary")),
        name="attn_prompt",
    )(sinks, p_all, p_all, p_all, p_all, p_all)


ATT_S_BT = 8
T_PAD = 8


def _attn_sample_kernel(sink_ref, q_ref, kn_ref, vn_ref, kc_ref, vc_ref, o_ref):
    tn = q_ref.shape[1]
    nbuf = kc_ref.shape[1]
    rows = 2 * tn
    keys = nbuf + T_PAD
    r = lax.broadcasted_iota(jnp.int32, (rows, keys), 0)
    t = jnp.where(r >= tn, r - tn, r)
    kj = lax.broadcasted_iota(jnp.int32, (rows, keys), 1)
    mask = (kj > t + (nbuf - WINDOW)) & (kj <= t + nbuf)
    top = lax.broadcasted_iota(jnp.int32, (rows, 1), 0) < tn
    zpad = jnp.zeros((T_PAD - tn, N_KV * HD), F32)
    scores, sinks, values = [], [], []
    for b in range(ATT_S_BT):
        q = q_ref[b].astype(BF16)
        k = jnp.concatenate([kc_ref[b], kn_ref[b], zpad], axis=0)
        v = jnp.concatenate([vc_ref[b], vn_ref[b], zpad], axis=0)
        for g in range(N_KV):
            ch, half = divmod(g, 2)
            q2 = _group_queries(q, g)
            values += list(_head_operands(v[:, ch * LANES:(ch + 1) * LANES], half))
            sinks += list(_group_sinks(sink_ref, g, top))
            scores += [_dot_nt(q2, k_) for k_ in _head_operands(k[:, ch * LANES:(ch + 1) * LANES], half)]
    p = _sink_softmax(jnp.stack(scores), mask[None], jnp.stack(sinks)).astype(BF16)
    for b in range(ATT_S_BT):
        outs = []
        for g in range(N_KV):
            i = 2 * (b * N_KV + g)
            o = (jnp.dot(p[i], values[i], preferred_element_type=F32)
                 + jnp.dot(p[i + 1], values[i + 1], preferred_element_type=F32))
            outs += [o[:tn], o[tn:]]
        o_ref[b] = jnp.concatenate(outs, axis=1).astype(o_ref.dtype)


def _attn_sample(q, k_new, v_new, k_cache, v_cache, sinks):
    bsz, tn, _ = q.shape
    nbuf = k_cache.shape[1]
    kvw = N_KV * HD
    blk = lambda w_, r_: pl.BlockSpec((ATT_S_BT, r_, w_), lambda i: (i, 0, 0))
    return pl.pallas_call(
        _attn_sample_kernel,
        out_shape=jax.ShapeDtypeStruct((bsz, tn, D), BF16),
        grid=(bsz // ATT_S_BT,),
        in_specs=[pl.BlockSpec(memory_space=pltpu.SMEM),
                  blk(D, tn), blk(kvw, tn), blk(kvw, tn), blk(kvw, nbuf), blk(kvw, nbuf)],
        out_specs=blk(D, tn),
        compiler_params=_cparams(("parallel",)),
        name="attn_sample",
    )(sinks, q, k_new, v_new, k_cache, v_cache)


def _to_pm(x):
    return jnp.concatenate([x[:, p * LANES:(p + 1) * LANES] for p in range(N_PAIR)], axis=0)


def _from_pm(x):
    r = x.shape[0] // N_PAIR
    return jnp.concatenate([x[p * r:(p + 1) * r] for p in range(N_PAIR)], axis=1)


def _param_pm(v, r):
    if v.shape == (N_PAIR * r, LANES):
        return v
    return jnp.concatenate(
        [jnp.broadcast_to(v[:, p * LANES:(p + 1) * LANES], (r, LANES)) for p in range(N_PAIR)], axis=0)


def _head_sum(x, ones_bd):
    hi = x.astype(BF16)
    out = jnp.dot(hi, ones_bd, preferred_element_type=F32)
    if _PASSES["head_sum"] == 2:
        lo = (x - hi.astype(F32)).astype(BF16)
        out = out + jnp.dot(lo, ones_bd, preferred_element_type=F32)
    return out


def _ones_bd():
    r = lax.broadcasted_iota(jnp.int32, (LANES, LANES), 0) // HD
    c = lax.broadcasted_iota(jnp.int32, (LANES, LANES), 1) // HD
    return jnp.where(r == c, 1.0, 0.0).astype(BF16)


def _softplus(z):
    return jnp.maximum(z, 0.0) + jnp.log(1.0 + jnp.exp(-jnp.abs(z)))


def _rwkv_pre(p_rkv, p_lora, prev_rkv, prev_lora, prm):
    rows = p_rkv.shape[0]
    xs = p_rkv + (prev_rkv - p_rkv) * prm["mu_rkv"]
    xl = p_lora + (prev_lora - p_lora) * prm["mu_lora"]
    wd = xl[:, 0:LORA_W]
    ad = xl[:, LORA_W:LORA_W + LORA_A]
    gd = xl[:, LANES:LANES + G_PAD]
    mm = (((1,), (0,)), ((), ()))
    w_pre = prm["w0"] + _pdot(jnp.tanh(wd), prm["w_up"], mm, _PASSES["lora_w"])
    a_pre = prm["a0"] + _pdot(ad, prm["a_up"], mm, _PASSES["lora_a"])
    g = _pdot(_sigmoid(gd), prm["g_up"], mm, _PASSES["lora_g"])
    logw = -jnp.exp(-_softplus(-w_pre) - 0.5)
    a = _to_pm(_sigmoid(a_pre))
    r = _to_pm(xs[:, 0:D])
    k = _to_pm(xs[:, D:2 * D])
    v = _to_pm(xs[:, 2 * D:3 * D])
    kk = k * _param_pm(prm["k_k"], rows)
    nrm = jnp.sqrt(_head_sum(kk * kk, _ones_bd()))
    kk = kk / jnp.maximum(nrm, 1e-12)
    k = k * (1.0 + (a - 1.0) * _param_pm(prm["k_a"], rows))
    return r, k, v, -kk, kk * a, _to_pm(logw), _to_pm(g)


def _rwkv_post(y, r, k, v, g, prm):
    rows = y.shape[0] // N_PAIR
    ones_bd = _ones_bd()
    mean = _head_sum(y, ones_bd) * (1.0 / HD)
    yc = y - mean
    var = _head_sum(yc * yc, ones_bd) * (1.0 / HD)
    yn = yc * lax.rsqrt(var + LNX_EPS) * _param_pm(prm["lnx_g"], rows) + _param_pm(prm["lnx_b"], rows)
    bonus = _head_sum(r * k * _param_pm(prm["r_k"], rows), ones_bd) * v
    return (yn + bonus) * g


_PRM_NAMES = ("mu_rkv", "mu_lora", "w0", "w_up", "a0", "a_up", "g_up", "k_k", "k_a", "r_k", "lnx_g", "lnx_b")


def _prm_specs(prm, n_grid):
    zero = lambda *_: (0, 0)
    return [pl.BlockSpec(prm[n].shape, zero) for n in _PRM_NAMES]


def _stack2(x, lo):
    return jnp.concatenate([jnp.where(lo, x, 0.0), jnp.where(lo, 0.0, x)], axis=1)


def _split_bf16(x):
    hi = x.astype(BF16)
    return hi, (x - hi.astype(F32)).astype(BF16)


def _pdot(a, b, dims, passes):
    if passes == 6:
        return lax.dot_general(a, b, dims, preferred_element_type=F32, precision=HIGHEST)
    dot = lambda x, y: lax.dot_general(x, y, dims, preferred_element_type=F32)
    if passes == 1:
        return dot(a.astype(BF16), b.astype(BF16))
    a_hi, a_lo = _split_bf16(a)
    b_hi, b_lo = _split_bf16(b)
    return dot(a_hi, b_hi) + (dot(a_hi, b_lo) + dot(a_lo, b_hi))


_PASSES = dict(lora_w=1, lora_a=1, lora_g=1, gram=1, inv=1, rhs=1, u=1, y=1, state=1, head_sum=1)


def _bdot(a, b, ca, cb, site):
    return _pdot(a, b, (((ca,), (cb,)), ((0,), (0,))), _PASSES[site])


def _bmm(a, b, site):
    return _bdot(a, b, 2, 1, site)


def _bmm_nt(a, b, site):
    return _bdot(a, b, 2, 2, site)


def _bmm_tn(a, b, site):
    return _bdot(a, b, 1, 1, site)


def _rwkv_chunk(s, r, k, v, a_, b_, logw):
    c = r.shape[1]
    ti = lax.broadcasted_iota(jnp.int32, (N_PAIR, c, c), 1)
    si = lax.broadcasted_iota(jnp.int32, (N_PAIR, c, c), 2)
    tri = jnp.where(si <= ti, 1.0, 0.0).astype(BF16)
    w_hi, w_lo = _split_bf16(logw)
    cdot = lambda x: lax.dot_general(tri, x, (((2,), (1,)), ((0,), (0,))), preferred_element_type=F32)
    cum = cdot(w_hi) + cdot(w_lo)
    e_neg = jnp.exp(-cum)
    l_end = cum[:, c - 1:c, :]
    e_end = jnp.exp(l_end)
    assert c == HD, "the lane split of (C, 2C) time matrices reuses the head mask"
    lo = lax.broadcasted_iota(jnp.int32, (N_PAIR, c, LANES), 2) < HD
    at = a_ * jnp.exp(cum - logw)
    rt = r * jnp.exp(cum)
    yb, yk = _stack2(b_ * e_neg, lo), _stack2(k * e_neg, lo)
    vs = _stack2(v, lo)
    gmat = _bmm_nt(jnp.concatenate([at, rt], axis=1), jnp.concatenate([yb, yk], axis=1), "gram")
    n2 = 2 * c
    tr = lax.broadcasted_iota(jnp.int32, (N_PAIR, c, n2), 1)
    tc = lax.broadcasted_iota(jnp.int32, (N_PAIR, c, n2), 2) & (c - 1)
    strict, incl = tr > tc, tr >= tc
    a_ab = jnp.where(strict, gmat[:, :c, :n2], 0.0)
    a_ak = jnp.where(strict, gmat[:, :c, n2:], 0.0)
    a_rb = jnp.where(incl, gmat[:, c:, :n2], 0.0)
    a_rk = jnp.where(incl, gmat[:, c:, n2:], 0.0)
    tinv = jnp.where(tr == tc, 1.0, 0.0) + a_ab
    pw = _bmm(a_ab, _stack2(a_ab, lo), "inv")
    d = 4
    while d < c:
        res = _bmm(jnp.concatenate([tinv, pw], axis=1), _stack2(pw, lo), "inv")
        tinv, pw = tinv + res[:, :c], res[:, c:]
        d *= 2
    tinv = tinv + _bmm(tinv, _stack2(pw, lo), "inv")
    rhs = _bmm_nt(at, s, "rhs") + _bmm(a_ak, vs, "rhs")
    us = _stack2(_bmm(tinv, _stack2(rhs, lo), "u"), lo)
    uv = jnp.concatenate([us, vs], axis=1)
    y = _bmm_nt(rt, s, "y") + _bmm(jnp.concatenate([a_rb, a_rk], axis=2), uv, "y")
    s_new = s * e_end + _bmm_tn(uv, jnp.concatenate([yb * e_end, yk * e_end], axis=1), "state")
    return s_new, y


def _rwkv_prompt_kernel(prkv_ref, plora_ref, *rest):
    prm_refs = rest[:len(_PRM_NAMES)]
    o_ref, s_out_ref, s_ref, carry_rkv, carry_lora = rest[len(_PRM_NAMES):]
    ci = pl.program_id(1)
    c = RWKV_CHUNK

    @pl.when(ci == 0)
    def _():
        s_ref[...] = jnp.zeros_like(s_ref)
        carry_rkv[...] = jnp.zeros_like(carry_rkv)
        carry_lora[...] = jnp.zeros_like(carry_lora)

    prm = {n: ref[...] for n, ref in zip(_PRM_NAMES, prm_refs)}

    def shifted(x, last):
        first = lax.broadcasted_iota(jnp.int32, x.shape, 0) == 0
        return jnp.where(first, last, pltpu.roll(x, 1, 0))

    s = s_ref[...]
    last_rkv, last_lora = carry_rkv[0:1, :], carry_lora[0:1, :]
    sh = (N_PAIR, c, LANES)
    for i in range(prkv_ref.shape[0] // c):
        rows = slice(i * c, (i + 1) * c)
        p_rkv, p_lora = prkv_ref[rows, :], plora_ref[rows, :]
        r, k, v, a_, b_, logw, g = _rwkv_pre(p_rkv, p_lora, shifted(p_rkv, last_rkv),
                                             shifted(p_lora, last_lora), prm)
        last_rkv, last_lora = p_rkv[c - 1:c, :], p_lora[c - 1:c, :]
        s, y = _rwkv_chunk(s, r.reshape(sh), k.reshape(sh), v.reshape(sh),
                           a_.reshape(sh), b_.reshape(sh), logw.reshape(sh))
        out = _rwkv_post(y.reshape(N_PAIR * c, LANES), r, k, v, g, prm)
        o_ref[rows, :] = _from_pm(out).astype(o_ref.dtype)
    s_ref[...] = s
    carry_rkv[0:1, :] = last_rkv
    carry_lora[0:1, :] = last_lora

    @pl.when(ci == pl.num_programs(1) - 1)
    def _():
        s_out_ref[0] = s


RWKV_BLOCK = 2 * RWKV_CHUNK


def _rwkv_prompt(p_all, prm, batch, seq):
    c = RWKV_BLOCK if seq % RWKV_BLOCK == 0 else RWKV_CHUNK
    nc = seq // c
    row = lambda b, i: b * nc + i
    return pl.pallas_call(
        _rwkv_prompt_kernel,
        out_shape=(jax.ShapeDtypeStruct((batch * seq, D), BF16),
                   jax.ShapeDtypeStruct((batch, N_PAIR, LANES, LANES), F32)),
        grid=(batch, nc),
        in_specs=[pl.BlockSpec((c, 3 * D), lambda b, i: (row(b, i), C_RKV // (3 * D))),
                  pl.BlockSpec((c, LORA_PAD), lambda b, i: (row(b, i), C_LORA // LORA_PAD))]
                 + _prm_specs(prm, 2),
        out_specs=(pl.BlockSpec((c, D), lambda b, i: (row(b, i), 0)),
                   pl.BlockSpec((1, N_PAIR, LANES, LANES), lambda b, i: (b, 0, 0, 0))),
        scratch_shapes=[pltpu.VMEM((N_PAIR, LANES, LANES), F32),
                        pltpu.VMEM((8, 3 * D), F32),
                        pltpu.VMEM((8, LORA_PAD), F32)],
        compiler_params=_cparams(("parallel", "arbitrary")),
        name="rwkv_prompt",
    )(p_all, p_all, *[prm[n] for n in _PRM_NAMES])


def _rwkv_sample_pre_kernel(prkv_ref, plora_ref, qrkv_ref, qlora_ref, srkv_ref, slora_ref, *rest):
    prm_refs = rest[:len(_PRM_NAMES)]
    outs = rest[len(_PRM_NAMES):]
    t = pl.program_id(0)
    prm = {n: ref[...] for n, ref in zip(_PRM_NAMES, prm_refs)}
    first = t == 0
    prev_rkv = jnp.where(first, srkv_ref[...], qrkv_ref[...])
    prev_lora = jnp.where(first, slora_ref[...], qlora_ref[...])
    r, k, v, a_, b_, logw, g = _rwkv_pre(prkv_ref[...], plora_ref[...], prev_rkv, prev_lora, prm)
    for ref, val in zip(outs, (r, jnp.exp(logw), k, v, a_, b_, g)):
        ref[...] = _from_pm(val)


def _rwkv_sample_pre(p_all, row0, shift_rkv, shift_lora, prm, bsz, tn):
    base = row0 // bsz
    cur = lambda t: base + t
    prv = lambda t: base + jnp.maximum(t - 1, 0)
    out = jax.ShapeDtypeStruct((tn * bsz, D), F32)
    return pl.pallas_call(
        _rwkv_sample_pre_kernel,
        out_shape=(out,) * 7,
        grid=(tn,),
        in_specs=[pl.BlockSpec((bsz, 3 * D), lambda t: (cur(t), C_RKV // (3 * D))),
                  pl.BlockSpec((bsz, LORA_PAD), lambda t: (cur(t), C_LORA // LORA_PAD)),
                  pl.BlockSpec((bsz, 3 * D), lambda t: (prv(t), C_RKV // (3 * D))),
                  pl.BlockSpec((bsz, LORA_PAD), lambda t: (prv(t), C_LORA // LORA_PAD)),
                  pl.BlockSpec((bsz, 3 * D), lambda t: (0, 0)),
                  pl.BlockSpec((bsz, LORA_PAD), lambda t: (0, 0))]
                 + _prm_specs(prm, 1),
        out_specs=tuple(pl.BlockSpec((bsz, D), lambda t: (t, 0)) for _ in range(7)),
        compiler_params=_cparams(("arbitrary",)),
        name="rwkv_sample_pre",
    )(p_all, p_all, p_all, p_all, shift_rkv, shift_lora, *[prm[n] for n in _PRM_NAMES])


def _rwkv_sample_scan_kernel(s0_ref, r_ref, w_ref, k_ref, v_ref, a_ref, b_ref, y_ref, so_ref, s_ref):
    bsz = s0_ref.shape[-1]
    tn = r_ref.shape[1] // bsz
    s_ref[...] = s0_ref[0]
    for t in range(tn):
        cols = slice(t * bsz, (t + 1) * bsz)
        r_t, w_t, k_t = r_ref[:, cols], w_ref[:, cols], k_ref[:, cols]
        a_t, b_t = a_ref[:, cols], b_ref[:, cols]

        def body(i, _):
            s_i = s_ref[i]
            sa = jnp.sum(s_i * a_t, axis=0, keepdims=True)
            v_i = v_ref[i, :, cols]
            s_i = s_i * w_t + sa * b_t + v_i * k_t
            s_ref[i] = s_i
            y_ref[i, :, cols] = jnp.sum(s_i * r_t, axis=0, keepdims=True)
            return 0

        lax.fori_loop(0, HD, body, 0)
    so_ref[0] = s_ref[...]


def _rwkv_sample_scan(s0, r, w, k, v, a_, b_):
    nh, _, _, bsz = s0.shape
    tb = r.shape[1]
    vec = pl.BlockSpec((HD, tb), lambda h: (h, 0))
    vec3 = pl.BlockSpec((HD, 1, tb), lambda h: (h, 0, 0))
    st = pl.BlockSpec((1, HD, HD, bsz), lambda h: (h, 0, 0, 0))
    return pl.pallas_call(
        _rwkv_sample_scan_kernel,
        out_shape=(jax.ShapeDtypeStruct((nh * HD, 1, tb), F32), jax.ShapeDtypeStruct(s0.shape, F32)),
        grid=(nh,),
        in_specs=[st, vec, vec, vec, vec3, vec, vec],
        out_specs=(vec3, st),
        scratch_shapes=[pltpu.VMEM((HD, HD, bsz), F32)],
        compiler_params=_cparams(("parallel",)),
        name="rwkv_sample_scan",
    )(s0, r, w, k, v, a_, b_)


def _rwkv_sample_post_kernel(y_ref, r_ref, k_ref, v_ref, g_ref, *rest):
    prm_refs = rest[:len(_PRM_NAMES)]
    o_ref = rest[len(_PRM_NAMES)]
    prm = {n: ref[...] for n, ref in zip(_PRM_NAMES, prm_refs)}
    out = _rwkv_post(_to_pm(y_ref[...]), _to_pm(r_ref[...]), _to_pm(k_ref[...]),
                     _to_pm(v_ref[...]), _to_pm(g_ref[...]), prm)
    o_ref[...] = _from_pm(out).astype(o_ref.dtype)


def _rwkv_sample_post(y, r, k, v, g, prm, bsz):
    n = y.shape[0]
    blk = pl.BlockSpec((bsz, D), lambda t: (t, 0))
    return pl.pallas_call(
        _rwkv_sample_post_kernel,
        out_shape=jax.ShapeDtypeStruct((n, D), BF16),
        grid=(n // bsz,),
        in_specs=[blk] * 5 + _prm_specs(prm, 1),
        out_specs=blk,
        compiler_params=_cparams(("parallel",)),
        name="rwkv_sample_post",
    )(y, r, k, v, g, *[prm[n] for n in _PRM_NAMES])


N_CHUNK = D // LANES


def _store_token_major(ref, x, lead=()):
    rows = x.shape[0]
    for c in range(N_CHUNK):
        ref[lead + (pl.ds(c, rows, stride=N_CHUNK), slice(None))] = x[:, c * LANES:(c + 1) * LANES]


def _load_token_major(ref, rows, lead=()):
    return jnp.concatenate(
        [ref[lead + (pl.ds(c, rows, stride=N_CHUNK), slice(None))] for c in range(N_CHUNK)], axis=1)


def _outproj_kernel(attp_ref, atts_ref, rwp_ref, rws_ref, ga_ref, gr_ref, xp_ref, xs_ref, w_ref, g2_ref, *rest,
                    n_experts, n_prompt_tiles):
    is_prompt = pl.program_id(0) < n_prompt_tiles
    att = jnp.where(is_prompt, attp_ref[...], atts_ref[...]).astype(F32)
    rw = jnp.where(is_prompt, rwp_ref[...], rws_ref[...]).astype(F32)
    x = jnp.where(is_prompt, xp_ref[...], xs_ref[...])
    m = _sigmoid(ga_ref[...]) * att + _sigmoid(gr_ref[...]) * rw
    xn = x + jnp.dot(m.astype(BF16), w_ref[...], preferred_element_type=F32)
    h2 = _rms(xn, g2_ref[...])
    if n_experts:
        router_ref, xo_ref, h2_ref, gate_ref = rest
        logits = _pdot(h2, router_ref[...], (((1,), (0,)), ((), ())), 3)
        lane = lax.broadcasted_iota(jnp.int32, logits.shape, 1).astype(F32)
        lg = jnp.where(lane < n_experts, logits, -jnp.inf)
        v1 = jnp.max(lg, -1, keepdims=True)
        i1 = jnp.min(jnp.where(lg == v1, lane, float(LANES)), -1, keepdims=True)
        lg2 = jnp.where(lane == i1, -jnp.inf, lg)
        v2 = jnp.max(lg2, -1, keepdims=True)
        i2 = jnp.min(jnp.where(lg2 == v2, lane, float(LANES)), -1, keepdims=True)
        e2 = jnp.exp(v2 - v1)
        den = 1.0 + e2
        gate_ref[...] = (jnp.where(lane == 0.0, i1, 0.0) + jnp.where(lane == 1.0, i2, 0.0)
                         + jnp.where(lane == 2.0, 1.0 / den, 0.0) + jnp.where(lane == 3.0, e2 / den, 0.0))
        _store_token_major(h2_ref, h2)
    else:
        xo_ref, h2_ref = rest
        h2_ref[...] = h2.astype(BF16)
    xo_ref[...] = xn


def _outproj(att_p, att_s, rw_p, rw_s, p_all, x_p, x_s, s_base, w_out, g2, router, n_experts, tm):
    n = att_p.shape[0] + att_s.shape[0]
    npt = att_p.shape[0] // tm
    row = lambda w_: pl.BlockSpec((tm, w_), lambda i: (i, 0))
    p_spec, s_spec = _two_part_specs((tm, D), npt, 0)
    xp_spec, xs_spec = _two_part_specs((tm, D), npt, s_base)
    in_specs = [p_spec, s_spec, p_spec, s_spec,
                pl.BlockSpec((tm, D), lambda i: (i, C_GATE // D)),
                pl.BlockSpec((tm, D), lambda i: (i, C_GATE // D + 1)),
                xp_spec, xs_spec,
                pl.BlockSpec((D, D), lambda i: (0, 0)),
                pl.BlockSpec((1, D), lambda i: (0, 0))]
    args = [att_p, att_s, rw_p, rw_s, p_all, p_all, x_p, x_s, w_out, g2]
    if n_experts:
        h2_shape = jax.ShapeDtypeStruct((n * N_CHUNK, LANES), F32)
        h2_spec = pl.BlockSpec((tm * N_CHUNK, LANES), lambda i: (i, 0))
    else:
        h2_shape, h2_spec = jax.ShapeDtypeStruct((n, D), BF16), row(D)
    out_shape = [jax.ShapeDtypeStruct((n, D), F32), h2_shape]
    out_specs = [row(D), h2_spec]
    if n_experts:
        in_specs.append(pl.BlockSpec((D, LANES), lambda i: (0, 0)))
        args.append(router)
        out_shape.append(jax.ShapeDtypeStruct((n, LANES), F32))
        out_specs.append(row(LANES))
    return pl.pallas_call(
        functools.partial(_outproj_kernel, n_experts=n_experts, n_prompt_tiles=npt),
        out_shape=tuple(out_shape),
        grid=(n // tm,),
        in_specs=in_specs,
        out_specs=tuple(out_specs),
        compiler_params=_cparams(("parallel",)),
        name="outproj",
    )(*args)


def _swiglu_part(h, wg, wu, wd):
    a = jnp.dot(h, wg, preferred_element_type=F32)
    b = jnp.dot(h, wu, preferred_element_type=F32)
    t = (a * _sigmoid(a) * b).astype(BF16)
    return jnp.dot(t, wd, preferred_element_type=F32)


def _final_store(y, gf_ref, out_refs, n_prompt_tiles):
    if gf_ref is None:
        (o_ref,) = out_refs
        o_ref[...] = y
        return
    op_ref, os_ref = out_refs
    y = _rms(y, gf_ref[...])
    is_prompt = pl.program_id(0) < n_prompt_tiles

    @pl.when(is_prompt)
    def _():
        op_ref[...] = y

    @pl.when(jnp.logical_not(is_prompt))
    def _():
        os_ref[...] = y


def _final_out(n, n_p, tm, final):
    if not final:
        return jax.ShapeDtypeStruct((n, D), F32), pl.BlockSpec((tm, D), lambda i, *_: (i, 0))
    return ((jax.ShapeDtypeStruct((n_p, D), F32), jax.ShapeDtypeStruct((n - n_p, D), F32)),
            _two_part_specs((tm, D), n_p // tm, 0))


def _ffn_kernel(h_ref, wg_ref, wu_ref, wd_ref, x_ref, *rest, final, n_prompt_tiles):
    rest = list(rest)
    gf_ref = rest.pop(0) if final else None
    acc_ref = rest.pop()
    f = pl.program_id(1)

    @pl.when(f == 0)
    def _():
        acc_ref[...] = jnp.zeros_like(acc_ref)

    acc_ref[...] += _swiglu_part(h_ref[...], wg_ref[...], wu_ref[...], wd_ref[...])

    @pl.when(f == pl.num_programs(1) - 1)
    def _():
        _final_store(x_ref[...] + acc_ref[...], gf_ref, rest, n_prompt_tiles)


def _ffn(h, wg, wu, wd, x, gf, n_p, tm, tf):
    n = x.shape[0]
    fdim = wg.shape[1]
    final = gf is not None
    in_specs = [pl.BlockSpec((tm, D), lambda i, f: (i, 0)),
                pl.BlockSpec((D, tf), lambda i, f: (0, f)),
                pl.BlockSpec((D, tf), lambda i, f: (0, f)),
                pl.BlockSpec((tf, D), lambda i, f: (f, 0)),
                pl.BlockSpec((tm, D), lambda i, f: (i, 0))]
    args = [h, wg, wu, wd, x]
    if final:
        in_specs.append(pl.BlockSpec((1, D), lambda i, f: (0, 0)))
        args.append(gf)
    out_shape, out_specs = _final_out(n, n_p, tm, final)
    return pl.pallas_call(
        functools.partial(_ffn_kernel, final=final, n_prompt_tiles=n_p // tm),
        out_shape=out_shape,
        grid=(n // tm, fdim // tf),
        in_specs=in_specs,
        out_specs=out_specs,
        scratch_shapes=[pltpu.VMEM((tm, D), F32)],
        compiler_params=_cparams(("arbitrary" if final else "parallel", "arbitrary")),
        name="ffn",
    )(*args)


MOE_TM = 512


def _moe_plan(route, n_experts):
    n = route.shape[0]
    e_flat = route[:, :2].astype(jnp.int32).T.reshape(-1)
    n_asg = e_flat.shape[0]
    order = jnp.argsort(e_flat, stable=True).astype(jnp.int32)
    counts = jnp.sum((e_flat[:, None] == jnp.arange(n_experts)[None, :]).astype(jnp.int32), axis=0)
    first = jnp.cumsum(counts) - counts
    padded = (counts + MOE_TM - 1) // MOE_TM * MOE_TM
    ends = jnp.cumsum(padded)
    offs = ends - padded
    n_tiles = -(-n_asg // MOE_TM) + n_experts
    start = jnp.arange(n_tiles, dtype=jnp.int32) * MOE_TM
    n_active = ends[-1] // MOE_TM
    tile_e = jnp.sum((start[:, None] >= ends[None, :]).astype(jnp.int32), axis=1)
    last_e = jnp.sum(((n_active - 1) * MOE_TM >= ends).astype(jnp.int32))
    tile_e = jnp.minimum(tile_e, last_e)
    n_valid = jnp.clip(counts[tile_e] - (start - offs[tile_e]), 0, MOE_TM)
    n_valid = jnp.where(start < ends[-1], n_valid, 0)
    r_in_tile = jnp.arange(MOE_TM, dtype=jnp.int32)[None, :]
    src = (first[tile_e] + start - offs[tile_e])[:, None] + r_in_tile
    real = r_in_tile < n_valid[:, None]
    dst = jnp.where(real, order[jnp.where(real, src, 0)], 0).reshape(-1)
    tok = jnp.where(dst >= n, dst - n, dst)
    return (tile_e.astype(jnp.int32), n_valid.astype(jnp.int32), n_active.reshape(1).astype(jnp.int32),
            tok.astype(jnp.int32), dst.astype(jnp.int32))


def _moe_kernel(te_ref, nv_ref, na_ref, tok_ref, dst_ref, h_hbm, wg_ref, wu_ref, wd_ref, o_hbm,
                xbuf, xb, acc, stage, gsem, ssem, *, nf):
    del te_ref
    j, f = pl.program_id(0), pl.program_id(1)
    n_tiles = pl.num_programs(0)
    n_active = na_ref[0]
    active = j < n_active
    slot = j % 2
    other = 1 - slot
    tile_rows = MOE_TM * N_CHUNK
    dump0 = o_hbm.shape[0] // N_CHUNK - MOE_TM
    rows_per_step = MOE_TM // nf
    token = lambda t: pl.ds(pl.multiple_of(t * N_CHUNK, N_CHUNK), N_CHUNK)

    def gather_row(tile, slot_, r):
        tok = tok_ref[tile * MOE_TM + r]
        return pltpu.make_async_copy(h_hbm.at[token(tok)], xbuf.at[slot_, token(r)], gsem.at[slot_])

    def gather_wait(slot_):
        pltpu.make_async_copy(h_hbm.at[pl.ds(0, tile_rows)], xbuf.at[slot_], gsem.at[slot_]).wait()

    def scatter_row(tile, slot_, r, real):
        row = jnp.where(real & (r < nv_ref[tile]), dst_ref[tile * MOE_TM + r], dump0 + r)
        return pltpu.make_async_copy(stage.at[slot_, token(r)], o_hbm.at[token(row)], ssem.at[slot_])

    def scatter_wait(slot_):
        pltpu.make_async_copy(stage.at[slot_], o_hbm.at[pl.ds(0, tile_rows)], ssem.at[slot_]).wait()

    @pl.when(active & (f == 0))
    def _():
        @pl.when(j == 0)
        def _():
            def body(r, _):
                gather_row(0, 0, r).start()
                return 0
            lax.fori_loop(0, MOE_TM, body, 0, unroll=SUBLANES)
            stage[1] = jnp.zeros((tile_rows, LANES), F32)

        gather_wait(slot)
        xb[...] = _load_token_major(xbuf, MOE_TM, (slot,)).astype(BF16)

        @pl.when(j > 0)
        def _():
            scatter_wait(slot)

    @pl.when(active)
    def _():
        part = _swiglu_part(xb[...], wg_ref[0], wu_ref[0], wd_ref[0])
        nxt = jnp.minimum(j + 1, n_tiles - 1)
        prev = jnp.maximum(j - 1, 0)
        for u in range(rows_per_step):
            r = f * rows_per_step + u
            gather_row(nxt, other, r).start()
            scatter_row(prev, other, r, j > 0).start()

        if nf > 1:
            @pl.when(f == 0)
            def _():
                acc[...] = part

            @pl.when((f > 0) & (f < nf - 1))
            def _():
                acc[...] += part

        @pl.when(f == nf - 1)
        def _():
            _store_token_major(stage, acc[...] + part if nf > 1 else part, (slot,))

        @pl.when((f == nf - 1) & (j == n_active - 1))
        def _():
            gather_wait(other)
            scatter_wait(other)

            def body(r, _):
                scatter_row(j, slot, r, True).start()
                return 0
            lax.fori_loop(0, MOE_TM, body, 0, unroll=SUBLANES)
            scatter_wait(slot)


def _moe_experts(h, plan, wg, wu, wd, tf):
    tile_e, n_valid, n_active, tok, dst = plan
    n = h.shape[0] // N_CHUNK
    fdim = wg.shape[2]
    n_tiles = tile_e.shape[0]
    nf = fdim // tf

    def wmap(j, f, te, nv, na, tok_, dst_):
        return te[j], jnp.where(j < na[0], f, nf - 1)

    grid_spec = pltpu.PrefetchScalarGridSpec(
        num_scalar_prefetch=5,
        grid=(n_tiles, nf),
        in_specs=[pl.BlockSpec(memory_space=pl.ANY),
                  pl.BlockSpec((1, D, tf), lambda *a: (wmap(*a)[0], 0, wmap(*a)[1])),
                  pl.BlockSpec((1, D, tf), lambda *a: (wmap(*a)[0], 0, wmap(*a)[1])),
                  pl.BlockSpec((1, tf, D), lambda *a: (wmap(*a)[0], wmap(*a)[1], 0))],
        out_specs=pl.BlockSpec(memory_space=pl.ANY),
        scratch_shapes=[pltpu.VMEM((2, MOE_TM * N_CHUNK, LANES), F32),
                        pltpu.VMEM((MOE_TM, D), BF16),
                        pltpu.VMEM((MOE_TM, D), F32),
                        pltpu.VMEM((2, MOE_TM * N_CHUNK, LANES), F32),
                        pltpu.SemaphoreType.DMA((2,)),
                        pltpu.SemaphoreType.DMA((2,))],
    )
    return pl.pallas_call(
        functools.partial(_moe_kernel, nf=nf),
        out_shape=jax.ShapeDtypeStruct(((2 * n + MOE_TM) * N_CHUNK, LANES), F32),
        grid_spec=grid_spec,
        compiler_params=_cparams(("arbitrary", "arbitrary"), disable_bounds_checks=True),
        name="moe_experts",
    )(tile_e, n_valid, n_active, tok, dst, h, wg, wu, wd)


def _moe_combine_kernel(x_ref, o1_ref, o2_ref, route_ref, *rest, final, n_prompt_tiles):
    rest = list(rest)
    gf_ref = rest.pop(0) if final else None
    route = route_ref[...]
    lane = lax.broadcasted_iota(jnp.int32, route.shape, 1)
    w1 = jnp.sum(jnp.where(lane == 2, route, 0.0), -1, keepdims=True)
    w2 = jnp.sum(jnp.where(lane == 3, route, 0.0), -1, keepdims=True)
    rows = x_ref.shape[0]
    o1, o2 = _load_token_major(o1_ref, rows), _load_token_major(o2_ref, rows)
    _final_store(x_ref[...] + (w1 * o1 + w2 * o2), gf_ref, rest, n_prompt_tiles)


def _moe_combine(x, o, route, gf, n_p, tm):
    n = x.shape[0]
    nb = n // tm
    final = gf is not None
    in_specs = [pl.BlockSpec((tm, D), lambda i: (i, 0)),
                pl.BlockSpec((tm * N_CHUNK, LANES), lambda i: (i, 0)),
                pl.BlockSpec((tm * N_CHUNK, LANES), lambda i: (i + nb, 0)),
                pl.BlockSpec((tm, LANES), lambda i: (i, 0))]
    args = [x, o, o, route]
    if final:
        in_specs.append(pl.BlockSpec((1, D), lambda i: (0, 0)))
        args.append(gf)
    out_shape, out_specs = _final_out(n, n_p, tm, final)
    return pl.pallas_call(
        functools.partial(_moe_combine_kernel, final=final, n_prompt_tiles=n_p // tm),
        out_shape=out_shape,
        grid=(nb,),
        in_specs=in_specs,
        out_specs=out_specs,
        compiler_params=_cparams(("arbitrary",)),
        name="moe_combine",
    )(*args)


def _split_shift_cols(a):
    pad = jnp.zeros(a.shape[:-1] + (LORA_PAD - (LORA_W + LORA_A + LORA_G),), a.dtype)
    lead = a[..., 3 * D:3 * D + LORA_W + LORA_A]
    gd = a[..., 3 * D + LORA_W + LORA_A:]
    return a[..., :3 * D], jnp.concatenate([lead, gd, pad], -1)


def _relayout_w_in(w):
    q, k, v = w[:, 0:D], w[:, D:D + 256], w[:, D + 256:D + 512]
    pr = w[:, D + 512:D + 512 + 3360]
    gates = w[:, D + 512 + 3360:]
    rkv, lora = _split_shift_cols(pr)
    return jnp.concatenate([rkv, q, gates, k, v, lora], axis=1).astype(BF16)


def _rope_tables(pos):
    inv = ROPE_THETA ** (-jnp.arange(0, ROT, 2, dtype=F32) / ROT)
    ang = pos.astype(F32)[:, None] * inv[None, :]
    cos, sin = jnp.cos(ang), jnp.sin(ang)
    n = pos.shape[0]
    half = ROT // 2
    one = jnp.ones((n, HD - ROT), F32)
    zero = jnp.zeros((n, HD - half), F32)
    c = jnp.concatenate([cos, cos, one], 1)
    sa = jnp.concatenate([-sin, zero], 1)
    sb = jnp.concatenate([jnp.zeros((n, half), F32), sin, jnp.zeros((n, HD - ROT), F32)], 1)
    tile = lambda a: jnp.concatenate([a, a], 1)
    return tile(c), tile(sa), tile(sb)


def _pair_state_to_heads(s):
    even = s[:, :, :HD, :HD]
    odd = s[:, :, HD:, HD:]
    b = s.shape[0]
    return jnp.stack([even, odd], axis=2).reshape(b, 2 * N_PAIR, HD, HD)


def kernel(x_prompt, x_sample, cache_k_win, cache_v_win, state_wkv, state_shift, norm_mix_g, w_in, w_out, attn_sinks, shift_mu, decay_w0, decay_up, iclr_a0, iclr_up, gate_up, key_kk, key_ka, bonus_rk, lnx_g, lnx_b, norm_ffn_g, ffn_w_gate, ffn_w_up, ffn_w_down, moe_router, moe_w_gate, moe_w_up, moe_w_down, norm_final_g):
    batch, seq, _ = x_prompt.shape
    sb, st, _ = x_sample.shape
    depth = w_in.shape[0]
    n_p, n_s = batch * seq, sb * st
    n = n_p + n_s
    n_buf = cache_k_win.shape[2]
    if depth == 0:
        raise ValueError("depth must be positive")
    tm = next(c for c in (512, 256, 128) if n_p % c == 0 and n_s % c == 0 and seq % c == 0)
    npt = n_p // tm
    tm_in = 2 * tm if n_p % (2 * tm) == 0 and seq % (2 * tm) == 0 else tm
    n_in = -(-n // tm_in) * tm_in
    pad_rows = lambda a: jnp.pad(a, ((0, n_in - n_p - a.shape[0]), (0, 0)))

    x_p, x_s, s_base = x_prompt.reshape(n_p, D), pad_rows(x_sample.transpose(1, 0, 2).reshape(n_s, D)), 0
    rope_p = _rope_tables(jnp.arange(seq))
    rope_s = tuple(pad_rows(t) for t in _rope_tables(jnp.repeat(PAST_LEN + jnp.arange(st), sb)))
    row = lambda a: a.reshape(1, -1)

    new_p, new_s = [], []
    for l in range(depth):
        mu_rkv, mu_lora = _split_shift_cols(row(shift_mu[l]))
        g_up = jnp.concatenate([gate_up[l], jnp.zeros((G_PAD - LORA_G, D), F32)], 0)
        prm = dict(mu_rkv=mu_rkv, mu_lora=mu_lora, w0=row(decay_w0[l]), w_up=decay_up[l],
                   a0=row(iclr_a0[l]), a_up=iclr_up[l], g_up=g_up, k_k=row(key_kk[l]),
                   k_a=row(key_ka[l]), r_k=row(bonus_rk[l]), lnx_g=row(lnx_g[l]), lnx_b=row(lnx_b[l]))
        expand = lambda rows: {**prm, **{k_: _param_pm(prm[k_], rows) for k_ in ("k_k", "k_a", "r_k", "lnx_g", "lnx_b")}}
        prm_p, prm_s = expand(RWKV_CHUNK), expand(sb)
        p_all = _inproj(x_p, x_s, n_p // tm_in if l else 0, n_p // tm_in, n_in,
                        row(norm_mix_g[l]), _relayout_w_in(w_in[l]), rope_p, rope_s, seq, tm_in)

        att_p = _attn_prompt(p_all, attn_sinks[l], batch, seq)
        ps = p_all[n_p:n].reshape(st, sb, N_COLS).transpose(1, 0, 2)
        k_new, v_new = ps[..., C_K:C_K + 256], ps[..., C_V:C_V + 256]
        k_cache = cache_k_win[l].reshape(sb, n_buf, 256)
        v_cache = cache_v_win[l].reshape(sb, n_buf, 256)
        att_s = _attn_sample(ps[..., C_Q:C_Q + D], k_new, v_new, k_cache, v_cache, attn_sinks[l])
        att_s = pad_rows(att_s.transpose(1, 0, 2).reshape(n_s, D))

        rw_p, s_pairs = _rwkv_prompt(p_all, prm_p, batch, seq)
        sh_rkv, sh_lora = _split_shift_cols(state_shift[l])
        r_s, w_s, k_s, v_s, a_s, b_s, g_s = _rwkv_sample_pre(p_all, n_p, sh_rkv, sh_lora, prm_s, sb, st)
        s0 = state_wkv[l].transpose(1, 2, 3, 0)
        y_t, s_fin = _rwkv_sample_scan(s0, r_s.T, w_s.T, k_s.T, v_s.T[:, None, :], a_s.T, b_s.T)
        rw_s = pad_rows(_rwkv_sample_post(y_t[:, 0, :].T, r_s, k_s, v_s, g_s, prm_s, sb))

        is_moe = l % 2 == 1
        last = l == depth - 1
        gf = row(norm_final_g) if last else None
        if is_moe:
            ne = moe_router.shape[-1]
            router = jnp.concatenate([moe_router[l // 2], jnp.zeros((D, LANES - ne), F32)], 1)
            x_mid, h2, route = _outproj(att_p, att_s, rw_p, rw_s, p_all, x_p, x_s, s_base, w_out[l].astype(BF16),
                                        row(norm_ffn_g[l]), router, ne, tm)
            o_exp = _moe_experts(h2, _moe_plan(route, ne), moe_w_gate[l // 2].astype(BF16),
                                 moe_w_up[l // 2].astype(BF16), moe_w_down[l // 2].astype(BF16),
                                 _pick(moe_w_gate.shape[-1], (896, 512, 256, 128)))
            x = _moe_combine(x_mid, o_exp, route, gf, n_p, tm)
        else:
            x_mid, h2 = _outproj(att_p, att_s, rw_p, rw_s, p_all, x_p, x_s, s_base, w_out[l].astype(BF16),
                                 row(norm_ffn_g[l]), None, 0, tm)
            x = _ffn(h2, ffn_w_gate[l // 2].astype(BF16), ffn_w_up[l // 2].astype(BF16),
                     ffn_w_down[l // 2].astype(BF16), x_mid, gf, n_p, tm,
                     _pick(ffn_w_gate.shape[-1], (1408, 512, 256, 128)))
        if not last:
            x_p, x_s, s_base = x, x, npt

        n_win = min(WINDOW, seq)
        tail = lambda rows, c0, w_: jnp.stack(
            [lax.slice(p_all, ((b + 1) * seq - rows, c0), ((b + 1) * seq, c0 + w_)) for b in range(batch)])
        k_p = tail(n_win, C_K, N_KV * HD).reshape(batch, n_win, N_KV, HD)
        v_p = tail(n_win, C_V, N_KV * HD).reshape(batch, n_win, N_KV, HD)
        unsplit = lambda a: jnp.concatenate(
            [a[..., C_RKV:C_RKV + 3 * D], a[..., C_LORA:C_LORA + LORA_W + LORA_A + LORA_G]], -1)
        last_p = jnp.concatenate([tail(1, C_RKV, 3 * D), tail(1, C_LORA, LORA_W + LORA_A + LORA_G)], -1)[:, 0]
        new_p.append((k_p, v_p, _pair_state_to_heads(s_pairs), last_p))
        k_s_win = jnp.concatenate([k_cache, k_new], 1)[:, -n_buf:].reshape(sb, n_buf, N_KV, HD)
        v_s_win = jnp.concatenate([v_cache, v_new], 1)[:, -n_buf:].reshape(sb, n_buf, N_KV, HD)
        new_s.append((k_s_win, v_s_win, s_fin.transpose(3, 0, 1, 2), unsplit(ps[:, -1])))

    y_p, y_s = x
    y_p = y_p.reshape(batch, seq, D)
    y_s = y_s[:n_s].reshape(st, sb, D).transpose(1, 0, 2)
    stk = lambda sts, i: jnp.stack([s[i] for s in sts])
    return (y_p, y_s,
            stk(new_p, 0), stk(new_p, 1), stk(new_p, 2), stk(new_p, 3),
            stk(new_s, 0), stk(new_s, 1), stk(new_s, 2), stk(new_s, 3))
```

```python
import functools

import jax
import jax.numpy as jnp
from jax import lax
from jax.experimental import pallas as pl
from jax.experimental.pallas import tpu as pltpu

F32 = jnp.float32
BF16 = jnp.bfloat16
HIGHEST = lax.Precision.HIGHEST

LANES = 128
SUBLANES = 8
VMEM_LIMIT = 56 * 1024 * 1024

D = 1024
HD = 64
N_Q = 16
N_KV = 4
ROT = 16
ROPE_THETA = 500000.0
WINDOW = 128
RMS_EPS = 1e-5
LNX_EPS = 64e-5
N_PAIR = D // LANES
LORA_W, LORA_A, LORA_G = 64, 64, 160
LORA_PAD = 512
G_PAD = 256

C_RKV = 0
C_Q = 3072
C_GATE = 4096
C_K = 6144
C_V = 6400
C_LORA = 6656
N_COLS = 7168
TN_IN = 1024
Q_TILE = C_Q // TN_IN
KV_TILE = C_K // TN_IN

RWKV_CHUNK = 64


PAST_LEN = 16384


def _pick(n, cands):
    return next(c for c in cands if n % c == 0)


def _cparams(sem, **kw):
    return pltpu.CompilerParams(dimension_semantics=sem, vmem_limit_bytes=VMEM_LIMIT, **kw)


def _rms(x, g):
    return x * lax.rsqrt(jnp.mean(x * x, -1, keepdims=True) + RMS_EPS) * g


def _sigmoid(x):
    return 1.0 / (1.0 + jnp.exp(-x))


def _rope_chunk(a, c, sa, sb):
    return a * c + pltpu.roll(a, LANES - ROT // 2, 1) * sa + pltpu.roll(a, ROT // 2, 1) * sb


def _inproj_kernel(xp_ref, xs_ref, g_ref, w_ref, cp_ref, sap_ref, sbp_ref, cs_ref, sas_ref, sbs_ref,
                   o_ref, h_ref, *, n_prompt_tiles):
    i, j = pl.program_id(0), pl.program_id(1)
    is_prompt = i < n_prompt_tiles

    @pl.when(j == 0)
    def _():
        x = jnp.where(is_prompt, xp_ref[...], xs_ref[...])
        h_ref[...] = _rms(x, g_ref[...]).astype(BF16)

    acc = jnp.dot(h_ref[...], w_ref[...], preferred_element_type=F32)

    def roped(n_chunks):
        c = jnp.where(is_prompt, cp_ref[...], cs_ref[...])
        sa = jnp.where(is_prompt, sap_ref[...], sas_ref[...])
        sb = jnp.where(is_prompt, sbp_ref[...], sbs_ref[...])
        parts = [_rope_chunk(acc[:, k * LANES:(k + 1) * LANES], c, sa, sb) for k in range(n_chunks)]
        if n_chunks * LANES < TN_IN:
            parts.append(acc[:, n_chunks * LANES:])
        return jnp.concatenate(parts, axis=1)

    @pl.when(j == Q_TILE)
    def _():
        o_ref[...] = roped(TN_IN // LANES)

    @pl.when(j == KV_TILE)
    def _():
        o_ref[...] = roped(N_KV * HD // LANES)

    @pl.when((j != Q_TILE) & (j != KV_TILE))
    def _():
        o_ref[...] = acc


def _two_part_specs(block, npt, s_base, period=None):
    def p_map(i, *_):
        ip = jnp.minimum(i, npt - 1)
        return (ip % period if period else ip, 0)

    def s_map(i, *_):
        return (s_base + jnp.maximum(i - npt, 0), 0)

    return pl.BlockSpec(block, p_map), pl.BlockSpec(block, s_map)


def _inproj(x_p, x_s, s_base, npt, n, g, w, rope_p, rope_s, seq, tm):
    xp_spec, xs_spec = _two_part_specs((tm, D), npt, s_base)
    rp_spec, rs_spec = _two_part_specs((tm, LANES), npt, 0, period=seq // tm)
    return pl.pallas_call(
        functools.partial(_inproj_kernel, n_prompt_tiles=npt),
        out_shape=jax.ShapeDtypeStruct((n, N_COLS), F32),
        grid=(n // tm, N_COLS // TN_IN),
        in_specs=[xp_spec, xs_spec,
                  pl.BlockSpec((1, D), lambda i, j: (0, 0)),
                  pl.BlockSpec((D, TN_IN), lambda i, j: (0, j)),
                  rp_spec, rp_spec, rp_spec, rs_spec, rs_spec, rs_spec],
        out_specs=pl.BlockSpec((tm, TN_IN), lambda i, j: (i, j)),
        scratch_shapes=[pltpu.VMEM((tm, D), BF16)],
        compiler_params=_cparams(("parallel", "arbitrary")),
        name="inproj",
    )(x_p, x_s, g, w, *rope_p, *rope_s)


def _sink_softmax(s, mask, sink):
    s = jnp.where(mask, s * (HD ** -0.5), -jnp.inf)
    m = jnp.maximum(jnp.max(s, -1, keepdims=True), sink)
    p = jnp.exp(s - m)
    return p / (jnp.sum(p, -1, keepdims=True) + jnp.exp(sink - m))


def _dot_nt(a, b, **kw):
    return lax.dot_general(a, b, (((1,), (1,)), ((), ())), preferred_element_type=F32, **kw)


def _dot_tn(a, b, **kw):
    return lax.dot_general(a, b, (((0,), (0,)), ((), ())), preferred_element_type=F32, **kw)


def _head_operands(x, half):
    lo = lax.broadcasted_iota(jnp.int32, x.shape, 1) < HD
    sw = pltpu.roll(x, HD, 1)
    x_lo, x_hi = (x, sw) if half == 0 else (sw, x)
    return jnp.where(lo, x_lo, 0.0).astype(BF16), jnp.where(lo, 0.0, x_hi).astype(BF16)


def _group_queries(q, g):
    return jnp.concatenate([q[:, (2 * g) * LANES:(2 * g + 1) * LANES],
                            q[:, (2 * g + 1) * LANES:(2 * g + 2) * LANES]], axis=0)


def _group_sinks(sink_ref, g, top):
    return (jnp.where(top, sink_ref[4 * g], sink_ref[4 * g + 2]),
            jnp.where(top, sink_ref[4 * g + 1], sink_ref[4 * g + 3]))


def _attn_prompt_kernel(sink_ref, q_ref, kp_ref, kc_ref, vp_ref, vc_ref, o_ref):
    blk = pl.program_id(1)
    w = WINDOW
    q = q_ref[...].astype(BF16)
    k = jnp.concatenate([kp_ref[...], kc_ref[...]], axis=0)
    v = jnp.concatenate([vp_ref[...], vc_ref[...]], axis=0)
    qi = lax.broadcasted_iota(jnp.int32, (w, w), 0)
    kj = lax.broadcasted_iota(jnp.int32, (w, w), 1)
    band = jnp.concatenate([kj <= qi] * 2, axis=0)
    mask = band | (blk > 0)
    top = lax.broadcasted_iota(jnp.int32, (2 * w, 1), 0) < w
    outs = []
    for g in range(N_KV):
        ch, half = divmod(g, 2)
        q2 = _group_queries(q, g)
        o = 0.0
        for k_, v_, sink in zip(_head_operands(k[:, ch * LANES:(ch + 1) * LANES], half),
                                _head_operands(v[:, ch * LANES:(ch + 1) * LANES], half),
                                _group_sinks(sink_ref, g, top)):
            s = _dot_nt(q2, k_)
            p = _sink_softmax(jnp.where(band, s[:, w:], s[:, :w]), mask, sink)
            p_cat = jnp.concatenate([jnp.where(band, 0.0, p), jnp.where(band, p, 0.0)], axis=1)
            o = o + jnp.dot(p_cat.astype(BF16), v_, preferred_element_type=F32)
        outs += [o[:w], o[w:]]
    o_ref[...] = jnp.concatenate(outs, axis=1).astype(o_ref.dtype)


def _attn_prompt(p_all, sinks, batch, seq):
    nb = seq // WINDOW
    kcol, vcol = C_K // (N_KV * HD), C_V // (N_KV * HD)
    cur = lambda b, i: b * nb + i
    prev = lambda b, i: b * nb + jnp.maximum(i - 1, 0)
    return pl.pallas_call(
        _attn_prompt_kernel,
        out_shape=jax.ShapeDtypeStruct((batch * seq, D), BF16),
        grid=(batch, nb),
        in_specs=[
            pl.BlockSpec(memory_space=pltpu.SMEM),
            pl.BlockSpec((WINDOW, D), lambda b, i: (cur(b, i), C_Q // D)),
            pl.BlockSpec((WINDOW, N_KV * HD), lambda b, i: (prev(b, i), kcol)),
            pl.BlockSpec((WINDOW, N_KV * HD), lambda b, i: (cur(b, i), kcol)),
            pl.BlockSpec((WINDOW, N_KV * HD), lambda b, i: (prev(b, i), vcol)),
            pl.BlockSpec((WINDOW, N_KV * HD), lambda b, i: (cur(b, i), vcol)),
        ],
        out_specs=pl.BlockSpec((WINDOW, D), lambda b, i: (cur(b, i), 0)),
        compiler_params=_cparams(("parallel", "arbitrary")),
        name="attn_prompt",
    )(sinks, p_all, p_all, p_all, p_all, p_all)


ATT_S_BT = 8
T_PAD = 8


def _attn_sample_kernel(sink_ref, q_ref, kn_ref, vn_ref, kc_ref, vc_ref, o_ref):
    tn = q_ref.shape[1]
    nbuf = kc_ref.shape[1]
    rows = 2 * tn
    keys = nbuf + T_PAD
    r = lax.broadcasted_iota(jnp.int32, (rows, keys), 0)
    t = jnp.where(r >= tn, r - tn, r)
    kj = lax.broadcasted_iota(jnp.int32, (rows, keys), 1)
    mask = (kj > t + (nbuf - WINDOW)) & (kj <= t + nbuf)
    top = lax.broadcasted_iota(jnp.int32, (rows, 1), 0) < tn
    zpad = jnp.zeros((T_PAD - tn, N_KV * HD), F32)
    scores, sinks, values = [], [], []
    for b in range(ATT_S_BT):
        q = q_ref[b].astype(BF16)
        k = jnp.concatenate([kc_ref[b], kn_ref[b], zpad], axis=0)
        v = jnp.concatenate([vc_ref[b], vn_ref[b], zpad], axis=0)
        for g in range(N_KV):
            ch, half = divmod(g, 2)
            q2 = _group_queries(q, g)
            values += list(_head_operands(v[:, ch * LANES:(ch + 1) * LANES], half))
            sinks += list(_group_sinks(sink_ref, g, top))
            scores += [_dot_nt(q2, k_) for k_ in _head_operands(k[:, ch * LANES:(ch + 1) * LANES], half)]
    p = _sink_softmax(jnp.stack(scores), mask[None], jnp.stack(sinks)).astype(BF16)
    for b in range(ATT_S_BT):
        outs = []
        for g in range(N_KV):
            i = 2 * (b * N_KV + g)
            o = (jnp.dot(p[i], values[i], preferred_element_type=F32)
                 + jnp.dot(p[i + 1], values[i + 1], preferred_element_type=F32))
            outs += [o[:tn], o[tn:]]
        o_ref[b] = jnp.concatenate(outs, axis=1).astype(o_ref.dtype)


def _attn_sample(q, k_new, v_new, k_cache, v_cache, sinks):
    bsz, tn, _ = q.shape
    nbuf = k_cache.shape[1]
    kvw = N_KV * HD
    blk = lambda w_, r_: pl.BlockSpec((ATT_S_BT, r_, w_), lambda i: (i, 0, 0))
    return pl.pallas_call(
        _attn_sample_kernel,
        out_shape=jax.ShapeDtypeStruct((bsz, tn, D), BF16),
        grid=(bsz // ATT_S_BT,),
        in_specs=[pl.BlockSpec(memory_space=pltpu.SMEM),
                  blk(D, tn), blk(kvw, tn), blk(kvw, tn), blk(kvw, nbuf), blk(kvw, nbuf)],
        out_specs=blk(D, tn),
        compiler_params=_cparams(("parallel",)),
        name="attn_sample",
    )(sinks, q, k_new, v_new, k_cache, v_cache)


def _to_pm(x):
    return jnp.concatenate([x[:, p * LANES:(p + 1) * LANES] for p in range(N_PAIR)], axis=0)


def _from_pm(x):
    r = x.shape[0] // N_PAIR
    return jnp.concatenate([x[p * r:(p + 1) * r] for p in range(N_PAIR)], axis=1)


def _param_pm(v, r):
    if v.shape == (N_PAIR * r, LANES):
        return v
    return jnp.concatenate(
        [jnp.broadcast_to(v[:, p * LANES:(p + 1) * LANES], (r, LANES)) for p in range(N_PAIR)], axis=0)


def _head_sum(x, ones_bd):
    hi = x.astype(BF16)
    out = jnp.dot(hi, ones_bd, preferred_element_type=F32)
    if _PASSES["head_sum"] == 2:
        lo = (x - hi.astype(F32)).astype(BF16)
        out = out + jnp.dot(lo, ones_bd, preferred_element_type=F32)
    return out


def _ones_bd():
    r = lax.broadcasted_iota(jnp.int32, (LANES, LANES), 0) // HD
    c = lax.broadcasted_iota(jnp.int32, (LANES, LANES), 1) // HD
    return jnp.where(r == c, 1.0, 0.0).astype(BF16)


def _softplus(z):
    return jnp.maximum(z, 0.0) + jnp.log(1.0 + jnp.exp(-jnp.abs(z)))


def _rwkv_pre(p_rkv, p_lora, prev_rkv, prev_lora, prm):
    rows = p_rkv.shape[0]
    xs = p_rkv + (prev_rkv - p_rkv) * prm["mu_rkv"]
    xl = p_lora + (prev_lora - p_lora) * prm["mu_lora"]
    wd = xl[:, 0:LORA_W]
    ad = xl[:, LORA_W:LORA_W + LORA_A]
    gd = xl[:, LANES:LANES + G_PAD]
    mm = (((1,), (0,)), ((), ()))
    w_pre = prm["w0"] + _pdot(jnp.tanh(wd), prm["w_up"], mm, _PASSES["lora_w"])
    a_pre = prm["a0"] + _pdot(ad, prm["a_up"], mm, _PASSES["lora_a"])
    g = _pdot(_sigmoid(gd), prm["g_up"], mm, _PASSES["lora_g"])
    logw = -jnp.exp(-_softplus(-w_pre) - 0.5)
    a = _to_pm(_sigmoid(a_pre))
    r = _to_pm(xs[:, 0:D])
    k = _to_pm(xs[:, D:2 * D])
    v = _to_pm(xs[:, 2 * D:3 * D])
    kk = k * _param_pm(prm["k_k"], rows)
    nrm = jnp.sqrt(_head_sum(kk * kk, _ones_bd()))
    kk = kk / jnp.maximum(nrm, 1e-12)
    k = k * (1.0 + (a - 1.0) * _param_pm(prm["k_a"], rows))
    return r, k, v, -kk, kk * a, _to_pm(logw), _to_pm(g)


def _rwkv_post(y, r, k, v, g, prm):
    rows = y.shape[0] // N_PAIR
    ones_bd = _ones_bd()
    mean = _head_sum(y, ones_bd) * (1.0 / HD)
    yc = y - mean
    var = _head_sum(yc * yc, ones_bd) * (1.0 / HD)
    yn = yc * lax.rsqrt(var + LNX_EPS) * _param_pm(prm["lnx_g"], rows) + _param_pm(prm["lnx_b"], rows)
    bonus = _head_sum(r * k * _param_pm(prm["r_k"], rows), ones_bd) * v
    return (yn + bonus) * g


_PRM_NAMES = ("mu_rkv", "mu_lora", "w0", "w_up", "a0", "a_up", "g_up", "k_k", "k_a", "r_k", "lnx_g", "lnx_b")


def _prm_specs(prm, n_grid):
    zero = lambda *_: (0, 0)
    return [pl.BlockSpec(prm[n].shape, zero) for n in _PRM_NAMES]


def _stack2(x, lo):
    return jnp.concatenate([jnp.where(lo, x, 0.0), jnp.where(lo, 0.0, x)], axis=1)


def _split_bf16(x):
    hi = x.astype(BF16)
    return hi, (x - hi.astype(F32)).astype(BF16)


def _pdot(a, b, dims, passes):
    if passes == 6:
        return lax.dot_general(a, b, dims, preferred_element_type=F32, precision=HIGHEST)
    dot = lambda x, y: lax.dot_general(x, y, dims, preferred_element_type=F32)
    if passes == 1:
        return dot(a.astype(BF16), b.astype(BF16))
    a_hi, a_lo = _split_bf16(a)
    b_hi, b_lo = _split_bf16(b)
    return dot(a_hi, b_hi) + (dot(a_hi, b_lo) + dot(a_lo, b_hi))


_PASSES = dict(lora_w=1, lora_a=1, lora_g=1, gram=1, inv=1, rhs=1, u=1, y=1, state=1, head_sum=1)


def _bdot(a, b, ca, cb, site):
    return _pdot(a, b, (((ca,), (cb,)), ((0,), (0,))), _PASSES[site])


def _bmm(a, b, site):
    return _bdot(a, b, 2, 1, site)


def _bmm_nt(a, b, site):
    return _bdot(a, b, 2, 2, site)


def _bmm_tn(a, b, site):
    return _bdot(a, b, 1, 1, site)


def _rwkv_chunk(s, r, k, v, a_, b_, logw):
    c = r.shape[1]
    ti = lax.broadcasted_iota(jnp.int32, (N_PAIR, c, c), 1)
    si = lax.broadcasted_iota(jnp.int32, (N_PAIR, c, c), 2)
    tri = jnp.where(si <= ti, 1.0, 0.0).astype(BF16)
    w_hi, w_lo = _split_bf16(logw)
    cdot = lambda x: lax.dot_general(tri, x, (((2,), (1,)), ((0,), (0,))), preferred_element_type=F32)
    cum = cdot(w_hi) + cdot(w_lo)
    e_neg = jnp.exp(-cum)
    l_end = cum[:, c - 1:c, :]
    e_end = jnp.exp(l_end)
    assert c == HD, "the lane split of (C, 2C) time matrices reuses the head mask"
    lo = lax.broadcasted_iota(jnp.int32, (N_PAIR, c, LANES), 2) < HD
    at = a_ * jnp.exp(cum - logw)
    rt = r * jnp.exp(cum)
    yb, yk = _stack2(b_ * e_neg, lo), _stack2(k * e_neg, lo)
    vs = _stack2(v, lo)
    gmat = _bmm_nt(jnp.concatenate([at, rt], axis=1), jnp.concatenate([yb, yk], axis=1), "gram")
    n2 = 2 * c
    tr = lax.broadcasted_iota(jnp.int32, (N_PAIR, c, n2), 1)
    tc = lax.broadcasted_iota(jnp.int32, (N_PAIR, c, n2), 2) & (c - 1)
    strict, incl = tr > tc, tr >= tc
    a_ab = jnp.where(strict, gmat[:, :c, :n2], 0.0)
    a_ak = jnp.where(strict, gmat[:, :c, n2:], 0.0)
    a_rb = jnp.where(incl, gmat[:, c:, :n2], 0.0)
    a_rk = jnp.where(incl, gmat[:, c:, n2:], 0.0)
    tinv = jnp.where(tr == tc, 1.0, 0.0) + a_ab
    pw = _bmm(a_ab, _stack2(a_ab, lo), "inv")
    d = 4
    while d < c:
        res = _bmm(jnp.concatenate([tinv, pw], axis=1), _stack2(pw, lo), "inv")
        tinv, pw = tinv + res[:, :c], res[:, c:]
        d *= 2
    tinv = tinv + _bmm(tinv, _stack2(pw, lo), "inv")
    rhs = _bmm_nt(at, s, "rhs") + _bmm(a_ak, vs, "rhs")
    us = _stack2(_bmm(tinv, _stack2(rhs, lo), "u"), lo)
    uv = jnp.concatenate([us, vs], axis=1)
    y = _bmm_nt(rt, s, "y") + _bmm(jnp.concatenate([a_rb, a_rk], axis=2), uv, "y")
    s_new = s * e_end + _bmm_tn(uv, jnp.concatenate([yb * e_end, yk * e_end], axis=1), "state")
    return s_new, y


def _rwkv_prompt_kernel(prkv_ref, plora_ref, *rest):
    prm_refs = rest[:len(_PRM_NAMES)]
    o_ref, s_out_ref, s_ref, carry_rkv, carry_lora = rest[len(_PRM_NAMES):]
    ci = pl.program_id(1)
    c = RWKV_CHUNK

    @pl.when(ci == 0)
    def _():
        s_ref[...] = jnp.zeros_like(s_ref)
        carry_rkv[...] = jnp.zeros_like(carry_rkv)
        carry_lora[...] = jnp.zeros_like(carry_lora)

    prm = {n: ref[...] for n, ref in zip(_PRM_NAMES, prm_refs)}

    def shifted(x, last):
        first = lax.broadcasted_iota(jnp.int32, x.shape, 0) == 0
        return jnp.where(first, last, pltpu.roll(x, 1, 0))

    s = s_ref[...]
    last_rkv, last_lora = carry_rkv[0:1, :], carry_lora[0:1, :]
    sh = (N_PAIR, c, LANES)
    for i in range(prkv_ref.shape[0] // c):
        rows = slice(i * c, (i + 1) * c)
        p_rkv, p_lora = prkv_ref[rows, :], plora_ref[rows, :]
        r, k, v, a_, b_, logw, g = _rwkv_pre(p_rkv, p_lora, shifted(p_rkv, last_rkv),
                                             shifted(p_lora, last_lora), prm)
        last_rkv, last_lora = p_rkv[c - 1:c, :], p_lora[c - 1:c, :]
        s, y = _rwkv_chunk(s, r.reshape(sh), k.reshape(sh), v.reshape(sh),
                           a_.reshape(sh), b_.reshape(sh), logw.reshape(sh))
        out = _rwkv_post(y.reshape(N_PAIR * c, LANES), r, k, v, g, prm)
        o_ref[rows, :] = _from_pm(out).astype(o_ref.dtype)
    s_ref[...] = s
    carry_rkv[0:1, :] = last_rkv
    carry_lora[0:1, :] = last_lora

    @pl.when(ci == pl.num_programs(1) - 1)
    def _():
        s_out_ref[0] = s


RWKV_BLOCK = 4 * RWKV_CHUNK


def _rwkv_prompt(p_all, prm, batch, seq):
    c = RWKV_BLOCK if seq % RWKV_BLOCK == 0 else RWKV_CHUNK
    nc = seq // c
    row = lambda b, i: b * nc + i
    return pl.pallas_call(
        _rwkv_prompt_kernel,
        out_shape=(jax.ShapeDtypeStruct((batch * seq, D), BF16),
                   jax.ShapeDtypeStruct((batch, N_PAIR, LANES, LANES), F32)),
        grid=(batch, nc),
        in_specs=[pl.BlockSpec((c, 3 * D), lambda b, i: (row(b, i), C_RKV // (3 * D))),
                  pl.BlockSpec((c, LORA_PAD), lambda b, i: (row(b, i), C_LORA // LORA_PAD))]
                 + _prm_specs(prm, 2),
        out_specs=(pl.BlockSpec((c, D), lambda b, i: (row(b, i), 0)),
                   pl.BlockSpec((1, N_PAIR, LANES, LANES), lambda b, i: (b, 0, 0, 0))),
        scratch_shapes=[pltpu.VMEM((N_PAIR, LANES, LANES), F32),
                        pltpu.VMEM((8, 3 * D), F32),
                        pltpu.VMEM((8, LORA_PAD), F32)],
        compiler_params=_cparams(("parallel", "arbitrary")),
        name="rwkv_prompt",
    )(p_all, p_all, *[prm[n] for n in _PRM_NAMES])


def _rwkv_sample_pre_kernel(prkv_ref, plora_ref, qrkv_ref, qlora_ref, srkv_ref, slora_ref, *rest):
    prm_refs = rest[:len(_PRM_NAMES)]
    outs = rest[len(_PRM_NAMES):]
    t = pl.program_id(0)
    prm = {n: ref[...] for n, ref in zip(_PRM_NAMES, prm_refs)}
    first = t == 0
    prev_rkv = jnp.where(first, srkv_ref[...], qrkv_ref[...])
    prev_lora = jnp.where(first, slora_ref[...], qlora_ref[...])
    r, k, v, a_, b_, logw, g = _rwkv_pre(prkv_ref[...], plora_ref[...], prev_rkv, prev_lora, prm)
    for ref, val in zip(outs, (r, jnp.exp(logw), k, v, a_, b_, g)):
        ref[...] = _from_pm(val)


def _rwkv_sample_pre(p_all, row0, shift_rkv, shift_lora, prm, bsz, tn):
    base = row0 // bsz
    cur = lambda t: base + t
    prv = lambda t: base + jnp.maximum(t - 1, 0)
    out = jax.ShapeDtypeStruct((tn * bsz, D), F32)
    return pl.pallas_call(
        _rwkv_sample_pre_kernel,
        out_shape=(out,) * 7,
        grid=(tn,),
        in_specs=[pl.BlockSpec((bsz, 3 * D), lambda t: (cur(t), C_RKV // (3 * D))),
                  pl.BlockSpec((bsz, LORA_PAD), lambda t: (cur(t), C_LORA // LORA_PAD)),
                  pl.BlockSpec((bsz, 3 * D), lambda t: (prv(t), C_RKV // (3 * D))),
                  pl.BlockSpec((bsz, LORA_PAD), lambda t: (prv(t), C_LORA // LORA_PAD)),
                  pl.BlockSpec((bsz, 3 * D), lambda t: (0, 0)),
                  pl.BlockSpec((bsz, LORA_PAD), lambda t: (0, 0))]
                 + _prm_specs(prm, 1),
        out_specs=tuple(pl.BlockSpec((bsz, D), lambda t: (t, 0)) for _ in range(7)),
        compiler_params=_cparams(("arbitrary",)),
        name="rwkv_sample_pre",
    )(p_all, p_all, p_all, p_all, shift_rkv, shift_lora, *[prm[n] for n in _PRM_NAMES])


def _rwkv_sample_scan_kernel(s0_ref, r_ref, w_ref, k_ref, v_ref, a_ref, b_ref, y_ref, so_ref, s_ref):
    bsz = s0_ref.shape[-1]
    tn = r_ref.shape[1] // bsz
    s_ref[...] = s0_ref[0]
    for t in range(tn):
        cols = slice(t * bsz, (t + 1) * bsz)
        r_t, w_t, k_t = r_ref[:, cols], w_ref[:, cols], k_ref[:, cols]
        a_t, b_t = a_ref[:, cols], b_ref[:, cols]

        def body(i, _):
            s_i = s_ref[i]
            sa = jnp.sum(s_i * a_t, axis=0, keepdims=True)
            v_i = v_ref[i, :, cols]
            s_i = s_i * w_t + sa * b_t + v_i * k_t
            s_ref[i] = s_i
            y_ref[i, :, cols] = jnp.sum(s_i * r_t, axis=0, keepdims=True)
            return 0

        lax.fori_loop(0, HD, body, 0)
    so_ref[0] = s_ref[...]


def _rwkv_sample_scan(s0, r, w, k, v, a_, b_):
    nh, _, _, bsz = s0.shape
    tb = r.shape[1]
    vec = pl.BlockSpec((HD, tb), lambda h: (h, 0))
    vec3 = pl.BlockSpec((HD, 1, tb), lambda h: (h, 0, 0))
    st = pl.BlockSpec((1, HD, HD, bsz), lambda h: (h, 0, 0, 0))
    return pl.pallas_call(
        _rwkv_sample_scan_kernel,
        out_shape=(jax.ShapeDtypeStruct((nh * HD, 1, tb), F32), jax.ShapeDtypeStruct(s0.shape, F32)),
        grid=(nh,),
        in_specs=[st, vec, vec, vec, vec3, vec, vec],
        out_specs=(vec3, st),
        scratch_shapes=[pltpu.VMEM((HD, HD, bsz), F32)],
        compiler_params=_cparams(("parallel",)),
        name="rwkv_sample_scan",
    )(s0, r, w, k, v, a_, b_)


def _rwkv_sample_post_kernel(y_ref, r_ref, k_ref, v_ref, g_ref, *rest):
    prm_refs = rest[:len(_PRM_NAMES)]
    o_ref = rest[len(_PRM_NAMES)]
    prm = {n: ref[...] for n, ref in zip(_PRM_NAMES, prm_refs)}
    out = _rwkv_post(_to_pm(y_ref[...]), _to_pm(r_ref[...]), _to_pm(k_ref[...]),
                     _to_pm(v_ref[...]), _to_pm(g_ref[...]), prm)
    o_ref[...] = _from_pm(out).astype(o_ref.dtype)


def _rwkv_sample_post(y, r, k, v, g, prm, bsz):
    n = y.shape[0]
    blk = pl.BlockSpec((bsz, D), lambda t: (t, 0))
    return pl.pallas_call(
        _rwkv_sample_post_kernel,
        out_shape=jax.ShapeDtypeStruct((n, D), BF16),
        grid=(n // bsz,),
        in_specs=[blk] * 5 + _prm_specs(prm, 1),
        out_specs=blk,
        compiler_params=_cparams(("parallel",)),
        name="rwkv_sample_post",
    )(y, r, k, v, g, *[prm[n] for n in _PRM_NAMES])


N_CHUNK = D // LANES


def _store_token_major(ref, x, lead=()):
    rows = x.shape[0]
    for c in range(N_CHUNK):
        ref[lead + (pl.ds(c, rows, stride=N_CHUNK), slice(None))] = x[:, c * LANES:(c + 1) * LANES]


def _load_token_major(ref, rows, lead=()):
    return jnp.concatenate(
        [ref[lead + (pl.ds(c, rows, stride=N_CHUNK), slice(None))] for c in range(N_CHUNK)], axis=1)


def _outproj_kernel(attp_ref, atts_ref, rwp_ref, rws_ref, ga_ref, gr_ref, xp_ref, xs_ref, w_ref, g2_ref, *rest,
                    n_experts, n_prompt_tiles):
    is_prompt = pl.program_id(0) < n_prompt_tiles
    att = jnp.where(is_prompt, attp_ref[...], atts_ref[...]).astype(F32)
    rw = jnp.where(is_prompt, rwp_ref[...], rws_ref[...]).astype(F32)
    x = jnp.where(is_prompt, xp_ref[...], xs_ref[...])
    m = _sigmoid(ga_ref[...]) * att + _sigmoid(gr_ref[...]) * rw
    xn = x + jnp.dot(m.astype(BF16), w_ref[...], preferred_element_type=F32)
    h2 = _rms(xn, g2_ref[...])
    if n_experts:
        router_ref, xo_ref, h2_ref, gate_ref = rest
        logits = _pdot(h2, router_ref[...], (((1,), (0,)), ((), ())), 3)
        lane = lax.broadcasted_iota(jnp.int32, logits.shape, 1).astype(F32)
        lg = jnp.where(lane < n_experts, logits, -jnp.inf)
        v1 = jnp.max(lg, -1, keepdims=True)
        i1 = jnp.min(jnp.where(lg == v1, lane, float(LANES)), -1, keepdims=True)
        lg2 = jnp.where(lane == i1, -jnp.inf, lg)
        v2 = jnp.max(lg2, -1, keepdims=True)
        i2 = jnp.min(jnp.where(lg2 == v2, lane, float(LANES)), -1, keepdims=True)
        e2 = jnp.exp(v2 - v1)
        den = 1.0 + e2
        gate_ref[...] = (jnp.where(lane == 0.0, i1, 0.0) + jnp.where(lane == 1.0, i2, 0.0)
                         + jnp.where(lane == 2.0, 1.0 / den, 0.0) + jnp.where(lane == 3.0, e2 / den, 0.0))
        _store_token_major(h2_ref, h2)
    else:
        xo_ref, h2_ref = rest
        h2_ref[...] = h2.astype(BF16)
    xo_ref[...] = xn


def _outproj(att_p, att_s, rw_p, rw_s, p_all, x_p, x_s, s_base, w_out, g2, router, n_experts, tm):
    n = att_p.shape[0] + att_s.shape[0]
    npt = att_p.shape[0] // tm
    row = lambda w_: pl.BlockSpec((tm, w_), lambda i: (i, 0))
    p_spec, s_spec = _two_part_specs((tm, D), npt, 0)
    xp_spec, xs_spec = _two_part_specs((tm, D), npt, s_base)
    in_specs = [p_spec, s_spec, p_spec, s_spec,
                pl.BlockSpec((tm, D), lambda i: (i, C_GATE // D)),
                pl.BlockSpec((tm, D), lambda i: (i, C_GATE // D + 1)),
                xp_spec, xs_spec,
                pl.BlockSpec((D, D), lambda i: (0, 0)),
                pl.BlockSpec((1, D), lambda i: (0, 0))]
    args = [att_p, att_s, rw_p, rw_s, p_all, p_all, x_p, x_s, w_out, g2]
    if n_experts:
        h2_shape = jax.ShapeDtypeStruct((n * N_CHUNK, LANES), F32)
        h2_spec = pl.BlockSpec((tm * N_CHUNK, LANES), lambda i: (i, 0))
    else:
        h2_shape, h2_spec = jax.ShapeDtypeStruct((n, D), BF16), row(D)
    out_shape = [jax.ShapeDtypeStruct((n, D), F32), h2_shape]
    out_specs = [row(D), h2_spec]
    if n_experts:
        in_specs.append(pl.BlockSpec((D, LANES), lambda i: (0, 0)))
        args.append(router)
        out_shape.append(jax.ShapeDtypeStruct((n, LANES), F32))
        out_specs.append(row(LANES))
    return pl.pallas_call(
        functools.partial(_outproj_kernel, n_experts=n_experts, n_prompt_tiles=npt),
        out_shape=tuple(out_shape),
        grid=(n // tm,),
        in_specs=in_specs,
        out_specs=tuple(out_specs),
        compiler_params=_cparams(("parallel",)),
        name="outproj",
    )(*args)


def _swiglu_part(h, wg, wu, wd):
    a = jnp.dot(h, wg, preferred_element_type=F32)
    b = jnp.dot(h, wu, preferred_element_type=F32)
    t = (a * _sigmoid(a) * b).astype(BF16)
    return jnp.dot(t, wd, preferred_element_type=F32)


def _final_store(y, gf_ref, out_refs, n_prompt_tiles):
    if gf_ref is None:
        (o_ref,) = out_refs
        o_ref[...] = y
        return
    op_ref, os_ref = out_refs
    y = _rms(y, gf_ref[...])
    is_prompt = pl.program_id(0) < n_prompt_tiles

    @pl.when(is_prompt)
    def _():
        op_ref[...] = y

    @pl.when(jnp.logical_not(is_prompt))
    def _():
        os_ref[...] = y


def _final_out(n, n_p, tm, final):
    if not final:
        return jax.ShapeDtypeStruct((n, D), F32), pl.BlockSpec((tm, D), lambda i, *_: (i, 0))
    return ((jax.ShapeDtypeStruct((n_p, D), F32), jax.ShapeDtypeStruct((n - n_p, D), F32)),
            _two_part_specs((tm, D), n_p // tm, 0))


def _ffn_kernel(h_ref, wg_ref, wu_ref, wd_ref, x_ref, *rest, final, n_prompt_tiles):
    rest = list(rest)
    gf_ref = rest.pop(0) if final else None
    acc_ref = rest.pop()
    f = pl.program_id(1)

    @pl.when(f == 0)
    def _():
        acc_ref[...] = jnp.zeros_like(acc_ref)

    acc_ref[...] += _swiglu_part(h_ref[...], wg_ref[...], wu_ref[...], wd_ref[...])

    @pl.when(f == pl.num_programs(1) - 1)
    def _():
        _final_store(x_ref[...] + acc_ref[...], gf_ref, rest, n_prompt_tiles)


def _ffn(h, wg, wu, wd, x, gf, n_p, tm, tf):
    n = x.shape[0]
    fdim = wg.shape[1]
    final = gf is not None
    in_specs = [pl.BlockSpec((tm, D), lambda i, f: (i, 0)),
                pl.BlockSpec((D, tf), lambda i, f: (0, f)),
                pl.BlockSpec((D, tf), lambda i, f: (0, f)),
                pl.BlockSpec((tf, D), lambda i, f: (f, 0)),
                pl.BlockSpec((tm, D), lambda i, f: (i, 0))]
    args = [h, wg, wu, wd, x]
    if final:
        in_specs.append(pl.BlockSpec((1, D), lambda i, f: (0, 0)))
        args.append(gf)
    out_shape, out_specs = _final_out(n, n_p, tm, final)
    return pl.pallas_call(
        functools.partial(_ffn_kernel, final=final, n_prompt_tiles=n_p // tm),
        out_shape=out_shape,
        grid=(n // tm, fdim // tf),
        in_specs=in_specs,
        out_specs=out_specs,
        scratch_shapes=[pltpu.VMEM((tm, D), F32)],
        compiler_params=_cparams(("arbitrary" if final else "parallel", "arbitrary")),
        name="ffn",
    )(*args)


MOE_TM = 512


def _moe_plan(route, n_experts):
    n = route.shape[0]
    e_flat = route[:, :2].astype(jnp.int32).T.reshape(-1)
    n_asg = e_flat.shape[0]
    order = jnp.argsort(e_flat, stable=True).astype(jnp.int32)
    counts = jnp.sum((e_flat[:, None] == jnp.arange(n_experts)[None, :]).astype(jnp.int32), axis=0)
    first = jnp.cumsum(counts) - counts
    padded = (counts + MOE_TM - 1) // MOE_TM * MOE_TM
    ends = jnp.cumsum(padded)
    offs = ends - padded
    n_tiles = -(-n_asg // MOE_TM) + n_experts
    start = jnp.arange(n_tiles, dtype=jnp.int32) * MOE_TM
    n_active = ends[-1] // MOE_TM
    tile_e = jnp.sum((start[:, None] >= ends[None, :]).astype(jnp.int32), axis=1)
    last_e = jnp.sum(((n_active - 1) * MOE_TM >= ends).astype(jnp.int32))
    tile_e = jnp.minimum(tile_e, last_e)
    n_valid = jnp.clip(counts[tile_e] - (start - offs[tile_e]), 0, MOE_TM)
    n_valid = jnp.where(start < ends[-1], n_valid, 0)
    r_in_tile = jnp.arange(MOE_TM, dtype=jnp.int32)[None, :]
    src = (first[tile_e] + start - offs[tile_e])[:, None] + r_in_tile
    real = r_in_tile < n_valid[:, None]
    dst = jnp.where(real, order[jnp.where(real, src, 0)], 0).reshape(-1)
    tok = jnp.where(dst >= n, dst - n, dst)
    return (tile_e.astype(jnp.int32), n_valid.astype(jnp.int32), n_active.reshape(1).astype(jnp.int32),
            tok.astype(jnp.int32), dst.astype(jnp.int32))


def _moe_kernel(te_ref, nv_ref, na_ref, tok_ref, dst_ref, h_hbm, wg_ref, wu_ref, wd_ref, o_hbm,
                xbuf, xb, acc, stage, gsem, ssem, *, nf):
    del te_ref
    j, f = pl.program_id(0), pl.program_id(1)
    n_tiles = pl.num_programs(0)
    n_active = na_ref[0]
    active = j < n_active
    slot = j % 2
    other = 1 - slot
    tile_rows = MOE_TM * N_CHUNK
    dump0 = o_hbm.shape[0] // N_CHUNK - MOE_TM
    rows_per_step = MOE_TM // nf
    token = lambda t: pl.ds(pl.multiple_of(t * N_CHUNK, N_CHUNK), N_CHUNK)

    def gather_row(tile, slot_, r):
        tok = tok_ref[tile * MOE_TM + r]
        return pltpu.make_async_copy(h_hbm.at[token(tok)], xbuf.at[slot_, token(r)], gsem.at[slot_])

    def gather_wait(slot_):
        pltpu.make_async_copy(h_hbm.at[pl.ds(0, tile_rows)], xbuf.at[slot_], gsem.at[slot_]).wait()

    def scatter_row(tile, slot_, r, real):
        row = jnp.where(real & (r < nv_ref[tile]), dst_ref[tile * MOE_TM + r], dump0 + r)
        return pltpu.make_async_copy(stage.at[slot_, token(r)], o_hbm.at[token(row)], ssem.at[slot_])

    def scatter_wait(slot_):
        pltpu.make_async_copy(stage.at[slot_], o_hbm.at[pl.ds(0, tile_rows)], ssem.at[slot_]).wait()

    @pl.when(active & (f == 0))
    def _():
        @pl.when(j == 0)
        def _():
            def body(r, _):
                gather_row(0, 0, r).start()
                return 0
            lax.fori_loop(0, MOE_TM, body, 0, unroll=SUBLANES)
            stage[1] = jnp.zeros((tile_rows, LANES), F32)

        gather_wait(slot)
        xb[...] = _load_token_major(xbuf, MOE_TM, (slot,)).astype(BF16)

        @pl.when(j > 0)
        def _():
            scatter_wait(slot)

    @pl.when(active)
    def _():
        part = _swiglu_part(xb[...], wg_ref[0], wu_ref[0], wd_ref[0])
        nxt = jnp.minimum(j + 1, n_tiles - 1)
        prev = jnp.maximum(j - 1, 0)
        for u in range(rows_per_step):
            r = f * rows_per_step + u
            gather_row(nxt, other, r).start()
            scatter_row(prev, other, r, j > 0).start()

        if nf > 1:
            @pl.when(f == 0)
            def _():
                acc[...] = part

            @pl.when((f > 0) & (f < nf - 1))
            def _():
                acc[...] += part

        @pl.when(f == nf - 1)
        def _():
            _store_token_major(stage, acc[...] + part if nf > 1 else part, (slot,))

        @pl.when((f == nf - 1) & (j == n_active - 1))
        def _():
            gather_wait(other)
            scatter_wait(other)

            def body(r, _):
                scatter_row(j, slot, r, True).start()
                return 0
            lax.fori_loop(0, MOE_TM, body, 0, unroll=SUBLANES)
            scatter_wait(slot)


def _moe_experts(h, plan, wg, wu, wd, tf):
    tile_e, n_valid, n_active, tok, dst = plan
    n = h.shape[0] // N_CHUNK
    fdim = wg.shape[2]
    n_tiles = tile_e.shape[0]
    nf = fdim // tf

    def wmap(j, f, te, nv, na, tok_, dst_):
        return te[j], jnp.where(j < na[0], f, nf - 1)

    grid_spec = pltpu.PrefetchScalarGridSpec(
        num_scalar_prefetch=5,
        grid=(n_tiles, nf),
        in_specs=[pl.BlockSpec(memory_space=pl.ANY),
                  pl.BlockSpec((1, D, tf), lambda *a: (wmap(*a)[0], 0, wmap(*a)[1])),
                  pl.BlockSpec((1, D, tf), lambda *a: (wmap(*a)[0], 0, wmap(*a)[1])),
                  pl.BlockSpec((1, tf, D), lambda *a: (wmap(*a)[0], wmap(*a)[1], 0))],
        out_specs=pl.BlockSpec(memory_space=pl.ANY),
        scratch_shapes=[pltpu.VMEM((2, MOE_TM * N_CHUNK, LANES), F32),
                        pltpu.VMEM((MOE_TM, D), BF16),
                        pltpu.VMEM((MOE_TM, D), F32),
                        pltpu.VMEM((2, MOE_TM * N_CHUNK, LANES), F32),
                        pltpu.SemaphoreType.DMA((2,)),
                        pltpu.SemaphoreType.DMA((2,))],
    )
    return pl.pallas_call(
        functools.partial(_moe_kernel, nf=nf),
        out_shape=jax.ShapeDtypeStruct(((2 * n + MOE_TM) * N_CHUNK, LANES), F32),
        grid_spec=grid_spec,
        compiler_params=_cparams(("arbitrary", "arbitrary"), disable_bounds_checks=True),
        name="moe_experts",
    )(tile_e, n_valid, n_active, tok, dst, h, wg, wu, wd)


def _moe_combine_kernel(x_ref, o1_ref, o2_ref, route_ref, *rest, final, n_prompt_tiles):
    rest = list(rest)
    gf_ref = rest.pop(0) if final else None
    route = route_ref[...]
    lane = lax.broadcasted_iota(jnp.int32, route.shape, 1)
    w1 = jnp.sum(jnp.where(lane == 2, route, 0.0), -1, keepdims=True)
    w2 = jnp.sum(jnp.where(lane == 3, route, 0.0), -1, keepdims=True)
    rows = x_ref.shape[0]
    o1, o2 = _load_token_major(o1_ref, rows), _load_token_major(o2_ref, rows)
    _final_store(x_ref[...] + (w1 * o1 + w2 * o2), gf_ref, rest, n_prompt_tiles)


def _moe_combine(x, o, route, gf, n_p, tm):
    n = x.shape[0]
    nb = n // tm
    final = gf is not None
    in_specs = [pl.BlockSpec((tm, D), lambda i: (i, 0)),
                pl.BlockSpec((tm * N_CHUNK, LANES), lambda i: (i, 0)),
                pl.BlockSpec((tm * N_CHUNK, LANES), lambda i: (i + nb, 0)),
                pl.BlockSpec((tm, LANES), lambda i: (i, 0))]
    args = [x, o, o, route]
    if final:
        in_specs.append(pl.BlockSpec((1, D), lambda i: (0, 0)))
        args.append(gf)
    out_shape, out_specs = _final_out(n, n_p, tm, final)
    return pl.pallas_call(
        functools.partial(_moe_combine_kernel, final=final, n_prompt_tiles=n_p // tm),
        out_shape=out_shape,
        grid=(nb,),
        in_specs=in_specs,
        out_specs=out_specs,
        compiler_params=_cparams(("arbitrary",)),
        name="moe_combine",
    )(*args)


def _split_shift_cols(a):
    pad = jnp.zeros(a.shape[:-1] + (LORA_PAD - (LORA_W + LORA_A + LORA_G),), a.dtype)
    lead = a[..., 3 * D:3 * D + LORA_W + LORA_A]
    gd = a[..., 3 * D + LORA_W + LORA_A:]
    return a[..., :3 * D], jnp.concatenate([lead, gd, pad], -1)


def _relayout_w_in(w):
    q, k, v = w[:, 0:D], w[:, D:D + 256], w[:, D + 256:D + 512]
    pr = w[:, D + 512:D + 512 + 3360]
    gates = w[:, D + 512 + 3360:]
    rkv, lora = _split_shift_cols(pr)
    return jnp.concatenate([rkv, q, gates, k, v, lora], axis=1).astype(BF16)


def _rope_tables(pos):
    inv = ROPE_THETA ** (-jnp.arange(0, ROT, 2, dtype=F32) / ROT)
    ang = pos.astype(F32)[:, None] * inv[None, :]
    cos, sin = jnp.cos(ang), jnp.sin(ang)
    n = pos.shape[0]
    half = ROT // 2
    one = jnp.ones((n, HD - ROT), F32)
    zero = jnp.zeros((n, HD - half), F32)
    c = jnp.concatenate([cos, cos, one], 1)
    sa = jnp.concatenate([-sin, zero], 1)
    sb = jnp.concatenate([jnp.zeros((n, half), F32), sin, jnp.zeros((n, HD - ROT), F32)], 1)
    tile = lambda a: jnp.concatenate([a, a], 1)
    return tile(c), tile(sa), tile(sb)


def _pair_state_to_heads(s):
    even = s[:, :, :HD, :HD]
    odd = s[:, :, HD:, HD:]
    b = s.shape[0]
    return jnp.stack([even, odd], axis=2).reshape(b, 2 * N_PAIR, HD, HD)


def kernel(x_prompt, x_sample, cache_k_win, cache_v_win, state_wkv, state_shift, norm_mix_g, w_in, w_out, attn_sinks, shift_mu, decay_w0, decay_up, iclr_a0, iclr_up, gate_up, key_kk, key_ka, bonus_rk, lnx_g, lnx_b, norm_ffn_g, ffn_w_gate, ffn_w_up, ffn_w_down, moe_router, moe_w_gate, moe_w_up, moe_w_down, norm_final_g):
    batch, seq, _ = x_prompt.shape
    sb, st, _ = x_sample.shape
    depth = w_in.shape[0]
    n_p, n_s = batch * seq, sb * st
    n = n_p + n_s
    n_buf = cache_k_win.shape[2]
    if depth == 0:
        raise ValueError("depth must be positive")
    tm = next(c for c in (512, 256, 128) if n_p % c == 0 and n_s % c == 0 and seq % c == 0)
    npt = n_p // tm
    tm_in = 2 * tm if n_p % (2 * tm) == 0 and seq % (2 * tm) == 0 else tm
    n_in = -(-n // tm_in) * tm_in
    pad_rows = lambda a: jnp.pad(a, ((0, n_in - n_p - a.shape[0]), (0, 0)))

    x_p, x_s, s_base = x_prompt.reshape(n_p, D), pad_rows(x_sample.transpose(1, 0, 2).reshape(n_s, D)), 0
    rope_p = _rope_tables(jnp.arange(seq))
    rope_s = tuple(pad_rows(t) for t in _rope_tables(jnp.repeat(PAST_LEN + jnp.arange(st), sb)))
    row = lambda a: a.reshape(1, -1)

    new_p, new_s = [], []
    for l in range(depth):
        mu_rkv, mu_lora = _split_shift_cols(row(shift_mu[l]))
        g_up = jnp.concatenate([gate_up[l], jnp.zeros((G_PAD - LORA_G, D), F32)], 0)
        prm = dict(mu_rkv=mu_rkv, mu_lora=mu_lora, w0=row(decay_w0[l]), w_up=decay_up[l],
                   a0=row(iclr_a0[l]), a_up=iclr_up[l], g_up=g_up, k_k=row(key_kk[l]),
                   k_a=row(key_ka[l]), r_k=row(bonus_rk[l]), lnx_g=row(lnx_g[l]), lnx_b=row(lnx_b[l]))
        expand = lambda rows: {**prm, **{k_: _param_pm(prm[k_], rows) for k_ in ("k_k", "k_a", "r_k", "lnx_g", "lnx_b")}}
        prm_p, prm_s = expand(RWKV_CHUNK), expand(sb)
        p_all = _inproj(x_p, x_s, n_p // tm_in if l else 0, n_p // tm_in, n_in,
                        row(norm_mix_g[l]), _relayout_w_in(w_in[l]), rope_p, rope_s, seq, tm_in)

        att_p = _attn_prompt(p_all, attn_sinks[l], batch, seq)
        ps = p_all[n_p:n].reshape(st, sb, N_COLS).transpose(1, 0, 2)
        k_new, v_new = ps[..., C_K:C_K + 256], ps[..., C_V:C_V + 256]
        k_cache = cache_k_win[l].reshape(sb, n_buf, 256)
        v_cache = cache_v_win[l].reshape(sb, n_buf, 256)
        att_s = _attn_sample(ps[..., C_Q:C_Q + D], k_new, v_new, k_cache, v_cache, attn_sinks[l])
        att_s = pad_rows(att_s.transpose(1, 0, 2).reshape(n_s, D))

        rw_p, s_pairs = _rwkv_prompt(p_all, prm_p, batch, seq)
        sh_rkv, sh_lora = _split_shift_cols(state_shift[l])
        r_s, w_s, k_s, v_s, a_s, b_s, g_s = _rwkv_sample_pre(p_all, n_p, sh_rkv, sh_lora, prm_s, sb, st)
        s0 = state_wkv[l].transpose(1, 2, 3, 0)
        y_t, s_fin = _rwkv_sample_scan(s0, r_s.T, w_s.T, k_s.T, v_s.T[:, None, :], a_s.T, b_s.T)
        rw_s = pad_rows(_rwkv_sample_post(y_t[:, 0, :].T, r_s, k_s, v_s, g_s, prm_s, sb))

        is_moe = l % 2 == 1
        last = l == depth - 1
        gf = row(norm_final_g) if last else None
        if is_moe:
            ne = moe_router.shape[-1]
            router = jnp.concatenate([moe_router[l // 2], jnp.zeros((D, LANES - ne), F32)], 1)
            x_mid, h2, route = _outproj(att_p, att_s, rw_p, rw_s, p_all, x_p, x_s, s_base, w_out[l].astype(BF16),
                                        row(norm_ffn_g[l]), router, ne, tm)
            o_exp = _moe_experts(h2, _moe_plan(route, ne), moe_w_gate[l // 2].astype(BF16),
                                 moe_w_up[l // 2].astype(BF16), moe_w_down[l // 2].astype(BF16),
                                 _pick(moe_w_gate.shape[-1], (896, 512, 256, 128)))
            x = _moe_combine(x_mid, o_exp, route, gf, n_p, tm)
        else:
            x_mid, h2 = _outproj(att_p, att_s, rw_p, rw_s, p_all, x_p, x_s, s_base, w_out[l].astype(BF16),
                                 row(norm_ffn_g[l]), None, 0, tm)
            x = _ffn(h2, ffn_w_gate[l // 2].astype(BF16), ffn_w_up[l // 2].astype(BF16),
                     ffn_w_down[l // 2].astype(BF16), x_mid, gf, n_p, tm,
                     _pick(ffn_w_gate.shape[-1], (1408, 512, 256, 128)))
        if not last:
            x_p, x_s, s_base = x, x, npt

        n_win = min(WINDOW, seq)
        tail = lambda rows, c0, w_: jnp.stack(
            [lax.slice(p_all, ((b + 1) * seq - rows, c0), ((b + 1) * seq, c0 + w_)) for b in range(batch)])
        k_p = tail(n_win, C_K, N_KV * HD).reshape(batch, n_win, N_KV, HD)
        v_p = tail(n_win, C_V, N_KV * HD).reshape(batch, n_win, N_KV, HD)
        unsplit = lambda a: jnp.concatenate(
            [a[..., C_RKV:C_RKV + 3 * D], a[..., C_LORA:C_LORA + LORA_W + LORA_A + LORA_G]], -1)
        last_p = jnp.concatenate([tail(1, C_RKV, 3 * D), tail(1, C_LORA, LORA_W + LORA_A + LORA_G)], -1)[:, 0]
        new_p.append((k_p, v_p, _pair_state_to_heads(s_pairs), last_p))
        k_s_win = jnp.concatenate([k_cache, k_new], 1)[:, -n_buf:].reshape(sb, n_buf, N_KV, HD)
        v_s_win = jnp.concatenate([v_cache, v_new], 1)[:, -n_buf:].reshape(sb, n_buf, N_KV, HD)
        new_s.append((k_s_win, v_s_win, s_fin.transpose(3, 0, 1, 2), unsplit(ps[:, -1])))

    y_p, y_s = x
    y_p = y_p.reshape(batch, seq, D)
    y_s = y_s[:n_s].reshape(st, sb, D).transpose(1, 0, 2)
    stk = lambda sts, i: jnp.stack([s[i] for s in sts])
    return (y_p, y_s,
            stk(new_p, 0), stk(new_p, 1), stk(new_p, 2), stk(new_p, 3),
            stk(new_s, 0), stk(new_s, 1), stk(new_s, 2), stk(new_s, 3))
```

```python
import functools

import jax
import jax.numpy as jnp
from jax import lax
from jax.experimental import pallas as pl
from jax.experimental.pallas import tpu as pltpu

F32 = jnp.float32
BF16 = jnp.bfloat16
HIGHEST = lax.Precision.HIGHEST

LANES = 128
SUBLANES = 8
VMEM_LIMIT = 56 * 1024 * 1024

D = 1024
HD = 64
N_Q = 16
N_KV = 4
ROT = 16
ROPE_THETA = 500000.0
WINDOW = 128
RMS_EPS = 1e-5
LNX_EPS = 64e-5
N_PAIR = D // LANES
LORA_W, LORA_A, LORA_G = 64, 64, 160
LORA_PAD = 512
G_PAD = 256

C_RKV = 0
C_Q = 3072
C_GATE = 4096
C_K = 6144
C_V = 6400
C_LORA = 6656
N_COLS = 7168
TN_IN = 1024
Q_TILE = C_Q // TN_IN
KV_TILE = C_K // TN_IN

RWKV_CHUNK = 64


PAST_LEN = 16384


def _pick(n, cands):
    return next(c for c in cands if n % c == 0)


def _cparams(sem, **kw):
    return pltpu.CompilerParams(dimension_semantics=sem, vmem_limit_bytes=VMEM_LIMIT, **kw)


def _rms(x, g):
    return x * lax.rsqrt(jnp.mean(x * x, -1, keepdims=True) + RMS_EPS) * g


def _sigmoid(x):
    return 1.0 / (1.0 + jnp.exp(-x))


def _rope_chunk(a, c, sa, sb):
    return a * c + pltpu.roll(a, LANES - ROT // 2, 1) * sa + pltpu.roll(a, ROT // 2, 1) * sb


def _inproj_kernel(xp_ref, xs_ref, g_ref, w_ref, cp_ref, sap_ref, sbp_ref, cs_ref, sas_ref, sbs_ref,
                   o_ref, h_ref, *, n_prompt_tiles):
    i, j = pl.program_id(0), pl.program_id(1)
    is_prompt = i < n_prompt_tiles

    @pl.when(j == 0)
    def _():
        x = jnp.where(is_prompt, xp_ref[...], xs_ref[...])
        h_ref[...] = _rms(x, g_ref[...]).astype(BF16)

    acc = jnp.dot(h_ref[...], w_ref[...], preferred_element_type=F32)

    def roped(n_chunks):
        c = jnp.where(is_prompt, cp_ref[...], cs_ref[...])
        sa = jnp.where(is_prompt, sap_ref[...], sas_ref[...])
        sb = jnp.where(is_prompt, sbp_ref[...], sbs_ref[...])
        parts = [_rope_chunk(acc[:, k * LANES:(k + 1) * LANES], c, sa, sb) for k in range(n_chunks)]
        if n_chunks * LANES < TN_IN:
            parts.append(acc[:, n_chunks * LANES:])
        return jnp.concatenate(parts, axis=1)

    @pl.when(j == Q_TILE)
    def _():
        o_ref[...] = roped(TN_IN // LANES)

    @pl.when(j == KV_TILE)
    def _():
        o_ref[...] = roped(N_KV * HD // LANES)

    @pl.when((j != Q_TILE) & (j != KV_TILE))
    def _():
        o_ref[...] = acc


def _two_part_specs(block, npt, s_base, period=None):
    def p_map(i, *_):
        ip = jnp.minimum(i, npt - 1)
        return (ip % period if period else ip, 0)

    def s_map(i, *_):
        return (s_base + jnp.maximum(i - npt, 0), 0)

    return pl.BlockSpec(block, p_map), pl.BlockSpec(block, s_map)


def _inproj(x_p, x_s, s_base, npt, n, g, w, rope_p, rope_s, seq, tm):
    xp_spec, xs_spec = _two_part_specs((tm, D), npt, s_base)
    rp_spec, rs_spec = _two_part_specs((tm, LANES), npt, 0, period=seq // tm)
    return pl.pallas_call(
        functools.partial(_inproj_kernel, n_prompt_tiles=npt),
        out_shape=jax.ShapeDtypeStruct((n, N_COLS), F32),
        grid=(n // tm, N_COLS // TN_IN),
        in_specs=[xp_spec, xs_spec,
                  pl.BlockSpec((1, D), lambda i, j: (0, 0)),
                  pl.BlockSpec((D, TN_IN), lambda i, j: (0, j)),
                  rp_spec, rp_spec, rp_spec, rs_spec, rs_spec, rs_spec],
        out_specs=pl.BlockSpec((tm, TN_IN), lambda i, j: (i, j)),
        scratch_shapes=[pltpu.VMEM((tm, D), BF16)],
        compiler_params=_cparams(("parallel", "arbitrary")),
        name="inproj",
    )(x_p, x_s, g, w, *rope_p, *rope_s)


def _sink_softmax(s, mask, sink):
    s = jnp.where(mask, s * (HD ** -0.5), -jnp.inf)
    m = jnp.maximum(jnp.max(s, -1, keepdims=True), sink)
    p = jnp.exp(s - m)
    return p / (jnp.sum(p, -1, keepdims=True) + jnp.exp(sink - m))


def _dot_nt(a, b, **kw):
    return lax.dot_general(a, b, (((1,), (1,)), ((), ())), preferred_element_type=F32, **kw)


def _dot_tn(a, b, **kw):
    return lax.dot_general(a, b, (((0,), (0,)), ((), ())), preferred_element_type=F32, **kw)


def _head_operands(x, half):
    lo = lax.broadcasted_iota(jnp.int32, x.shape, 1) < HD
    sw = pltpu.roll(x, HD, 1)
    x_lo, x_hi = (x, sw) if half == 0 else (sw, x)
    return jnp.where(lo, x_lo, 0.0).astype(BF16), jnp.where(lo, 0.0, x_hi).astype(BF16)


def _group_queries(q, g):
    return jnp.concatenate([q[:, (2 * g) * LANES:(2 * g + 1) * LANES],
                            q[:, (2 * g + 1) * LANES:(2 * g + 2) * LANES]], axis=0)


def _group_sinks(sink_ref, g, top):
    return (jnp.where(top, sink_ref[4 * g], sink_ref[4 * g + 2]),
            jnp.where(top, sink_ref[4 * g + 1], sink_ref[4 * g + 3]))


def _attn_prompt_kernel(sink_ref, q_ref, kp_ref, kc_ref, vp_ref, vc_ref, o_ref):
    blk = pl.program_id(1)
    w = WINDOW
    q = q_ref[...].astype(BF16)
    k = jnp.concatenate([kp_ref[...], kc_ref[...]], axis=0)
    v = jnp.concatenate([vp_ref[...], vc_ref[...]], axis=0)
    qi = lax.broadcasted_iota(jnp.int32, (w, w), 0)
    kj = lax.broadcasted_iota(jnp.int32, (w, w), 1)
    band = jnp.concatenate([kj <= qi] * 2, axis=0)
    mask = band | (blk > 0)
    top = lax.broadcasted_iota(jnp.int32, (2 * w, 1), 0) < w
    outs = []
    for g in range(N_KV):
        ch, half = divmod(g, 2)
        q2 = _group_queries(q, g)
        o = 0.0
        for k_, v_, sink in zip(_head_operands(k[:, ch * LANES:(ch + 1) * LANES], half),
                                _head_operands(v[:, ch * LANES:(ch + 1) * LANES], half),
                                _group_sinks(sink_ref, g, top)):
            s = _dot_nt(q2, k_)
            p = _sink_softmax(jnp.where(band, s[:, w:], s[:, :w]), mask, sink)
            p_cat = jnp.concatenate([jnp.where(band, 0.0, p), jnp.where(band, p, 0.0)], axis=1)
            o = o + jnp.dot(p_cat.astype(BF16), v_, preferred_element_type=F32)
        outs += [o[:w], o[w:]]
    o_ref[...] = jnp.concatenate(outs, axis=1).astype(o_ref.dtype)


def _attn_prompt(p_all, sinks, batch, seq):
    nb = seq // WINDOW
    kcol, vcol = C_K // (N_KV * HD), C_V // (N_KV * HD)
    cur = lambda b, i: b * nb + i
    prev = lambda b, i: b * nb + jnp.maximum(i - 1, 0)
    return pl.pallas_call(
        _attn_prompt_kernel,
        out_shape=jax.ShapeDtypeStruct((batch * seq, D), BF16),
        grid=(batch, nb),
        in_specs=[
            pl.BlockSpec(memory_space=pltpu.SMEM),
            pl.BlockSpec((WINDOW, D), lambda b, i: (cur(b, i), C_Q // D)),
            pl.BlockSpec((WINDOW, N_KV * HD), lambda b, i: (prev(b, i), kcol)),
            pl.BlockSpec((WINDOW, N_KV * HD), lambda b, i: (cur(b, i), kcol)),
            pl.BlockSpec((WINDOW, N_KV * HD), lambda b, i: (prev(b, i), vcol)),
            pl.BlockSpec((WINDOW, N_KV * HD), lambda b, i: (cur(b, i), vcol)),
        ],
        out_specs=pl.BlockSpec((WINDOW, D), lambda b, i: (cur(b, i), 0)),
        compiler_params=_cparams(("parallel", "arbitrary")),
        name="attn_prompt",
    )(sinks, p_all, p_all, p_all, p_all, p_all)


ATT_S_BT = 8
T_PAD = 8


def _attn_sample_kernel(sink_ref, q_ref, kn_ref, vn_ref, kc_ref, vc_ref, o_ref):
    tn = q_ref.shape[1]
    nbuf = kc_ref.shape[1]
    rows = 2 * tn
    keys = nbuf + T_PAD
    r = lax.broadcasted_iota(jnp.int32, (rows, keys), 0)
    t = jnp.where(r >= tn, r - tn, r)
    kj = lax.broadcasted_iota(jnp.int32, (rows, keys), 1)
    mask = (kj > t + (nbuf - WINDOW)) & (kj <= t + nbuf)
    top = lax.broadcasted_iota(jnp.int32, (rows, 1), 0) < tn
    zpad = jnp.zeros((T_PAD - tn, N_KV * HD), F32)
    scores, sinks, values = [], [], []
    for b in range(ATT_S_BT):
        q = q_ref[b].astype(BF16)
        k = jnp.concatenate([kc_ref[b], kn_ref[b], zpad], axis=0)
        v = jnp.concatenate([vc_ref[b], vn_ref[b], zpad], axis=0)
        for g in range(N_KV):
            ch, half = divmod(g, 2)
            q2 = _group_queries(q, g)
            values += list(_head_operands(v[:, ch * LANES:(ch + 1) * LANES], half))
            sinks += list(_group_sinks(sink_ref, g, top))
            scores += [_dot_nt(q2, k_) for k_ in _head_operands(k[:, ch * LANES:(ch + 1) * LANES], half)]
    p = _sink_softmax(jnp.stack(scores), mask[None], jnp.stack(sinks)).astype(BF16)
    for b in range(ATT_S_BT):
        outs = []
        for g in range(N_KV):
            i = 2 * (b * N_KV + g)
            o = (jnp.dot(p[i], values[i], preferred_element_type=F32)
                 + jnp.dot(p[i + 1], values[i + 1], preferred_element_type=F32))
            outs += [o[:tn], o[tn:]]
        o_ref[b] = jnp.concatenate(outs, axis=1).astype(o_ref.dtype)


def _attn_sample(q, k_new, v_new, k_cache, v_cache, sinks):
    bsz, tn, _ = q.shape
    nbuf = k_cache.shape[1]
    kvw = N_KV * HD
    blk = lambda w_, r_: pl.BlockSpec((ATT_S_BT, r_, w_), lambda i: (i, 0, 0))
    return pl.pallas_call(
        _attn_sample_kernel,
        out_shape=jax.ShapeDtypeStruct((bsz, tn, D), BF16),
        grid=(bsz // ATT_S_BT,),
        in_specs=[pl.BlockSpec(memory_space=pltpu.SMEM),
                  blk(D, tn), blk(kvw, tn), blk(kvw, tn), blk(kvw, nbuf), blk(kvw, nbuf)],
        out_specs=blk(D, tn),
        compiler_params=_cparams(("parallel",)),
        name="attn_sample",
    )(sinks, q, k_new, v_new, k_cache, v_cache)


def _to_pm(x):
    return jnp.concatenate([x[:, p * LANES:(p + 1) * LANES] for p in range(N_PAIR)], axis=0)


def _from_pm(x):
    r = x.shape[0] // N_PAIR
    return jnp.concatenate([x[p * r:(p + 1) * r] for p in range(N_PAIR)], axis=1)


def _param_pm(v, r):
    if v.shape == (N_PAIR * r, LANES):
        return v
    return jnp.concatenate(
        [jnp.broadcast_to(v[:, p * LANES:(p + 1) * LANES], (r, LANES)) for p in range(N_PAIR)], axis=0)


def _head_sum(x, ones_bd):
    hi = x.astype(BF16)
    out = jnp.dot(hi, ones_bd, preferred_element_type=F32)
    if _PASSES["head_sum"] == 2:
        lo = (x - hi.astype(F32)).astype(BF16)
        out = out + jnp.dot(lo, ones_bd, preferred_element_type=F32)
    return out


def _ones_bd():
    r = lax.broadcasted_iota(jnp.int32, (LANES, LANES), 0) // HD
    c = lax.broadcasted_iota(jnp.int32, (LANES, LANES), 1) // HD
    return jnp.where(r == c, 1.0, 0.0).astype(BF16)


def _softplus(z):
    return jnp.maximum(z, 0.0) + jnp.log(1.0 + jnp.exp(-jnp.abs(z)))


def _rwkv_pre(p_rkv, p_lora, prev_rkv, prev_lora, prm):
    rows = p_rkv.shape[0]
    xs = p_rkv + (prev_rkv - p_rkv) * prm["mu_rkv"]
    xl = p_lora + (prev_lora - p_lora) * prm["mu_lora"]
    wd = xl[:, 0:LORA_W]
    ad = xl[:, LORA_W:LORA_W + LORA_A]
    gd = xl[:, LANES:LANES + G_PAD]
    mm = (((1,), (0,)), ((), ()))
    w_pre = prm["w0"] + _pdot(jnp.tanh(wd), prm["w_up"], mm, _PASSES["lora_w"])
    a_pre = prm["a0"] + _pdot(ad, prm["a_up"], mm, _PASSES["lora_a"])
    g = _pdot(_sigmoid(gd), prm["g_up"], mm, _PASSES["lora_g"])
    logw = -jnp.exp(-_softplus(-w_pre) - 0.5)
    a = _to_pm(_sigmoid(a_pre))
    r = _to_pm(xs[:, 0:D])
    k = _to_pm(xs[:, D:2 * D])
    v = _to_pm(xs[:, 2 * D:3 * D])
    kk = k * _param_pm(prm["k_k"], rows)
    nrm = jnp.sqrt(_head_sum(kk * kk, _ones_bd()))
    kk = kk / jnp.maximum(nrm, 1e-12)
    k = k * (1.0 + (a - 1.0) * _param_pm(prm["k_a"], rows))
    return r, k, v, -kk, kk * a, _to_pm(logw), _to_pm(g)


def _rwkv_post(y, r, k, v, g, prm):
    rows = y.shape[0] // N_PAIR
    ones_bd = _ones_bd()
    mean = _head_sum(y, ones_bd) * (1.0 / HD)
    yc = y - mean
    var = _head_sum(yc * yc, ones_bd) * (1.0 / HD)
    yn = yc * lax.rsqrt(var + LNX_EPS) * _param_pm(prm["lnx_g"], rows) + _param_pm(prm["lnx_b"], rows)
    bonus = _head_sum(r * k * _param_pm(prm["r_k"], rows), ones_bd) * v
    return (yn + bonus) * g


_PRM_NAMES = ("mu_rkv", "mu_lora", "w0", "w_up", "a0", "a_up", "g_up", "k_k", "k_a", "r_k", "lnx_g", "lnx_b")


def _prm_specs(prm, n_grid):
    zero = lambda *_: (0, 0)
    return [pl.BlockSpec(prm[n].shape, zero) for n in _PRM_NAMES]


def _stack2(x, lo):
    return jnp.concatenate([jnp.where(lo, x, 0.0), jnp.where(lo, 0.0, x)], axis=1)


def _split_bf16(x):
    hi = x.astype(BF16)
    return hi, (x - hi.astype(F32)).astype(BF16)


def _pdot(a, b, dims, passes):
    if passes == 6:
        return lax.dot_general(a, b, dims, preferred_element_type=F32, precision=HIGHEST)
    dot = lambda x, y: lax.dot_general(x, y, dims, preferred_element_type=F32)
    if passes == 1:
        return dot(a.astype(BF16), b.astype(BF16))
    a_hi, a_lo = _split_bf16(a)
    b_hi, b_lo = _split_bf16(b)
    return dot(a_hi, b_hi) + (dot(a_hi, b_lo) + dot(a_lo, b_hi))


_PASSES = dict(lora_w=1, lora_a=1, lora_g=1, gram=1, inv=1, rhs=1, u=1, y=1, state=1, head_sum=1)


def _bdot(a, b, ca, cb, site):
    return _pdot(a, b, (((ca,), (cb,)), ((0,), (0,))), _PASSES[site])


def _bmm(a, b, site):
    return _bdot(a, b, 2, 1, site)


def _bmm_nt(a, b, site):
    return _bdot(a, b, 2, 2, site)


def _bmm_tn(a, b, site):
    return _bdot(a, b, 1, 1, site)


def _rwkv_chunk(s, r, k, v, a_, b_, logw):
    c = r.shape[1]
    ti = lax.broadcasted_iota(jnp.int32, (N_PAIR, c, c), 1)
    si = lax.broadcasted_iota(jnp.int32, (N_PAIR, c, c), 2)
    tri = jnp.where(si <= ti, 1.0, 0.0).astype(BF16)
    w_hi, w_lo = _split_bf16(logw)
    cdot = lambda x: lax.dot_general(tri, x, (((2,), (1,)), ((0,), (0,))), preferred_element_type=F32)
    cum = cdot(w_hi) + cdot(w_lo)
    e_neg = jnp.exp(-cum)
    l_end = cum[:, c - 1:c, :]
    e_end = jnp.exp(l_end)
    assert c == HD, "the lane split of (C, 2C) time matrices reuses the head mask"
    lo = lax.broadcasted_iota(jnp.int32, (N_PAIR, c, LANES), 2) < HD
    at = a_ * jnp.exp(cum - logw)
    rt = r * jnp.exp(cum)
    yb, yk = _stack2(b_ * e_neg, lo), _stack2(k * e_neg, lo)
    vs = _stack2(v, lo)
    gmat = _bmm_nt(jnp.concatenate([at, rt], axis=1), jnp.concatenate([yb, yk], axis=1), "gram")
    n2 = 2 * c
    tr = lax.broadcasted_iota(jnp.int32, (N_PAIR, c, n2), 1)
    tc = lax.broadcasted_iota(jnp.int32, (N_PAIR, c, n2), 2) & (c - 1)
    strict, incl = tr > tc, tr >= tc
    a_ab = jnp.where(strict, gmat[:, :c, :n2], 0.0)
    a_ak = jnp.where(strict, gmat[:, :c, n2:], 0.0)
    a_rb = jnp.where(incl, gmat[:, c:, :n2], 0.0)
    a_rk = jnp.where(incl, gmat[:, c:, n2:], 0.0)
    tinv = jnp.where(tr == tc, 1.0, 0.0) + a_ab
    pw = _bmm(a_ab, _stack2(a_ab, lo), "inv")
    d = 4
    while d < c:
        res = _bmm(jnp.concatenate([tinv, pw], axis=1), _stack2(pw, lo), "inv")
        tinv, pw = tinv + res[:, :c], res[:, c:]
        d *= 2
    tinv = tinv + _bmm(tinv, _stack2(pw, lo), "inv")
    rhs = _bmm_nt(at, s, "rhs") + _bmm(a_ak, vs, "rhs")
    us = _stack2(_bmm(tinv, _stack2(rhs, lo), "u"), lo)
    uv = jnp.concatenate([us, vs], axis=1)
    y = _bmm_nt(rt, s, "y") + _bmm(jnp.concatenate([a_rb, a_rk], axis=2), uv, "y")
    s_new = s * e_end + _bmm_tn(uv, jnp.concatenate([yb * e_end, yk * e_end], axis=1), "state")
    return s_new, y


def _rwkv_prompt_kernel(prkv_ref, plora_ref, *rest):
    prm_refs = rest[:len(_PRM_NAMES)]
    o_ref, s_out_ref, s_ref, carry_rkv, carry_lora = rest[len(_PRM_NAMES):]
    ci = pl.program_id(1)
    c = RWKV_CHUNK

    @pl.when(ci == 0)
    def _():
        s_ref[...] = jnp.zeros_like(s_ref)
        carry_rkv[...] = jnp.zeros_like(carry_rkv)
        carry_lora[...] = jnp.zeros_like(carry_lora)

    prm = {n: ref[...] for n, ref in zip(_PRM_NAMES, prm_refs)}

    def shifted(x, last):
        first = lax.broadcasted_iota(jnp.int32, x.shape, 0) == 0
        return jnp.where(first, last, pltpu.roll(x, 1, 0))

    s = s_ref[...]
    last_rkv, last_lora = carry_rkv[0:1, :], carry_lora[0:1, :]
    sh = (N_PAIR, c, LANES)
    for i in range(prkv_ref.shape[0] // c):
        rows = slice(i * c, (i + 1) * c)
        p_rkv, p_lora = prkv_ref[rows, :], plora_ref[rows, :]
        r, k, v, a_, b_, logw, g = _rwkv_pre(p_rkv, p_lora, shifted(p_rkv, last_rkv),
                                             shifted(p_lora, last_lora), prm)
        last_rkv, last_lora = p_rkv[c - 1:c, :], p_lora[c - 1:c, :]
        s, y = _rwkv_chunk(s, r.reshape(sh), k.reshape(sh), v.reshape(sh),
                           a_.reshape(sh), b_.reshape(sh), logw.reshape(sh))
        out = _rwkv_post(y.reshape(N_PAIR * c, LANES), r, k, v, g, prm)
        o_ref[rows, :] = _from_pm(out).astype(o_ref.dtype)
    s_ref[...] = s
    carry_rkv[0:1, :] = last_rkv
    carry_lora[0:1, :] = last_lora

    @pl.when(ci == pl.num_programs(1) - 1)
    def _():
        s_out_ref[0] = s


RWKV_BLOCK = 4 * RWKV_CHUNK


def _rwkv_prompt(p_all, prm, batch, seq):
    c = RWKV_BLOCK if seq % RWKV_BLOCK == 0 else RWKV_CHUNK
    nc = seq // c
    row = lambda b, i: b * nc + i
    return pl.pallas_call(
        _rwkv_prompt_kernel,
        out_shape=(jax.ShapeDtypeStruct((batch * seq, D), BF16),
                   jax.ShapeDtypeStruct((batch, N_PAIR, LANES, LANES), F32)),
        grid=(batch, nc),
        in_specs=[pl.BlockSpec((c, 3 * D), lambda b, i: (row(b, i), C_RKV // (3 * D))),
                  pl.BlockSpec((c, LORA_PAD), lambda b, i: (row(b, i), C_LORA // LORA_PAD))]
                 + _prm_specs(prm, 2),
        out_specs=(pl.BlockSpec((c, D), lambda b, i: (row(b, i), 0)),
                   pl.BlockSpec((1, N_PAIR, LANES, LANES), lambda b, i: (b, 0, 0, 0))),
        scratch_shapes=[pltpu.VMEM((N_PAIR, LANES, LANES), F32),
                        pltpu.VMEM((8, 3 * D), F32),
                        pltpu.VMEM((8, LORA_PAD), F32)],
        compiler_params=_cparams(("parallel", "arbitrary")),
        name="rwkv_prompt",
    )(p_all, p_all, *[prm[n] for n in _PRM_NAMES])


def _rwkv_sample_pre_kernel(prkv_ref, plora_ref, qrkv_ref, qlora_ref, srkv_ref, slora_ref, *rest):
    prm_refs = rest[:len(_PRM_NAMES)]
    outs = rest[len(_PRM_NAMES):]
    t = pl.program_id(0)
    prm = {n: ref[...] for n, ref in zip(_PRM_NAMES, prm_refs)}
    first = t == 0
    prev_rkv = jnp.where(first, srkv_ref[...], qrkv_ref[...])
    prev_lora = jnp.where(first, slora_ref[...], qlora_ref[...])
    r, k, v, a_, b_, logw, g = _rwkv_pre(prkv_ref[...], plora_ref[...], prev_rkv, prev_lora, prm)
    for ref, val in zip(outs, (r, jnp.exp(logw), k, v, a_, b_, g)):
        ref[...] = _from_pm(val)


def _rwkv_sample_pre(p_all, row0, shift_rkv, shift_lora, prm, bsz, tn):
    base = row0 // bsz
    cur = lambda t: base + t
    prv = lambda t: base + jnp.maximum(t - 1, 0)
    out = jax.ShapeDtypeStruct((tn * bsz, D), F32)
    return pl.pallas_call(
        _rwkv_sample_pre_kernel,
        out_shape=(out,) * 7,
        grid=(tn,),
        in_specs=[pl.BlockSpec((bsz, 3 * D), lambda t: (cur(t), C_RKV // (3 * D))),
                  pl.BlockSpec((bsz, LORA_PAD), lambda t: (cur(t), C_LORA // LORA_PAD)),
                  pl.BlockSpec((bsz, 3 * D), lambda t: (prv(t), C_RKV // (3 * D))),
                  pl.BlockSpec((bsz, LORA_PAD), lambda t: (prv(t), C_LORA // LORA_PAD)),
                  pl.BlockSpec((bsz, 3 * D), lambda t: (0, 0)),
                  pl.BlockSpec((bsz, LORA_PAD), lambda t: (0, 0))]
                 + _prm_specs(prm, 1),
        out_specs=tuple(pl.BlockSpec((bsz, D), lambda t: (t, 0)) for _ in range(7)),
        compiler_params=_cparams(("arbitrary",)),
        name="rwkv_sample_pre",
    )(p_all, p_all, p_all, p_all, shift_rkv, shift_lora, *[prm[n] for n in _PRM_NAMES])


def _rwkv_sample_scan_kernel(s0_ref, r_ref, w_ref, k_ref, v_ref, a_ref, b_ref, y_ref, so_ref, s_ref):
    bsz = s0_ref.shape[-1]
    tn = r_ref.shape[1] // bsz
    s_ref[...] = s0_ref[0]
    for t in range(tn):
        cols = slice(t * bsz, (t + 1) * bsz)
        r_t, w_t, k_t = r_ref[:, cols], w_ref[:, cols], k_ref[:, cols]
        a_t, b_t = a_ref[:, cols], b_ref[:, cols]

        def body(i, _):
            s_i = s_ref[i]
            sa = jnp.sum(s_i * a_t, axis=0, keepdims=True)
            v_i = v_ref[i, :, cols]
            s_i = s_i * w_t + sa * b_t + v_i * k_t
            s_ref[i] = s_i
            y_ref[i, :, cols] = jnp.sum(s_i * r_t, axis=0, keepdims=True)
            return 0

        lax.fori_loop(0, HD, body, 0)
    so_ref[0] = s_ref[...]


def _rwkv_sample_scan(s0, r, w, k, v, a_, b_):
    nh, _, _, bsz = s0.shape
    tb = r.shape[1]
    vec = pl.BlockSpec((HD, tb), lambda h: (h, 0))
    vec3 = pl.BlockSpec((HD, 1, tb), lambda h: (h, 0, 0))
    st = pl.BlockSpec((1, HD, HD, bsz), lambda h: (h, 0, 0, 0))
    return pl.pallas_call(
        _rwkv_sample_scan_kernel,
        out_shape=(jax.ShapeDtypeStruct((nh * HD, 1, tb), F32), jax.ShapeDtypeStruct(s0.shape, F32)),
        grid=(nh,),
        in_specs=[st, vec, vec, vec, vec3, vec, vec],
        out_specs=(vec3, st),
        scratch_shapes=[pltpu.VMEM((HD, HD, bsz), F32)],
        compiler_params=_cparams(("parallel",)),
        name="rwkv_sample_scan",
    )(s0, r, w, k, v, a_, b_)


def _rwkv_sample_post_kernel(y_ref, r_ref, k_ref, v_ref, g_ref, *rest):
    prm_refs = rest[:len(_PRM_NAMES)]
    o_ref = rest[len(_PRM_NAMES)]
    prm = {n: ref[...] for n, ref in zip(_PRM_NAMES, prm_refs)}
    out = _rwkv_post(_to_pm(y_ref[...]), _to_pm(r_ref[...]), _to_pm(k_ref[...]),
                     _to_pm(v_ref[...]), _to_pm(g_ref[...]), prm)
    o_ref[...] = _from_pm(out).astype(o_ref.dtype)


def _rwkv_sample_post(y, r, k, v, g, prm, bsz):
    n = y.shape[0]
    blk = pl.BlockSpec((bsz, D), lambda t: (t, 0))
    return pl.pallas_call(
        _rwkv_sample_post_kernel,
        out_shape=jax.ShapeDtypeStruct((n, D), BF16),
        grid=(n // bsz,),
        in_specs=[blk] * 5 + _prm_specs(prm, 1),
        out_specs=blk,
        compiler_params=_cparams(("parallel",)),
        name="rwkv_sample_post",
    )(y, r, k, v, g, *[prm[n] for n in _PRM_NAMES])


N_CHUNK = D // LANES


def _store_token_major(ref, x, lead=()):
    rows = x.shape[0]
    for c in range(N_CHUNK):
        ref[lead + (pl.ds(c, rows, stride=N_CHUNK), slice(None))] = x[:, c * LANES:(c + 1) * LANES]


def _load_token_major(ref, rows, lead=()):
    return jnp.concatenate(
        [ref[lead + (pl.ds(c, rows, stride=N_CHUNK), slice(None))] for c in range(N_CHUNK)], axis=1)


def _outproj_kernel(attp_ref, atts_ref, rwp_ref, rws_ref, ga_ref, gr_ref, xp_ref, xs_ref, w_ref, g2_ref, *rest,
                    n_experts, n_prompt_tiles):
    is_prompt = pl.program_id(0) < n_prompt_tiles
    att = jnp.where(is_prompt, attp_ref[...], atts_ref[...]).astype(F32)
    rw = jnp.where(is_prompt, rwp_ref[...], rws_ref[...]).astype(F32)
    x = jnp.where(is_prompt, xp_ref[...], xs_ref[...])
    m = _sigmoid(ga_ref[...]) * att + _sigmoid(gr_ref[...]) * rw
    xn = x + jnp.dot(m.astype(BF16), w_ref[...], preferred_element_type=F32)
    h2 = _rms(xn, g2_ref[...])
    if n_experts:
        router_ref, xo_ref, h2_ref, gate_ref = rest
        logits = _pdot(h2, router_ref[...], (((1,), (0,)), ((), ())), 3)
        lane = lax.broadcasted_iota(jnp.int32, logits.shape, 1).astype(F32)
        lg = jnp.where(lane < n_experts, logits, -jnp.inf)
        v1 = jnp.max(lg, -1, keepdims=True)
        i1 = jnp.min(jnp.where(lg == v1, lane, float(LANES)), -1, keepdims=True)
        lg2 = jnp.where(lane == i1, -jnp.inf, lg)
        v2 = jnp.max(lg2, -1, keepdims=True)
        i2 = jnp.min(jnp.where(lg2 == v2, lane, float(LANES)), -1, keepdims=True)
        e2 = jnp.exp(v2 - v1)
        den = 1.0 + e2
        gate_ref[...] = (jnp.where(lane == 0.0, i1, 0.0) + jnp.where(lane == 1.0, i2, 0.0)
                         + jnp.where(lane == 2.0, 1.0 / den, 0.0) + jnp.where(lane == 3.0, e2 / den, 0.0))
        _store_token_major(h2_ref, h2)
    else:
        xo_ref, h2_ref = rest
        h2_ref[...] = h2.astype(BF16)
    xo_ref[...] = xn


def _outproj(att_p, att_s, rw_p, rw_s, p_all, x_p, x_s, s_base, w_out, g2, router, n_experts, tm):
    n = att_p.shape[0] + att_s.shape[0]
    npt = att_p.shape[0] // tm
    row = lambda w_: pl.BlockSpec((tm, w_), lambda i: (i, 0))
    p_spec, s_spec = _two_part_specs((tm, D), npt, 0)
    xp_spec, xs_spec = _two_part_specs((tm, D), npt, s_base)
    in_specs = [p_spec, s_spec, p_spec, s_spec,
                pl.BlockSpec((tm, D), lambda i: (i, C_GATE // D)),
                pl.BlockSpec((tm, D), lambda i: (i, C_GATE // D + 1)),
                xp_spec, xs_spec,
                pl.BlockSpec((D, D), lambda i: (0, 0)),
                pl.BlockSpec((1, D), lambda i: (0, 0))]
    args = [att_p, att_s, rw_p, rw_s, p_all, p_all, x_p, x_s, w_out, g2]
    if n_experts:
        h2_shape = jax.ShapeDtypeStruct((n * N_CHUNK, LANES), F32)
        h2_spec = pl.BlockSpec((tm * N_CHUNK, LANES), lambda i: (i, 0))
    else:
        h2_shape, h2_spec = jax.ShapeDtypeStruct((n, D), BF16), row(D)
    out_shape = [jax.ShapeDtypeStruct((n, D), F32), h2_shape]
    out_specs = [row(D), h2_spec]
    if n_experts:
        in_specs.append(pl.BlockSpec((D, LANES), lambda i: (0, 0)))
        args.append(router)
        out_shape.append(jax.ShapeDtypeStruct((n, LANES), F32))
        out_specs.append(row(LANES))
    return pl.pallas_call(
        functools.partial(_outproj_kernel, n_experts=n_experts, n_prompt_tiles=npt),
        out_shape=tuple(out_shape),
        grid=(n // tm,),
        in_specs=in_specs,
        out_specs=tuple(out_specs),
        compiler_params=_cparams(("parallel",)),
        name="outproj",
    )(*args)


def _swiglu_part(h, wg, wu, wd):
    a = jnp.dot(h, wg, preferred_element_type=F32)
    b = jnp.dot(h, wu, preferred_element_type=F32)
    t = (a * _sigmoid(a) * b).astype(BF16)
    return jnp.dot(t, wd, preferred_element_type=F32)


def _final_store(y, gf_ref, out_refs, n_prompt_tiles):
    if gf_ref is None:
        (o_ref,) = out_refs
        o_ref[...] = y
        return
    op_ref, os_ref = out_refs
    y = _rms(y, gf_ref[...])
    is_prompt = pl.program_id(0) < n_prompt_tiles

    @pl.when(is_prompt)
    def _():
        op_ref[...] = y

    @pl.when(jnp.logical_not(is_prompt))
    def _():
        os_ref[...] = y


def _final_out(n, n_p, tm, final):
    if not final:
        return jax.ShapeDtypeStruct((n, D), F32), pl.BlockSpec((tm, D), lambda i, *_: (i, 0))
    return ((jax.ShapeDtypeStruct((n_p, D), F32), jax.ShapeDtypeStruct((n - n_p, D), F32)),
            _two_part_specs((tm, D), n_p // tm, 0))


def _ffn_kernel(h_ref, wg_ref, wu_ref, wd_ref, x_ref, *rest, final, n_prompt_tiles):
    rest = list(rest)
    gf_ref = rest.pop(0) if final else None
    acc_ref = rest.pop()
    f = pl.program_id(1)

    @pl.when(f == 0)
    def _():
        acc_ref[...] = jnp.zeros_like(acc_ref)

    acc_ref[...] += _swiglu_part(h_ref[...], wg_ref[...], wu_ref[...], wd_ref[...])

    @pl.when(f == pl.num_programs(1) - 1)
    def _():
        _final_store(x_ref[...] + acc_ref[...], gf_ref, rest, n_prompt_tiles)


def _ffn(h, wg, wu, wd, x, gf, n_p, tm, tf):
    n = x.shape[0]
    fdim = wg.shape[1]
    final = gf is not None
    in_specs = [pl.BlockSpec((tm, D), lambda i, f: (i, 0)),
                pl.BlockSpec((D, tf), lambda i, f: (0, f)),
                pl.BlockSpec((D, tf), lambda i, f: (0, f)),
                pl.BlockSpec((tf, D), lambda i, f: (f, 0)),
                pl.BlockSpec((tm, D), lambda i, f: (i, 0))]
    args = [h, wg, wu, wd, x]
    if final:
        in_specs.append(pl.BlockSpec((1, D), lambda i, f: (0, 0)))
        args.append(gf)
    out_shape, out_specs = _final_out(n, n_p, tm, final)
    return pl.pallas_call(
        functools.partial(_ffn_kernel, final=final, n_prompt_tiles=n_p // tm),
        out_shape=out_shape,
        grid=(n // tm, fdim // tf),
        in_specs=in_specs,
        out_specs=out_specs,
        scratch_shapes=[pltpu.VMEM((tm, D), F32)],
        compiler_params=_cparams(("arbitrary" if final else "parallel", "arbitrary")),
        name="ffn",
    )(*args)


MOE_TM = 512


def _moe_plan(route, n_experts):
    n = route.shape[0]
    e_flat = route[:, :2].astype(jnp.int32).T.reshape(-1)
    n_asg = e_flat.shape[0]
    order = jnp.argsort(e_flat, stable=True).astype(jnp.int32)
    counts = jnp.sum((e_flat[:, None] == jnp.arange(n_experts)[None, :]).astype(jnp.int32), axis=0)
    first = jnp.cumsum(counts) - counts
    padded = (counts + MOE_TM - 1) // MOE_TM * MOE_TM
    ends = jnp.cumsum(padded)
    offs = ends - padded
    n_tiles = -(-n_asg // MOE_TM) + n_experts
    start = jnp.arange(n_tiles, dtype=jnp.int32) * MOE_TM
    n_active = ends[-1] // MOE_TM
    tile_e = jnp.sum((start[:, None] >= ends[None, :]).astype(jnp.int32), axis=1)
    last_e = jnp.sum(((n_active - 1) * MOE_TM >= ends).astype(jnp.int32))
    tile_e = jnp.minimum(tile_e, last_e)
    n_valid = jnp.clip(counts[tile_e] - (start - offs[tile_e]), 0, MOE_TM)
    n_valid = jnp.where(start < ends[-1], n_valid, 0)
    r_in_tile = jnp.arange(MOE_TM, dtype=jnp.int32)[None, :]
    src = (first[tile_e] + start - offs[tile_e])[:, None] + r_in_tile
    real = r_in_tile < n_valid[:, None]
    asg = jnp.where(real, order[jnp.where(real, src, 0)], 0)
    tok = jnp.where(asg >= n, asg - n, asg).reshape(-1)
    dump = jnp.broadcast_to(n_asg + r_in_tile, asg.shape)
    dst = jnp.concatenate([jnp.where(real, asg, dump), dump[:1]], axis=0).reshape(-1)
    return (tile_e.astype(jnp.int32), n_valid.astype(jnp.int32), n_active.reshape(1).astype(jnp.int32),
            tok.astype(jnp.int32), dst.astype(jnp.int32))


def _moe_kernel(te_ref, nv_ref, na_ref, tok_ref, dst_ref, h_hbm, wg_ref, wu_ref, wd_ref, o_hbm,
                xbuf, xb, acc, stage, gsem, ssem, *, nf):
    del te_ref, nv_ref
    j, f = pl.program_id(0), pl.program_id(1)
    n_tiles = pl.num_programs(0)
    n_active = na_ref[0]
    active = j < n_active
    slot = j % 2
    other = 1 - slot
    tile_rows = MOE_TM * N_CHUNK
    rows_per_step = MOE_TM // nf
    token = lambda t: pl.ds(pl.multiple_of(t * N_CHUNK, N_CHUNK), N_CHUNK)

    def gather_row(tile, slot_, r):
        tok = tok_ref[tile * MOE_TM + r]
        return pltpu.make_async_copy(h_hbm.at[token(tok)], xbuf.at[slot_, token(r)], gsem.at[slot_])

    def gather_wait(slot_):
        pltpu.make_async_copy(h_hbm.at[pl.ds(0, tile_rows)], xbuf.at[slot_], gsem.at[slot_]).wait()

    def scatter_row(tile, slot_, r):
        row = dst_ref[tile * MOE_TM + r]
        return pltpu.make_async_copy(stage.at[slot_, token(r)], o_hbm.at[token(row)], ssem.at[slot_])

    def scatter_wait(slot_):
        pltpu.make_async_copy(stage.at[slot_], o_hbm.at[pl.ds(0, tile_rows)], ssem.at[slot_]).wait()

    @pl.when(active & (f == 0))
    def _():
        @pl.when(j == 0)
        def _():
            def body(r, _):
                gather_row(0, 0, r).start()
                return 0
            lax.fori_loop(0, MOE_TM, body, 0, unroll=SUBLANES)
            stage[1] = jnp.zeros((tile_rows, LANES), F32)

        gather_wait(slot)
        xb[...] = _load_token_major(xbuf, MOE_TM, (slot,)).astype(BF16)

        @pl.when(j > 0)
        def _():
            scatter_wait(slot)

    @pl.when(active)
    def _():
        part = _swiglu_part(xb[...], wg_ref[0], wu_ref[0], wd_ref[0])
        nxt = jnp.minimum(j + 1, n_tiles - 1)
        prev = jnp.where(j > 0, j - 1, n_tiles)
        for u in range(rows_per_step):
            r = f * rows_per_step + u
            gather_row(nxt, other, r).start()
            scatter_row(prev, other, r).start()

        if nf > 1:
            @pl.when(f == 0)
            def _():
                acc[...] = part

            @pl.when((f > 0) & (f < nf - 1))
            def _():
                acc[...] += part

        @pl.when(f == nf - 1)
        def _():
            _store_token_major(stage, acc[...] + part if nf > 1 else part, (slot,))

        @pl.when((f == nf - 1) & (j == n_active - 1))
        def _():
            gather_wait(other)
            scatter_wait(other)

            def body(r, _):
                scatter_row(j, slot, r).start()
                return 0
            lax.fori_loop(0, MOE_TM, body, 0, unroll=SUBLANES)
            scatter_wait(slot)


def _moe_experts(h, plan, wg, wu, wd, tf):
    tile_e, n_valid, n_active, tok, dst = plan
    n = h.shape[0] // N_CHUNK
    fdim = wg.shape[2]
    n_tiles = tile_e.shape[0]
    nf = fdim // tf

    def wmap(j, f, te, nv, na, tok_, dst_):
        return te[j], jnp.where(j < na[0], f, nf - 1)

    grid_spec = pltpu.PrefetchScalarGridSpec(
        num_scalar_prefetch=5,
        grid=(n_tiles, nf),
        in_specs=[pl.BlockSpec(memory_space=pl.ANY),
                  pl.BlockSpec((1, D, tf), lambda *a: (wmap(*a)[0], 0, wmap(*a)[1])),
                  pl.BlockSpec((1, D, tf), lambda *a: (wmap(*a)[0], 0, wmap(*a)[1])),
                  pl.BlockSpec((1, tf, D), lambda *a: (wmap(*a)[0], wmap(*a)[1], 0))],
        out_specs=pl.BlockSpec(memory_space=pl.ANY),
        scratch_shapes=[pltpu.VMEM((2, MOE_TM * N_CHUNK, LANES), F32),
                        pltpu.VMEM((MOE_TM, D), BF16),
                        pltpu.VMEM((MOE_TM, D), F32),
                        pltpu.VMEM((2, MOE_TM * N_CHUNK, LANES), F32),
                        pltpu.SemaphoreType.DMA((2,)),
                        pltpu.SemaphoreType.DMA((2,))],
    )
    return pl.pallas_call(
        functools.partial(_moe_kernel, nf=nf),
        out_shape=jax.ShapeDtypeStruct(((2 * n + MOE_TM) * N_CHUNK, LANES), F32),
        grid_spec=grid_spec,
        compiler_params=_cparams(("arbitrary", "arbitrary"), disable_bounds_checks=True),
        name="moe_experts",
    )(tile_e, n_valid, n_active, tok, dst, h, wg, wu, wd)


def _moe_combine_kernel(x_ref, o1_ref, o2_ref, route_ref, *rest, final, n_prompt_tiles):
    rest = list(rest)
    gf_ref = rest.pop(0) if final else None
    route = route_ref[...]
    lane = lax.broadcasted_iota(jnp.int32, route.shape, 1)
    w1 = jnp.sum(jnp.where(lane == 2, route, 0.0), -1, keepdims=True)
    w2 = jnp.sum(jnp.where(lane == 3, route, 0.0), -1, keepdims=True)
    rows = x_ref.shape[0]
    o1, o2 = _load_token_major(o1_ref, rows), _load_token_major(o2_ref, rows)
    _final_store(x_ref[...] + (w1 * o1 + w2 * o2), gf_ref, rest, n_prompt_tiles)


def _moe_combine(x, o, route, gf, n_p, tm):
    n = x.shape[0]
    nb = n // tm
    final = gf is not None
    in_specs = [pl.BlockSpec((tm, D), lambda i: (i, 0)),
                pl.BlockSpec((tm * N_CHUNK, LANES), lambda i: (i, 0)),
                pl.BlockSpec((tm * N_CHUNK, LANES), lambda i: (i + nb, 0)),
                pl.BlockSpec((tm, LANES), lambda i: (i, 0))]
    args = [x, o, o, route]
    if final:
        in_specs.append(pl.BlockSpec((1, D), lambda i: (0, 0)))
        args.append(gf)
    out_shape, out_specs = _final_out(n, n_p, tm, final)
    return pl.pallas_call(
        functools.partial(_moe_combine_kernel, final=final, n_prompt_tiles=n_p // tm),
        out_shape=out_shape,
        grid=(nb,),
        in_specs=in_specs,
        out_specs=out_specs,
        compiler_params=_cparams(("arbitrary",)),
        name="moe_combine",
    )(*args)


def _split_shift_cols(a):
    pad = jnp.zeros(a.shape[:-1] + (LORA_PAD - (LORA_W + LORA_A + LORA_G),), a.dtype)
    lead = a[..., 3 * D:3 * D + LORA_W + LORA_A]
    gd = a[..., 3 * D + LORA_W + LORA_A:]
    return a[..., :3 * D], jnp.concatenate([lead, gd, pad], -1)


def _relayout_w_in(w):
    q, k, v = w[:, 0:D], w[:, D:D + 256], w[:, D + 256:D + 512]
    pr = w[:, D + 512:D + 512 + 3360]
    gates = w[:, D + 512 + 3360:]
    rkv, lora = _split_shift_cols(pr)
    return jnp.concatenate([rkv, q, gates, k, v, lora], axis=1).astype(BF16)


def _rope_tables(pos):
    inv = ROPE_THETA ** (-jnp.arange(0, ROT, 2, dtype=F32) / ROT)
    ang = pos.astype(F32)[:, None] * inv[None, :]
    cos, sin = jnp.cos(ang), jnp.sin(ang)
    n = pos.shape[0]
    half = ROT // 2
    one = jnp.ones((n, HD - ROT), F32)
    zero = jnp.zeros((n, HD - half), F32)
    c = jnp.concatenate([cos, cos, one], 1)
    sa = jnp.concatenate([-sin, zero], 1)
    sb = jnp.concatenate([jnp.zeros((n, half), F32), sin, jnp.zeros((n, HD - ROT), F32)], 1)
    tile = lambda a: jnp.concatenate([a, a], 1)
    return tile(c), tile(sa), tile(sb)


def _pair_state_to_heads(s):
    even = s[:, :, :HD, :HD]
    odd = s[:, :, HD:, HD:]
    b = s.shape[0]
    return jnp.stack([even, odd], axis=2).reshape(b, 2 * N_PAIR, HD, HD)


def kernel(x_prompt, x_sample, cache_k_win, cache_v_win, state_wkv, state_shift, norm_mix_g, w_in, w_out, attn_sinks, shift_mu, decay_w0, decay_up, iclr_a0, iclr_up, gate_up, key_kk, key_ka, bonus_rk, lnx_g, lnx_b, norm_ffn_g, ffn_w_gate, ffn_w_up, ffn_w_down, moe_router, moe_w_gate, moe_w_up, moe_w_down, norm_final_g):
    batch, seq, _ = x_prompt.shape
    sb, st, _ = x_sample.shape
    depth = w_in.shape[0]
    n_p, n_s = batch * seq, sb * st
    n = n_p + n_s
    n_buf = cache_k_win.shape[2]
    if depth == 0:
        raise ValueError("depth must be positive")
    tm = next(c for c in (512, 256, 128) if n_p % c == 0 and n_s % c == 0 and seq % c == 0)
    npt = n_p // tm
    tm_in = 2 * tm if n_p % (2 * tm) == 0 and seq % (2 * tm) == 0 else tm
    n_in = -(-n // tm_in) * tm_in
    pad_rows = lambda a: jnp.pad(a, ((0, n_in - n_p - a.shape[0]), (0, 0)))

    x_p, x_s, s_base = x_prompt.reshape(n_p, D), pad_rows(x_sample.transpose(1, 0, 2).reshape(n_s, D)), 0
    rope_p = _rope_tables(jnp.arange(seq))
    rope_s = tuple(pad_rows(t) for t in _rope_tables(jnp.repeat(PAST_LEN + jnp.arange(st), sb)))
    row = lambda a: a.reshape(1, -1)

    new_p, new_s = [], []
    for l in range(depth):
        mu_rkv, mu_lora = _split_shift_cols(row(shift_mu[l]))
        g_up = jnp.concatenate([gate_up[l], jnp.zeros((G_PAD - LORA_G, D), F32)], 0)
        prm = dict(mu_rkv=mu_rkv, mu_lora=mu_lora, w0=row(decay_w0[l]), w_up=decay_up[l],
                   a0=row(iclr_a0[l]), a_up=iclr_up[l], g_up=g_up, k_k=row(key_kk[l]),
                   k_a=row(key_ka[l]), r_k=row(bonus_rk[l]), lnx_g=row(lnx_g[l]), lnx_b=row(lnx_b[l]))
        expand = lambda rows: {**prm, **{k_: _param_pm(prm[k_], rows) for k_ in ("k_k", "k_a", "r_k", "lnx_g", "lnx_b")}}
        prm_p, prm_s = expand(RWKV_CHUNK), expand(sb)
        p_all = _inproj(x_p, x_s, n_p // tm_in if l else 0, n_p // tm_in, n_in,
                        row(norm_mix_g[l]), _relayout_w_in(w_in[l]), rope_p, rope_s, seq, tm_in)

        att_p = _attn_prompt(p_all, attn_sinks[l], batch, seq)
        ps = p_all[n_p:n].reshape(st, sb, N_COLS).transpose(1, 0, 2)
        k_new, v_new = ps[..., C_K:C_K + 256], ps[..., C_V:C_V + 256]
        k_cache = cache_k_win[l].reshape(sb, n_buf, 256)
        v_cache = cache_v_win[l].reshape(sb, n_buf, 256)
        att_s = _attn_sample(ps[..., C_Q:C_Q + D], k_new, v_new, k_cache, v_cache, attn_sinks[l])
        att_s = pad_rows(att_s.transpose(1, 0, 2).reshape(n_s, D))

        rw_p, s_pairs = _rwkv_prompt(p_all, prm_p, batch, seq)
        sh_rkv, sh_lora = _split_shift_cols(state_shift[l])
        r_s, w_s, k_s, v_s, a_s, b_s, g_s = _rwkv_sample_pre(p_all, n_p, sh_rkv, sh_lora, prm_s, sb, st)
        s0 = state_wkv[l].transpose(1, 2, 3, 0)
        y_t, s_fin = _rwkv_sample_scan(s0, r_s.T, w_s.T, k_s.T, v_s.T[:, None, :], a_s.T, b_s.T)
        rw_s = pad_rows(_rwkv_sample_post(y_t[:, 0, :].T, r_s, k_s, v_s, g_s, prm_s, sb))

        is_moe = l % 2 == 1
        last = l == depth - 1
        gf = row(norm_final_g) if last else None
        if is_moe:
            ne = moe_router.shape[-1]
            router = jnp.concatenate([moe_router[l // 2], jnp.zeros((D, LANES - ne), F32)], 1)
            x_mid, h2, route = _outproj(att_p, att_s, rw_p, rw_s, p_all, x_p, x_s, s_base, w_out[l].astype(BF16),
                                        row(norm_ffn_g[l]), router, ne, tm)
            o_exp = _moe_experts(h2, _moe_plan(route, ne), moe_w_gate[l // 2].astype(BF16),
                                 moe_w_up[l // 2].astype(BF16), moe_w_down[l // 2].astype(BF16),
                                 _pick(moe_w_gate.shape[-1], (896, 512, 256, 128)))
            x = _moe_combine(x_mid, o_exp, route, gf, n_p, tm)
        else:
            x_mid, h2 = _outproj(att_p, att_s, rw_p, rw_s, p_all, x_p, x_s, s_base, w_out[l].astype(BF16),
                                 row(norm_ffn_g[l]), None, 0, tm)
            x = _ffn(h2, ffn_w_gate[l // 2].astype(BF16), ffn_w_up[l // 2].astype(BF16),
                     ffn_w_down[l // 2].astype(BF16), x_mid, gf, n_p, tm,
                     _pick(ffn_w_gate.shape[-1], (1408, 512, 256, 128)))
        if not last:
            x_p, x_s, s_base = x, x, npt

        n_win = min(WINDOW, seq)
        tail = lambda rows, c0, w_: jnp.stack(
            [lax.slice(p_all, ((b + 1) * seq - rows, c0), ((b + 1) * seq, c0 + w_)) for b in range(batch)])
        k_p = tail(n_win, C_K, N_KV * HD).reshape(batch, n_win, N_KV, HD)
        v_p = tail(n_win, C_V, N_KV * HD).reshape(batch, n_win, N_KV, HD)
        unsplit = lambda a: jnp.concatenate(
            [a[..., C_RKV:C_RKV + 3 * D], a[..., C_LORA:C_LORA + LORA_W + LORA_A + LORA_G]], -1)
        last_p = jnp.concatenate([tail(1, C_RKV, 3 * D), tail(1, C_LORA, LORA_W + LORA_A + LORA_G)], -1)[:, 0]
        new_p.append((k_p, v_p, _pair_state_to_heads(s_pairs), last_p))
        k_s_win = jnp.concatenate([k_cache, k_new], 1)[:, -n_buf:].reshape(sb, n_buf, N_KV, HD)
        v_s_win = jnp.concatenate([v_cache, v_new], 1)[:, -n_buf:].reshape(sb, n_buf, N_KV, HD)
        new_s.append((k_s_win, v_s_win, s_fin.transpose(3, 0, 1, 2), unsplit(ps[:, -1])))

    y_p, y_s = x
    y_p = y_p.reshape(batch, seq, D)
    y_s = y_s[:n_s].reshape(st, sb, D).transpose(1, 0, 2)
    stk = lambda sts, i: jnp.stack([s[i] for s in sts])
    return (y_p, y_s,
            stk(new_p, 0), stk(new_p, 1), stk(new_p, 2), stk(new_p, 3),
            stk(new_s, 0), stk(new_s, 1), stk(new_s, 2), stk(new_s, 3))
```
